```python
import math
import jax, jax.numpy as jnp
from jax import lax
import numpy as np

D_MODEL = 2048
BATCH = 2
SEQ = 4096
DEPTH = 4
DEC_BATCH = 128
DEC_SEQ = 1
PAST_LEN = 8192
PAGE_SIZE = 128

N_META = 16
N_A_LAYERS = DEPTH // 2
N_B_LAYERS = DEPTH - N_A_LAYERS
RET_HEADS = 8
RET_DK = D_MODEL // RET_HEADS
RET_DV = 2 * D_MODEL // RET_HEADS
RET_CHUNK = 128
MLA_HEADS = D_MODEL // 128
MLA_NOPE = 128
MLA_ROPE = 64
MLA_V = 128
Q_LORA = 512
KV_LORA = 512
D_FF = 11 * D_MODEL // 4
CONV_W = 3
ROPE_BASE = 10000.0
EPS = 1e-6
Q_BLOCK = 128

kernel_name = 'yoco_retention_mla_convffn_step'


def rms(x):
    xf = x.astype(jnp.float32)
    return (xf * lax.rsqrt(jnp.mean(xf * xf, axis=-1, keepdims=True) + EPS)).astype(x.dtype)


def rms_norm(x, g):
    return rms(x) * g


def rope(x, pos):
    d = x.shape[-1]
    inv = ROPE_BASE ** (-jnp.arange(0, d, 2, dtype=jnp.float32) / d)
    ang = pos.astype(jnp.float32)[:, None] * inv[None, :]
    cos = jnp.cos(ang)[:, None, :]
    sin = jnp.sin(ang)[:, None, :]
    xf = x.astype(jnp.float32)
    x1, x2 = xf[..., : d // 2], xf[..., d // 2:]
    return jnp.concatenate([x1 * cos - x2 * sin, x1 * sin + x2 * cos], axis=-1).astype(x.dtype)


def ret_log_gamma():
    return jnp.log1p(-jnp.power(2.0, -5.0 - jnp.arange(RET_HEADS, dtype=jnp.float32)))


def retention_chunk(state, q, k, v):
    C = q.shape[1]
    dt = v.dtype
    lg = ret_log_gamma()
    n = jnp.arange(C, dtype=jnp.float32)
    diff = n[:, None] - n[None, :]
    intra = jnp.where(diff >= 0, jnp.exp(lg[:, None, None] * jnp.maximum(diff, 0.0)), 0.0)
    q_dec = jnp.exp(lg[:, None] * (n[None, :] + 1.0)).T[None, :, :, None].astype(dt)
    k_dec = jnp.exp(lg[:, None] * (C - 1.0 - n[None, :])).T[None, :, :, None].astype(dt)
    s_dec = jnp.exp(lg * C)[None, :, None, None].astype(dt)
    scores = jnp.einsum('bnhk,bmhk->bhnm', q, k) * intra.astype(dt)
    out = jnp.einsum('bhnm,bmhv->bnhv', scores, v) + jnp.einsum('bnhk,bhkv->bnhv', q * q_dec, state)
    new_state = state * s_dec + jnp.einsum('bmhk,bmhv->bhkv', k * k_dec, v)
    return new_state, out


def retention_scan(state, q, k, v, lead):
    B, S = q.shape[:2]
    outs = []
    if lead > 0:
        state, o = retention_chunk(state, q[:, :lead], k[:, :lead], v[:, :lead])
        outs.append(o)
    rest = S - lead
    if rest > 0:
        n = rest // RET_CHUNK

        def split(t):
            return jnp.moveaxis(t[:, lead:].reshape(B, n, RET_CHUNK, t.shape[2], t.shape[3]), 1, 0)

        state, o = lax.scan(lambda st, xs: retention_chunk(st, *xs), state, (split(q), split(k), split(v)))
        outs.append(jnp.moveaxis(o, 0, 1).reshape(B, rest, RET_HEADS, RET_DV))
    return state, jnp.concatenate(outs, axis=1)


def retention_mixer(xn, pos, state, w_qkvg, w_o, lead):
    B, S, _ = xn.shape
    hk = RET_HEADS * RET_DK
    hv = RET_HEADS * RET_DV
    q, k, v, g = jnp.split(xn @ w_qkvg, [hk, 2 * hk, 2 * hk + hv], axis=-1)
    q = rope(q.reshape(B, S, RET_HEADS, RET_DK), pos)
    k = rope(k.reshape(B, S, RET_HEADS, RET_DK), pos) * (RET_DK ** -0.5)
    v = v.reshape(B, S, RET_HEADS, RET_DV)
    state, o = retention_scan(state, q, k, v, lead)
    o = rms(o).reshape(B, S, hv)
    return (jax.nn.silu(g) * o) @ w_o, state


def conv_ffn(xn, conv_state, w_gate, w_up, w_down, conv_w, conv_b):
    S = xn.shape[1]
    gate = xn @ w_gate
    up = xn @ w_up
    ext = jnp.concatenate([conv_state, gate], axis=1)
    conv = ext[:, 0:S] * conv_w[0]
    for i in range(1, CONV_W):
        conv = conv + ext[:, i:i + S] * conv_w[i]
    conv = conv + conv_b
    return (jax.nn.silu(conv) * up) @ w_down, ext[:, -(CONV_W - 1):]


def mla_kv_side(h, pos, kv_in_g, w_dkv, kv_norm_g, w_kr):
    hn = rms_norm(h, kv_in_g)
    c = rms_norm(hn @ w_dkv, kv_norm_g)
    r = rope((hn @ w_kr)[:, :, None, :], pos)[:, :, 0, :]
    return c, r


def mla_query(xn, pos, w_dq, q_norm_g, w_uq):
    B, S, _ = xn.shape
    cq = rms_norm(xn @ w_dq, q_norm_g)
    q = (cq @ w_uq).reshape(B, S, MLA_HEADS, MLA_NOPE + MLA_ROPE)
    return q[..., :MLA_NOPE], rope(q[..., MLA_NOPE:], pos)


def make_prompt_attn(w_uk, w_uv):
    def make(c, r):
        k_nope = jnp.einsum('btc,chd->bthd', c, w_uk)
        v = jnp.einsum('btc,chd->bthd', c, w_uv)
        L = c.shape[1]
        kpos = jnp.arange(L)
        scale = (MLA_NOPE + MLA_ROPE) ** -0.5

        def attend(q_nope, q_rope):
            B = q_nope.shape[0]
            nb = -(-L // Q_BLOCK)
            pad = nb * Q_BLOCK - L

            def blocks(t):
                t = jnp.pad(t, ((0, 0), (0, pad), (0, 0), (0, 0)))
                return jnp.moveaxis(t.reshape(B, nb, Q_BLOCK, t.shape[2], t.shape[3]), 1, 0)

            def block(args):
                i, qn, qr = args
                s = (jnp.einsum('bqhd,bkhd->bhqk', qn, k_nope)
                     + jnp.einsum('bqhd,bkd->bhqk', qr, r)).astype(jnp.float32) * scale
                qpos = i * Q_BLOCK + jnp.arange(Q_BLOCK)
                s = jnp.where(kpos[None, :] <= qpos[:, None], s, -jnp.inf)
                p = jax.nn.softmax(s, axis=-1).astype(v.dtype)
                return jnp.einsum('bhqk,bkhd->bqhd', p, v)

            o = lax.map(block, (jnp.arange(nb), blocks(q_nope), blocks(q_rope)))
            return jnp.moveaxis(o, 0, 1).reshape(B, nb * Q_BLOCK, MLA_HEADS, MLA_V)[:, :L]

        return attend

    return make


def make_sample_attn(past_lat, past_rope, w_uk, w_uv):
    scale = (MLA_NOPE + MLA_ROPE) ** -0.5
    P = past_lat.shape[1]

    def make(c_new, r_new):
        S = c_new.shape[1]

        def attend(q_nope, q_rope):
            q_lat = jnp.einsum('bshd,chd->bshc', q_nope, w_uk)

            def scores(lat, rk):
                return (jnp.einsum('bshc,btc->bhst', q_lat, lat)
                        + jnp.einsum('bshd,btd->bhst', q_rope, rk)).astype(jnp.float32) * scale

            s_new = jnp.where(jnp.tril(jnp.ones((S, S), dtype=bool)), scores(c_new, r_new), -jnp.inf)
            s = jnp.concatenate([scores(past_lat, past_rope), s_new], axis=-1)
            p = jax.nn.softmax(s, axis=-1).astype(c_new.dtype)
            o_lat = (jnp.einsum('bhst,btc->bshc', p[..., :P], past_lat)
                     + jnp.einsum('bhst,btc->bshc', p[..., P:], c_new))
            return jnp.einsum('bshc,chd->bshd', o_lat, w_uv)

        return attend

    return make


def trunk(h, pos, ret_state0, conv_state0, ret_lead, make_attn,
          norm_g, ret_w_qkvg, ret_w_o, ffn_w_gate, ffn_w_up, ffn_w_down, ffn_conv_w, ffn_conv_b,
          kv_in_g, w_dkv, kv_norm_g, w_kr, mla_w_dq, mla_q_norm_g, mla_w_uq, mla_w_o):
    B, S, _ = h.shape
    ret_states, conv_states = [], []
    attend, c_kv, k_r = None, None, None
    for layer in range(DEPTH):
        g = norm_g[layer]
        xn = rms_norm(h, g[0])
        if layer < N_A_LAYERS:
            mix, st = retention_mixer(xn, pos, ret_state0[layer], ret_w_qkvg[layer], ret_w_o[layer], ret_lead)
            ret_states.append(st)
        else:
            j = layer - N_A_LAYERS
            q_nope, q_rope = mla_query(xn, pos, mla_w_dq[j], mla_q_norm_g[j], mla_w_uq[j])
            mix = attend(q_nope, q_rope).reshape(B, S, MLA_HEADS * MLA_V) @ mla_w_o[j]
        h = h + rms_norm(mix, g[1])
        ff, cs = conv_ffn(rms_norm(h, g[2]), conv_state0[layer], ffn_w_gate[layer], ffn_w_up[layer],
                          ffn_w_down[layer], ffn_conv_w[layer], ffn_conv_b[layer])
        conv_states.append(cs)
        h = h + rms_norm(ff, g[3])
        if layer == N_A_LAYERS - 1:
            c_kv, k_r = mla_kv_side(h, pos, kv_in_g, w_dkv, kv_norm_g, w_kr)
            attend = make_attn(c_kv, k_r)
    return h, jnp.stack(ret_states), jnp.stack(conv_states), c_kv, k_r


def setup_inputs(seed: int = 0) -> dict:
    key = jax.random.key(seed)
    ks = jax.random.split(key, 32)
    n_pages = PAST_LEN // PAGE_SIZE
    n_pool = (DEC_BATCH * n_pages * 5) // 4
    f32 = jnp.float32

    def nrm(k, shape, scale):
        return jax.random.normal(k, shape, f32) * scale

    hk = RET_HEADS * RET_DK
    hv = RET_HEADS * RET_DV
    page_table = jax.random.permutation(ks[6], n_pool)[: DEC_BATCH * n_pages].reshape(DEC_BATCH, n_pages).astype(jnp.int32)
    return {
        'x_prompt': nrm(ks[0], (BATCH, SEQ, D_MODEL), 1.0),
        'x_sample': nrm(ks[1], (DEC_BATCH, DEC_SEQ, D_MODEL), 1.0),
        'state_retention': nrm(ks[2], (N_A_LAYERS, DEC_BATCH, RET_HEADS, RET_DK, RET_DV), 0.1),
        'state_conv': nrm(ks[3], (DEPTH, DEC_BATCH, CONV_W - 1, D_FF), 1.0),
        'cache_kv_latent': nrm(ks[4], (n_pool, PAGE_SIZE, KV_LORA), 1.0),
        'cache_k_rope': nrm(ks[5], (n_pool, PAGE_SIZE, MLA_ROPE), 1.0),
        'page_table': page_table,
        'meta_tokens': nrm(ks[7], (N_META, D_MODEL), 1.0),
        'norm_g': 1.0 + nrm(ks[8], (DEPTH, 4, D_MODEL), 0.01),
        'ret_w_qkvg': nrm(ks[9], (N_A_LAYERS, D_MODEL, 2 * hk + 2 * hv), D_MODEL ** -0.5),
        'ret_w_o': nrm(ks[10], (N_A_LAYERS, hv, D_MODEL), hv ** -0.5),
        'ffn_w_gate': nrm(ks[11], (DEPTH, D_MODEL, D_FF), D_MODEL ** -0.5),
        'ffn_w_up': nrm(ks[12], (DEPTH, D_MODEL, D_FF), D_MODEL ** -0.5),
        'ffn_w_down': nrm(ks[13], (DEPTH, D_FF, D_MODEL), D_FF ** -0.5),
        'ffn_conv_w': nrm(ks[14], (DEPTH, CONV_W, D_FF), CONV_W ** -0.5),
        'ffn_conv_b': nrm(ks[15], (DEPTH, D_FF), 0.01),
        'kv_in_g': 1.0 + nrm(ks[16], (D_MODEL,), 0.01),
        'w_dkv': nrm(ks[17], (D_MODEL, KV_LORA), D_MODEL ** -0.5),
        'kv_norm_g': 1.0 + nrm(ks[18], (KV_LORA,), 0.01),
        'w_kr': nrm(ks[19], (D_MODEL, MLA_ROPE), D_MODEL ** -0.5),
        'w_uk': nrm(ks[20], (KV_LORA, MLA_HEADS, MLA_NOPE), KV_LORA ** -0.5),
        'w_uv': nrm(ks[21], (KV_LORA, MLA_HEADS, MLA_V), KV_LORA ** -0.5),
        'mla_w_dq': nrm(ks[22], (N_B_LAYERS, D_MODEL, Q_LORA), D_MODEL ** -0.5),
        'mla_q_norm_g': 1.0 + nrm(ks[23], (N_B_LAYERS, Q_LORA), 0.01),
        'mla_w_uq': nrm(ks[24], (N_B_LAYERS, Q_LORA, MLA_HEADS * (MLA_NOPE + MLA_ROPE)), Q_LORA ** -0.5),
        'mla_w_o': nrm(ks[25], (N_B_LAYERS, MLA_HEADS * MLA_V, D_MODEL), (MLA_HEADS * MLA_V) ** -0.5),
    }


def reference(x_prompt, x_sample, state_retention, state_conv, cache_kv_latent, cache_k_rope, page_table,
              meta_tokens, norm_g, ret_w_qkvg, ret_w_o, ffn_w_gate, ffn_w_up, ffn_w_down, ffn_conv_w, ffn_conv_b,
              kv_in_g, w_dkv, kv_norm_g, w_kr, w_uk, w_uv, mla_w_dq, mla_q_norm_g, mla_w_uq, mla_w_o):
    weights = (norm_g, ret_w_qkvg, ret_w_o, ffn_w_gate, ffn_w_up, ffn_w_down, ffn_conv_w, ffn_conv_b,
               kv_in_g, w_dkv, kv_norm_g, w_kr, mla_w_dq, mla_q_norm_g, mla_w_uq, mla_w_o)
    B = x_prompt.shape[0]
    dt = x_prompt.dtype
    h0 = jnp.concatenate([jnp.broadcast_to(meta_tokens[None].astype(dt), (B, N_META, D_MODEL)), x_prompt], axis=1)
    pos_p = jnp.arange(h0.shape[1])
    ret0 = jnp.zeros((N_A_LAYERS, B, RET_HEADS, RET_DK, RET_DV), dt)
    conv0 = jnp.zeros((DEPTH, B, CONV_W - 1, D_FF), dt)
    h_p, ret_p, conv_p, lat_p, rope_p = trunk(h0, pos_p, ret0, conv0, N_META, make_prompt_attn(w_uk, w_uv), *weights)
    y_prompt = h_p[:, N_META:]
    DB, S = x_sample.shape[:2]
    pos_s = PAST_LEN + jnp.arange(S)
    past_lat = cache_kv_latent[page_table].reshape(DB, -1, KV_LORA)
    past_rope = cache_k_rope[page_table].reshape(DB, -1, MLA_ROPE)
    y_sample, ret_s, conv_s, lat_s, rope_s = trunk(x_sample, pos_s, state_retention, state_conv, S,
                                                   make_sample_attn(past_lat, past_rope, w_uk, w_uv), *weights)
    return (y_prompt, y_sample, ret_p, ret_s, conv_p, conv_s, lat_p, lat_s, rope_p, rope_s)
```

```python
import functools

import jax
import jax.numpy as jnp
from jax import lax
from jax.experimental import pallas as pl
from jax.experimental.pallas import tpu as pltpu

D_MODEL = 2048
SEQ = 4096
DEPTH = 4
PAST_LEN = 8192
PAGE_SIZE = 128
N_META = 16
N_A_LAYERS = DEPTH // 2
RET_HEADS = 8
RET_DK = D_MODEL // RET_HEADS
RET_DV = 2 * D_MODEL // RET_HEADS
RET_CHUNK = 128
MLA_HEADS = D_MODEL // 128
MLA_NOPE = 128
MLA_ROPE = 64
MLA_V = 128
Q_LORA = 512
KV_LORA = 512
D_FF = 11 * D_MODEL // 4
CONV_W = 3
ROPE_BASE = 10000.0
EPS = 1e-6

V7X_VMEM_BYTES = 64 * 1024 * 1024
VMEM_REQUEST_CAP = V7X_VMEM_BYTES - 8 * 1024 * 1024
LANES = 128
SUBLANES = 8
BF16_ROWS = 16
SLAB = 2 * LANES
NEG = -1e30

F32 = jnp.float32
BF16 = jnp.bfloat16


def _cparams(semantics, vmem_bytes):
    limit = int(min(max(vmem_bytes + (6 << 20), 16 << 20), VMEM_REQUEST_CAP))
    return pltpu.CompilerParams(dimension_semantics=semantics, vmem_limit_bytes=limit)


def _divisor(n, cap, mult):
    d = (min(n, cap) // mult) * mult
    while d >= mult:
        if n % d == 0:
            return d
        d -= mult
    return n


def _silu(x):
    return x * (1.0 / (1.0 + jnp.exp(-x)))


def _rms(x):
    return x * lax.rsqrt(jnp.mean(x * x, axis=-1, keepdims=True) + EPS)


def _rms_cast_body(x_ref, g_ref, o_ref):
    o_ref[...] = (_rms(x_ref[...]) * g_ref[...]).astype(o_ref.dtype)


def _rms_cast(x, g, out_dtype=BF16):
    m, d = x.shape
    tm = _divisor(m, 512, BF16_ROWS)
    return pl.pallas_call(
        _rms_cast_body,
        out_shape=jax.ShapeDtypeStruct((m, d), out_dtype),
        grid=(m // tm,),
        in_specs=[pl.BlockSpec((tm, d), lambda i: (i, 0)), pl.BlockSpec((1, d), lambda i: (0, 0))],
        out_specs=pl.BlockSpec((tm, d), lambda i: (i, 0)),
        compiler_params=_cparams(("parallel",), 2 * tm * d * 6 + tm * d * 8),
        name="rms_cast",
    )(x, g.reshape(1, d))


def _resid_norm_body(h_ref, y_ref, gp_ref, gn_ref, ho_ref, xo_ref):
    h = h_ref[...] + _rms(y_ref[...]) * gp_ref[...]
    ho_ref[...] = h
    xo_ref[...] = (_rms(h) * gn_ref[...]).astype(xo_ref.dtype)


def _resid_norm(h, y, g_post, g_next):
    m, d = h.shape
    tm = _divisor(m, 256, BF16_ROWS)
    row = pl.BlockSpec((tm, d), lambda i: (i, 0))
    vec = pl.BlockSpec((1, d), lambda i: (0, 0))
    return pl.pallas_call(
        _resid_norm_body,
        out_shape=(jax.ShapeDtypeStruct((m, d), F32), jax.ShapeDtypeStruct((m, d), BF16)),
        grid=(m // tm,),
        in_specs=[row, row, vec, vec],
        out_specs=(row, row),
        compiler_params=_cparams(("parallel",), 2 * tm * d * 14 + tm * d * 12),
        name="resid_norm",
    )(h, y, g_post.reshape(1, d), g_next.reshape(1, d))


def _mm_body(a_ref, w_ref, o_ref):
    a = a_ref[...].astype(BF16)
    o_ref[...] = jnp.dot(a, w_ref[...], preferred_element_type=F32).astype(o_ref.dtype)


def _mm_slab_body(a_ref, w_ref, s_ref, o_ref, *, rep):
    a = a_ref[...].astype(BF16)
    acc = jnp.dot(a, w_ref[...], preferred_element_type=F32)
    s = s_ref[...]
    o_ref[...] = (acc + jnp.concatenate([s] * rep, axis=1)).astype(o_ref.dtype)


def _matmul(a, w, out_dtype, slab=None, name="matmul"):
    m, k = a.shape
    n = w.shape[1]
    ab, ob = a.dtype.itemsize, jnp.dtype(out_dtype).itemsize
    mult = SLAB if slab is not None else LANES
    budget = 36 << 20
    tn = _divisor(n, 512 if m > 256 else 2048, mult)
    while 4 * k * tn * 2 > budget and tn > mult:
        tn = _divisor(n, tn - mult, mult)

    def est(tm):
        return 2 * (tm * k * ab + k * tn * 2 + tm * tn * ob) + tm * tn * 4 + tm * k * 2

    tm = _divisor(m, 1408, BF16_ROWS)
    while est(tm) > budget and tm > BF16_ROWS:
        tm = _divisor(m, tm - BF16_ROWS, BF16_ROWS)
    in_specs = [pl.BlockSpec((tm, k), lambda i, j: (i, 0)), pl.BlockSpec((k, tn), lambda i, j: (0, j))]
    args = [a, w]
    body = _mm_body
    if slab is not None:
        in_specs.append(pl.BlockSpec((tm, SLAB), lambda i, j: (i, 0)))
        args.append(slab)
        body = functools.partial(_mm_slab_body, rep=tn // SLAB)
    return pl.pallas_call(
        body,
        out_shape=jax.ShapeDtypeStruct((m, n), out_dtype),
        grid=(m // tm, n // tn),
        in_specs=in_specs,
        out_specs=pl.BlockSpec((tm, tn), lambda i, j: (i, j)),
        compiler_params=_cparams(("parallel", "parallel"), est(tm)),
        name=name,
    )(*args)


def _rope_tables(pos, d):
    inv = ROPE_BASE ** (-jnp.arange(0, d, 2, dtype=F32) / d)
    ang = pos.astype(F32)[:, None] * inv[None, :]
    return jnp.cos(ang), jnp.sin(ang)


def _slab_rope_tables(pos):
    cos, sin = _rope_tables(pos, MLA_ROPE)
    n, half = cos.shape
    one = jnp.ones((n, MLA_NOPE), F32)
    z_nope = jnp.zeros((n, MLA_NOPE), F32)
    z_half = jnp.zeros((n, half), F32)
    z_tail = jnp.zeros((n, SLAB - MLA_NOPE - MLA_ROPE), F32)
    c = jnp.concatenate([one, cos, cos, z_tail], axis=1)
    s1 = jnp.concatenate([z_nope, -sin, z_half, z_tail], axis=1)
    s2 = jnp.concatenate([z_nope, z_half, sin, z_tail], axis=1)
    return c, s1, s2


def _slab_rope(x, c, s1, s2):
    half = MLA_ROPE // 2
    return x * c + pltpu.roll(x, SLAB - half, 1) * s1 + pltpu.roll(x, half, 1) * s2


def _ret_rope(x, cos, sin):
    half = RET_DK // 2
    x1, x2 = x[:, :half], x[:, half:]
    return jnp.concatenate([x1 * cos - x2 * sin, x1 * sin + x2 * cos], axis=1)


def _ret_log_gamma():
    return jnp.log1p(-jnp.power(2.0, -5.0 - jnp.arange(RET_HEADS, dtype=F32)))


def _ret_prompt_body(lg_ref, q_ref, k_ref, v_ref, g_ref, cos_ref, sin_ref, o_ref, so_ref, st_ref, *, pad, n_chunks):
    h = pl.program_id(1)
    c = pl.program_id(2)
    chunk = q_ref.shape[0]

    @pl.when(c == 0)
    def _():
        st_ref[...] = jnp.zeros_like(st_ref)

    lg = lg_ref[h]
    lead = jnp.where(c == 0, float(pad), 0.0)
    cos, sin = cos_ref[...], sin_ref[...]
    q = _ret_rope(q_ref[...].astype(F32), cos, sin)
    k = _ret_rope(k_ref[...].astype(F32), cos, sin) * (RET_DK ** -0.5)
    v = v_ref[...]
    n_col = lax.broadcasted_iota(jnp.int32, (chunk, 1), 0).astype(F32)
    n_row = lax.broadcasted_iota(jnp.int32, (1, chunk), 1).astype(F32)
    diff = n_col - n_row
    intra = jnp.where(diff >= 0, jnp.exp(lg * jnp.maximum(diff, 0.0)), 0.0)
    q_dec = jnp.exp(lg * (n_col + 1.0 - lead))
    k_dec = jnp.exp(lg * (chunk - 1.0 - n_col))
    s_dec = jnp.exp(jnp.full((1, 1), lg * (chunk - lead), F32))
    st = st_ref[...]
    scores = lax.dot_general(q.astype(BF16), k.astype(BF16), (((1,), (1,)), ((), ())), preferred_element_type=F32) * intra
    out = jnp.dot(scores.astype(BF16), v, preferred_element_type=F32)
    out = out + jnp.dot((q * q_dec).astype(BF16), st.astype(BF16), preferred_element_type=F32)
    kt = (k * k_dec).T.astype(BF16)
    st_ref[...] = st * s_dec + jnp.dot(kt, v, preferred_element_type=F32)
    g = g_ref[...].astype(F32)
    o_ref[...] = (_silu(g) * _rms(out)).astype(o_ref.dtype)

    @pl.when(c == n_chunks - 1)
    def _():
        so_ref[...] = st_ref[...]


def _ret_prompt(qkvg, cos, sin, pad):
    b, s, _ = qkvg.shape
    chunk = RET_CHUNK
    nc = s // chunk
    kq = RET_HEADS
    body = functools.partial(_ret_prompt_body, pad=pad, n_chunks=nc)
    return pl.pallas_call(
        body,
        out_shape=(jax.ShapeDtypeStruct((b, s, RET_HEADS * RET_DV), BF16),
                   jax.ShapeDtypeStruct((b, RET_HEADS, RET_DK, RET_DV), F32)),
        grid=(b, RET_HEADS, nc),
        in_specs=[
            pl.BlockSpec(memory_space=pltpu.SMEM),
            pl.BlockSpec((None, chunk, RET_DK), lambda bi, h, c: (bi, c, h)),
            pl.BlockSpec((None, chunk, RET_DK), lambda bi, h, c: (bi, c, kq + h)),
            pl.BlockSpec((None, chunk, RET_DV), lambda bi, h, c: (bi, c, kq + h)),
            pl.BlockSpec((None, chunk, RET_DV), lambda bi, h, c: (bi, c, 2 * kq + h)),
            pl.BlockSpec((chunk, RET_DK // 2), lambda bi, h, c: (c, 0)),
            pl.BlockSpec((chunk, RET_DK // 2), lambda bi, h, c: (c, 0)),
        ],
        out_specs=(
            pl.BlockSpec((None, chunk, RET_DV), lambda bi, h, c: (bi, c, h)),
            pl.BlockSpec((None, None, RET_DK, RET_DV), lambda bi, h, c: (bi, h, 0, 0)),
        ),
        scratch_shapes=[pltpu.VMEM((RET_DK, RET_DV), F32)],
        compiler_params=_cparams(("parallel", "parallel", "arbitrary"), 8 << 20),
        name="ret_prompt",
    )(_ret_log_gamma(), qkvg, qkvg, qkvg, qkvg, cos, sin)


def _ret_sample_body(lg_ref, q_ref, k_ref, v_ref, g_ref, cos_ref, sin_ref, st_ref, o_ref, so_ref):
    h = pl.program_id(1)
    bt = q_ref.shape[0]
    gamma = jnp.exp(jnp.full((1, 1), lg_ref[h], F32))
    cos, sin = cos_ref[...], sin_ref[...]
    q = _ret_rope(q_ref[...], cos, sin)
    k = _ret_rope(k_ref[...], cos, sin) * (RET_DK ** -0.5)
    v = v_ref[...]
    qk = jnp.sum(q * k, axis=-1, keepdims=True)
    qg = (q * gamma).astype(BF16)
    eye = lax.broadcasted_iota(jnp.int32, (RET_DK, RET_DK), 0) == lax.broadcasted_iota(jnp.int32, (RET_DK, RET_DK), 1)
    rows = []
    for i in range(bt):
        st = st_ref[i]
        cross = jnp.dot(qg, st.astype(BF16), preferred_element_type=F32)
        rows.append(cross[i:i + 1])
        k_col = jnp.sum(jnp.where(eye, k[i:i + 1], 0.0), axis=1, keepdims=True)
        so_ref[i] = st * gamma + k_col * v[i:i + 1]
    out = qk * v + jnp.concatenate(rows, axis=0)
    o_ref[...] = _silu(g_ref[...]) * _rms(out)


def _ret_sample(qkvg, state, cos, sin, out_buf, layer):
    db = qkvg.shape[0]
    bt = SUBLANES
    kq = RET_HEADS
    n_layers = state.shape[0]
    st_block = (None, bt, None, RET_DK, RET_DV)
    in_specs = [
        pl.BlockSpec(memory_space=pltpu.SMEM),
        pl.BlockSpec((bt, RET_DK), lambda bi, h: (bi, h)),
        pl.BlockSpec((bt, RET_DK), lambda bi, h: (bi, kq + h)),
        pl.BlockSpec((bt, RET_DV), lambda bi, h: (bi, kq + h)),
        pl.BlockSpec((bt, RET_DV), lambda bi, h: (bi, 2 * kq + h)),
        pl.BlockSpec((1, RET_DK // 2), lambda bi, h: (0, 0)),
        pl.BlockSpec((1, RET_DK // 2), lambda bi, h: (0, 0)),
        pl.BlockSpec(st_block, lambda bi, h: (layer, bi, h, 0, 0)),
    ]
    args = [_ret_log_gamma(), qkvg, qkvg, qkvg, qkvg, cos, sin, state]
    aliases = {}
    body = _ret_sample_body
    if out_buf is not None:
        in_specs.append(pl.BlockSpec(memory_space=pl.ANY))
        args.append(out_buf)
        aliases = {len(args) - 1: 1}
        body = lambda *refs: _ret_sample_body(*refs[:8], *refs[9:])
    return pl.pallas_call(
        body,
        out_shape=(jax.ShapeDtypeStruct((db, RET_HEADS * RET_DV), F32),
                   jax.ShapeDtypeStruct((n_layers, db, RET_HEADS, RET_DK, RET_DV), F32)),
        grid=(db // bt, RET_HEADS),
        in_specs=in_specs,
        out_specs=(
            pl.BlockSpec((bt, RET_DV), lambda bi, h: (bi, h)),
            pl.BlockSpec(st_block, lambda bi, h: (layer, bi, h, 0, 0)),
        ),
        input_output_aliases=aliases,
        compiler_params=_cparams(("parallel", "parallel"), 4 * bt * RET_DK * RET_DV * 4 + (4 << 20)),
        name="ret_sample",
    )(*args)


def _conv_act(g2, g1, g0, up, cw_ref, cb_ref):
    conv = g2 * cw_ref[0:1, :] + g1 * cw_ref[1:2, :] + g0 * cw_ref[2:3, :] + cb_ref[...]
    return _silu(conv) * up


def _ffn1_seq_body(x_ref, wg_ref, wu_ref, cw_ref, cb_ref, act_ref, cs_ref, carry_ref):
    i = pl.program_id(1)
    j = pl.program_id(2)
    x = x_ref[...]
    gate = jnp.dot(x, wg_ref[...], preferred_element_type=F32)
    up = jnp.dot(x, wu_ref[...], preferred_element_type=F32)
    tm = gate.shape[0]
    head = 2 * SUBLANES
    act = _conv_act(pltpu.roll(gate, 2, 0), pltpu.roll(gate, 1, 0), gate, up, cw_ref, cb_ref)
    act_ref[head:, :] = act[head:].astype(act_ref.dtype)
    @pl.when(i == 0)
    def _():
        carry_ref[j] = jnp.zeros(carry_ref.shape[1:], F32)

    win = jnp.concatenate([carry_ref[j], gate[:head]], axis=0)
    w1 = pltpu.roll(win, 1, 0)[SUBLANES:]
    w2 = pltpu.roll(win, 2, 0)[SUBLANES:]
    act_ref[:head, :] = _conv_act(w2, w1, gate[:head], up[:head], cw_ref, cb_ref).astype(act_ref.dtype)
    tail = gate[tm - SUBLANES:]
    carry_ref[j] = tail
    cs_ref[...] = tail


def _ffn1_seq(x, w_gate, w_up, conv_w, conv_b, n_seq):
    m, k = x.shape
    f = w_gate.shape[1]
    s = m // n_seq
    tm = _divisor(s, 1408, BF16_ROWS)
    tn = _divisor(f, 512, LANES)
    ni, nj = s // tm, f // tn
    est = 2 * (tm * k * 2 + 2 * k * tn * 2 + tm * tn * 2) + 4 * tm * tn * 4
    return pl.pallas_call(
        _ffn1_seq_body,
        out_shape=(jax.ShapeDtypeStruct((m, f), BF16), jax.ShapeDtypeStruct((n_seq, SUBLANES, f), F32)),
        grid=(n_seq, ni, nj),
        in_specs=[
            pl.BlockSpec((tm, k), lambda b, i, j: (b * ni + i, 0)),
            pl.BlockSpec((k, tn), lambda b, i, j: (0, j)),
            pl.BlockSpec((k, tn), lambda b, i, j: (0, j)),
            pl.BlockSpec((CONV_W, tn), lambda b, i, j: (0, j)),
            pl.BlockSpec((1, tn), lambda b, i, j: (0, j)),
        ],
        out_specs=(
            pl.BlockSpec((tm, tn), lambda b, i, j: (b * ni + i, j)),
            pl.BlockSpec((None, SUBLANES, tn), lambda b, i, j: (b, 0, j)),
        ),
        scratch_shapes=[pltpu.VMEM((nj, SUBLANES, tn), F32)],
        compiler_params=_cparams(("arbitrary", "arbitrary", "arbitrary"), est),
        name="ffn1_seq",
    )(x, w_gate, w_up, conv_w, conv_b.reshape(1, f))


def _ffn1_tok_body(x_ref, wg_ref, wu_ref, cw_ref, cb_ref, s0_ref, s1_ref, act_ref, gate_ref):
    x = x_ref[...]
    gate = jnp.dot(x, wg_ref[...], preferred_element_type=F32)
    up = jnp.dot(x, wu_ref[...], preferred_element_type=F32)
    act_ref[...] = _conv_act(s0_ref[...], s1_ref[...], gate, up, cw_ref, cb_ref).astype(act_ref.dtype)
    gate_ref[...] = gate


def _ffn1_tok(x, w_gate, w_up, conv_w, conv_b, conv_state):
    m, k = x.shape
    f = w_gate.shape[1]
    tn = _divisor(f, 1024, LANES)
    nj = f // tn
    cs = conv_state.reshape(m, (CONV_W - 1) * f)
    est = 2 * (m * k * 2 + 2 * k * tn * 2 + 6 * m * tn * 4)
    return pl.pallas_call(
        _ffn1_tok_body,
        out_shape=(jax.ShapeDtypeStruct((m, f), BF16), jax.ShapeDtypeStruct((m, f), F32)),
        grid=(nj,),
        in_specs=[
            pl.BlockSpec((m, k), lambda j: (0, 0)),
            pl.BlockSpec((k, tn), lambda j: (0, j)),
            pl.BlockSpec((k, tn), lambda j: (0, j)),
            pl.BlockSpec((CONV_W, tn), lambda j: (0, j)),
            pl.BlockSpec((1, tn), lambda j: (0, j)),
            pl.BlockSpec((m, tn), lambda j: (0, j)),
            pl.BlockSpec((m, tn), lambda j: (0, nj + j)),
        ],
        out_specs=(pl.BlockSpec((m, tn), lambda j: (0, j)), pl.BlockSpec((m, tn), lambda j: (0, j))),
        compiler_params=_cparams(("parallel",), est),
        name="ffn1_tok",
    )(x, w_gate, w_up, conv_w, conv_b.reshape(1, f), cs, cs)


def _kv_post_body(y_ref, g_ref, c_ref, s1_ref, s2_ref, cf_ref, cb_ref, r_ref):
    y = y_ref[...]
    c = _rms(y[:, :KV_LORA]) * g_ref[...]
    cf_ref[...] = c
    cb_ref[...] = c.astype(cb_ref.dtype)
    r_ref[...] = _slab_rope(y[:, KV_LORA:], c_ref[...], s1_ref[...], s2_ref[...])


def _kv_post(y, g, tables):
    m = y.shape[0]
    t_rows = tables[0].shape[0]
    tm = _divisor(m if t_rows == 1 else t_rows, 512, BF16_ROWS)
    if t_rows == 1:
        tspec = pl.BlockSpec((1, SLAB), lambda i: (0, 0))
    else:
        nt = t_rows // tm
        tspec = pl.BlockSpec((tm, SLAB), lambda i: (i % nt, 0))
    return pl.pallas_call(
        _kv_post_body,
        out_shape=(jax.ShapeDtypeStruct((m, KV_LORA), F32), jax.ShapeDtypeStruct((m, KV_LORA), BF16),
                   jax.ShapeDtypeStruct((m, SLAB), F32)),
        grid=(m // tm,),
        in_specs=[pl.BlockSpec((tm, KV_LORA + SLAB), lambda i: (i, 0)), pl.BlockSpec((1, KV_LORA), lambda i: (0, 0)),
                  tspec, tspec, tspec],
        out_specs=(pl.BlockSpec((tm, KV_LORA), lambda i: (i, 0)), pl.BlockSpec((tm, KV_LORA), lambda i: (i, 0)),
                   pl.BlockSpec((tm, SLAB), lambda i: (i, 0))),
        compiler_params=_cparams(("parallel",), 16 << 20),
        name="kv_post",
    )(y, g.reshape(1, KV_LORA), *tables)


def _attn_prompt_body(q_ref, k_ref, v_ref, c_ref, s1_ref, s2_ref, o_ref, m_ref, l_ref, acc_ref, *, pad, tq):
    qi = pl.program_id(2)
    scale = (MLA_NOPE + MLA_ROPE) ** -0.5
    q = (_slab_rope(q_ref[...], c_ref[...], s1_ref[...], s2_ref[...]) * scale).astype(BF16)
    m_ref[...] = jnp.full_like(m_ref, NEG)
    l_ref[...] = jnp.zeros_like(l_ref)
    acc_ref[...] = jnp.zeros_like(acc_ref)

    def step(ki, masked):
        start = ki * tq if isinstance(ki, int) else pl.multiple_of(ki * tq, tq)
        s = lax.dot_general(q, k_ref[pl.ds(start, tq), :], (((1,), (1,)), ((), ())), preferred_element_type=F32)
        if masked:
            qpos = qi * tq + lax.broadcasted_iota(jnp.int32, (tq, 1), 0)
            kpos = ki * tq + lax.broadcasted_iota(jnp.int32, (1, tq), 1)
            ok = (kpos <= qpos) & ((kpos >= pad) | (qpos < pad))
            s = jnp.where(ok, s, NEG)
        m_old = m_ref[...]
        m_new = jnp.maximum(m_old, jnp.max(s, axis=-1, keepdims=True))
        alpha = jnp.exp(m_old - m_new)
        p = jnp.exp(s - m_new)
        l_ref[...] = alpha * l_ref[...] + jnp.sum(p, axis=-1, keepdims=True)
        acc_ref[...] = alpha * acc_ref[...] + jnp.dot(p.astype(BF16), v_ref[pl.ds(start, tq), :],
                                                     preferred_element_type=F32)
        m_ref[...] = m_new

    step(0, True)

    def mid(ki, carry):
        step(ki, False)
        return carry

    lax.fori_loop(1, qi, mid, 0)

    @pl.when(qi > 0)
    def _():
        step(qi, True)

    o_ref[...] = (acc_ref[...] / l_ref[...]).astype(o_ref.dtype)


def _attn_prompt(q, k, v, tables, pad):
    b, s, _ = q.shape
    tq = _divisor(s, 384, LANES)
    body = functools.partial(_attn_prompt_body, pad=pad, tq=tq)
    tspec = pl.BlockSpec((tq, SLAB), lambda bi, h, qi: (qi, 0))
    est = 2 * (tq * SLAB * 4 + s * SLAB * 2 + s * MLA_V * 2 + 3 * tq * SLAB * 4 + tq * MLA_V * 2) + 8 * tq * tq * 4
    return pl.pallas_call(
        body,
        out_shape=jax.ShapeDtypeStruct((b, s, MLA_HEADS * MLA_V), BF16),
        grid=(b, MLA_HEADS, s // tq),
        in_specs=[
            pl.BlockSpec((None, tq, SLAB), lambda bi, h, qi: (bi, qi, h)),
            pl.BlockSpec((None, s, SLAB), lambda bi, h, qi: (bi, 0, h)),
            pl.BlockSpec((None, s, MLA_V), lambda bi, h, qi: (bi, 0, h)),
            tspec, tspec, tspec,
        ],
        out_specs=pl.BlockSpec((None, tq, MLA_V), lambda bi, h, qi: (bi, qi, h)),
        scratch_shapes=[pltpu.VMEM((tq, 1), F32), pltpu.VMEM((tq, 1), F32), pltpu.VMEM((tq, MLA_V), F32)],
        compiler_params=_cparams(("parallel", "parallel", "arbitrary"), est),
        name="attn_prompt",
    )(q, k, v, *tables)


def _q_absorb_body(q_ref, w_ref, c_ref, s1_ref, s2_ref, ql_ref, qr_ref):
    scale = (MLA_NOPE + MLA_ROPE) ** -0.5
    slab = _slab_rope(q_ref[...], c_ref[...], s1_ref[...], s2_ref[...]) * scale
    qr_ref[...] = slab
    ql_ref[...] = lax.dot_general(slab[:, :MLA_NOPE].astype(BF16), w_ref[...], (((1,), (1,)), ((), ())),
                                  preferred_element_type=F32)


def _q_absorb(q, w_uk2, tables):
    db = q.shape[0]
    tspec = pl.BlockSpec((1, SLAB), lambda h: (0, 0))
    return pl.pallas_call(
        _q_absorb_body,
        out_shape=(jax.ShapeDtypeStruct((db, MLA_HEADS * KV_LORA), F32), jax.ShapeDtypeStruct((db, MLA_HEADS * SLAB), F32)),
        grid=(MLA_HEADS,),
        in_specs=[pl.BlockSpec((db, SLAB), lambda h: (0, h)), pl.BlockSpec((KV_LORA, MLA_NOPE), lambda h: (0, h)),
                  tspec, tspec, tspec],
        out_specs=(pl.BlockSpec((db, KV_LORA), lambda h: (0, h)), pl.BlockSpec((db, SLAB), lambda h: (0, h))),
        compiler_params=_cparams(("parallel",), 8 << 20),
        name="q_absorb",
    )(q, w_uk2, *tables)


def _attn_sample_body(pt_ref, ql_ref, qs_ref, cn_ref, rn_ref, *refs, pages, n_groups):
    lat_refs = refs[:pages]
    rope_refs = refs[pages:2 * pages]
    o_ref, m_ref, l_ref, acc_ref, latb_ref, ropeb_ref = refs[2 * pages:]
    g = pl.program_id(1)
    page = lat_refs[0].shape[0]

    @pl.when(g == 0)
    def _():
        m_ref[...] = jnp.full_like(m_ref, NEG)
        l_ref[...] = jnp.zeros_like(l_ref)
        acc_ref[...] = jnp.zeros_like(acc_ref)

    for i in range(pages):
        latb_ref[i * page:(i + 1) * page, :] = lat_refs[i][...].astype(BF16)
        ropeb_ref[i * page:(i + 1) * page, :] = rope_refs[i][...].astype(BF16)
    ql = ql_ref[...]
    qr = qs_ref[...][:, MLA_NOPE:MLA_NOPE + MLA_ROPE]
    nt = (((1,), (1,)), ((), ()))
    s = lax.dot_general(ql.astype(BF16), latb_ref[...], nt, preferred_element_type=F32)
    s = s + lax.dot_general(qr.astype(BF16), ropeb_ref[...], nt, preferred_element_type=F32)
    m_old = m_ref[...]
    m_new = jnp.maximum(m_old, jnp.max(s, axis=-1, keepdims=True))
    alpha = jnp.exp(m_old - m_new)
    p = jnp.exp(s - m_new)
    l_ref[...] = alpha * l_ref[...] + jnp.sum(p, axis=-1, keepdims=True)
    acc_ref[...] = alpha * acc_ref[...] + jnp.dot(p.astype(BF16), latb_ref[...], preferred_element_type=F32)
    m_ref[...] = m_new

    @pl.when(g == n_groups - 1)
    def _():
        cn = cn_ref[...]
        rn = rn_ref[...][:, MLA_NOPE:MLA_NOPE + MLA_ROPE]
        s_new = jnp.sum(ql * cn, axis=-1, keepdims=True) + jnp.sum(qr * rn, axis=-1, keepdims=True)
        m_old = m_ref[...]
        m_new = jnp.maximum(m_old, s_new)
        alpha = jnp.exp(m_old - m_new)
        p_new = jnp.exp(s_new - m_new)
        l = alpha * l_ref[...] + p_new
        o_ref[...] = (alpha * acc_ref[...] + p_new * cn) / l


def _attn_sample(page_table, cache_lat, cache_rope, q_lat, q_slab, c_new, r_new):
    db, n_pages = page_table.shape
    page = cache_lat.shape[1]
    pages = _divisor(n_pages, 16, 1)
    n_groups = n_pages // pages
    body = functools.partial(_attn_sample_body, pages=pages, n_groups=n_groups)

    def page_spec(width, i):
        return pl.BlockSpec((None, page, width), lambda b, g, pt: (pt[b, g * pages + i], 0, 0))

    in_specs = [
        pl.BlockSpec((None, MLA_HEADS, KV_LORA), lambda b, g, pt: (b, 0, 0)),
        pl.BlockSpec((None, MLA_HEADS, SLAB), lambda b, g, pt: (b, 0, 0)),
        pl.BlockSpec((None, 1, KV_LORA), lambda b, g, pt: (b, 0, 0)),
        pl.BlockSpec((None, 1, SLAB), lambda b, g, pt: (b, 0, 0)),
    ]
    in_specs += [page_spec(KV_LORA, i) for i in range(pages)]
    in_specs += [page_spec(MLA_ROPE, i) for i in range(pages)]
    est = 2 * pages * page * (KV_LORA + LANES) * 4 + pages * page * (KV_LORA + LANES) * 2 + (4 << 20)
    return pl.pallas_call(
        body,
        out_shape=jax.ShapeDtypeStruct((db, MLA_HEADS, KV_LORA), F32),
        grid_spec=pltpu.PrefetchScalarGridSpec(
            num_scalar_prefetch=1,
            grid=(db, n_groups),
            in_specs=in_specs,
            out_specs=pl.BlockSpec((None, MLA_HEADS, KV_LORA), lambda b, g, pt: (b, 0, 0)),
            scratch_shapes=[
                pltpu.VMEM((MLA_HEADS, 1), F32), pltpu.VMEM((MLA_HEADS, 1), F32), pltpu.VMEM((MLA_HEADS, KV_LORA), F32),
                pltpu.VMEM((pages * page, KV_LORA), BF16), pltpu.VMEM((pages * page, MLA_ROPE), BF16),
            ],
        ),
        compiler_params=_cparams(("parallel", "arbitrary"), est),
        name="attn_sample",
    )(page_table, q_lat, q_slab, c_new, r_new, *([cache_lat] * pages), *([cache_rope] * pages))


def _head_out_body(o_ref, w_ref, y_ref):
    y_ref[...] = jnp.dot(o_ref[...].astype(BF16), w_ref[...], preferred_element_type=F32).astype(y_ref.dtype)


def _head_out(o_lat, w_uv2):
    db = o_lat.shape[0]
    return pl.pallas_call(
        _head_out_body,
        out_shape=jax.ShapeDtypeStruct((db, MLA_HEADS * MLA_V), BF16),
        grid=(MLA_HEADS,),
        in_specs=[pl.BlockSpec((db, KV_LORA), lambda h: (0, h)), pl.BlockSpec((KV_LORA, MLA_V), lambda h: (0, h))],
        out_specs=pl.BlockSpec((db, MLA_V), lambda h: (0, h)),
        compiler_params=_cparams(("parallel",), 4 << 20),
        name="head_out",
    )(o_lat, w_uv2)


def _prep_weights(ret_w_qkvg, ret_w_o, ffn_w_gate, ffn_w_up, ffn_w_down, w_dkv, w_kr, w_uk, w_uv, mla_w_dq, mla_w_uq,
                  mla_w_o):
    tail = SLAB - MLA_NOPE - MLA_ROPE
    w_kv = jnp.concatenate([w_dkv, jnp.zeros((D_MODEL, MLA_NOPE), F32), w_kr, jnp.zeros((D_MODEL, tail), F32)], axis=1)
    w_uk_ext = jnp.pad(w_uk, ((0, 0), (0, 0), (0, SLAB - MLA_NOPE))).reshape(KV_LORA, MLA_HEADS * SLAB)
    n_b = mla_w_uq.shape[0]
    w_uq_ext = jnp.pad(mla_w_uq.reshape(n_b, Q_LORA, MLA_HEADS, MLA_NOPE + MLA_ROPE),
                       ((0, 0), (0, 0), (0, 0), (0, tail))).reshape(n_b, Q_LORA, MLA_HEADS * SLAB)
    return dict(
        qkvg=ret_w_qkvg.astype(BF16), ret_o=ret_w_o.astype(BF16), gate=ffn_w_gate.astype(BF16), up=ffn_w_up.astype(BF16),
        down=ffn_w_down.astype(BF16), kv=w_kv.astype(BF16), uk_ext=w_uk_ext.astype(BF16),
        uk2=w_uk.reshape(KV_LORA, MLA_HEADS * MLA_NOPE).astype(BF16), uv2=w_uv.reshape(KV_LORA, MLA_HEADS * MLA_V).astype(BF16),
        dq=mla_w_dq.astype(BF16), uq_ext=w_uq_ext.astype(BF16), mla_o=mla_w_o.astype(BF16),
    )


def _trunk(h, w, norm_g, ffn_conv_w, ffn_conv_b, kv_in_g, kv_norm_g, mla_q_norm_g, ret_mixer, ffn1, kv_tables, attend):
    conv_states = []
    c_f32 = r_slab = kv_ctx = None
    xn = _rms_cast(h, norm_g[0, 0])
    for layer in range(DEPTH):
        g = norm_g[layer]
        if layer < N_A_LAYERS:
            qkvg = _matmul(xn, w["qkvg"][layer], ret_mixer.qkvg_dtype, name="qkvg")
            gated = ret_mixer(layer, qkvg)
            mix = _matmul(gated, w["ret_o"][layer], F32, name="ret_o")
        else:
            j = layer - N_A_LAYERS
            cq = _matmul(xn, w["dq"][j], F32, name="dq")
            cqn = _rms_cast(cq, mla_q_norm_g[j])
            q = _matmul(cqn, w["uq_ext"][j], F32, name="uq")
            o = attend(q, kv_ctx)
            mix = _matmul(o, w["mla_o"][j], F32, name="mla_o")
        h, xn = _resid_norm(h, mix, g[1], g[2])
        act, cs = ffn1(layer, xn, w["gate"][layer], w["up"][layer], ffn_conv_w[layer], ffn_conv_b[layer])
        conv_states.append(cs)
        ff = _matmul(act, w["down"][layer], F32, name="down")
        g_next = norm_g[layer + 1, 0] if layer + 1 < DEPTH else g[3]
        h, xn = _resid_norm(h, ff, g[3], g_next)
        if layer == N_A_LAYERS - 1:
            hn = _rms_cast(h, kv_in_g)
            y = _matmul(hn, w["kv"], F32, name="kv_down")
            c_f32, c_bf16, r_slab = _kv_post(y, kv_norm_g, kv_tables)
            kv_ctx = (c_f32, c_bf16, r_slab)
    return h, conv_states, c_f32, r_slab


class _PromptRetention:
    qkvg_dtype = BF16

    def __init__(self, n_seq, pad, cos, sin):
        self.n_seq, self.pad, self.cos, self.sin = n_seq, pad, cos, sin
        self.states = []

    def __call__(self, layer, qkvg):
        m, n = qkvg.shape
        gated, state = _ret_prompt(qkvg.reshape(self.n_seq, m // self.n_seq, n), self.cos, self.sin, self.pad)
        self.states.append(state)
        return gated.reshape(m, -1)


class _SampleRetention:
    qkvg_dtype = F32

    def __init__(self, state, cos, sin):
        self.state, self.cos, self.sin = state, cos, sin
        self.out = None

    def __call__(self, layer, qkvg):
        gated, self.out = _ret_sample(qkvg, self.state, self.cos, self.sin, self.out, layer)
        return gated


def kernel(x_prompt, x_sample, state_retention, state_conv, cache_kv_latent, cache_k_rope, page_table, meta_tokens, norm_g,
           ret_w_qkvg, ret_w_o, ffn_w_gate, ffn_w_up, ffn_w_down, ffn_conv_w, ffn_conv_b, kv_in_g, w_dkv, kv_norm_g, w_kr,
           w_uk, w_uv, mla_w_dq, mla_q_norm_g, mla_w_uq, mla_w_o):
    w = _prep_weights(ret_w_qkvg, ret_w_o, ffn_w_gate, ffn_w_up, ffn_w_down, w_dkv, w_kr, w_uk, w_uv, mla_w_dq, mla_w_uq,
                      mla_w_o)
    shared = (norm_g, ffn_conv_w, ffn_conv_b, kv_in_g, kv_norm_g, mla_q_norm_g)

    b, seq, d = x_prompt.shape
    pad = RET_CHUNK - N_META
    s_pad = pad + N_META + seq
    h0 = jnp.concatenate([jnp.zeros((b, pad, d), F32), jnp.broadcast_to(meta_tokens[None], (b, N_META, d)), x_prompt], axis=1)
    pos_p = jnp.arange(s_pad) - pad
    cos_p, sin_p = _rope_tables(pos_p, RET_DK)
    tables_p = _slab_rope_tables(pos_p)
    ret_p = _PromptRetention(b, pad, cos_p, sin_p)

    def ffn1_p(layer, xn, wg, wu, cw, cb):
        act, tail = _ffn1_seq(xn, wg, wu, cw, cb, b)
        return act, tail[:, SUBLANES - (CONV_W - 1):, :]

    def attend_p(q, kv_ctx):
        if "k" not in attend_p.cache:
            _, c_bf16, r_slab = kv_ctx
            attend_p.cache["k"] = _matmul(c_bf16, w["uk_ext"], BF16, slab=r_slab, name="k_up").reshape(b, s_pad, -1)
            attend_p.cache["v"] = _matmul(c_bf16, w["uv2"], BF16, name="v_up").reshape(b, s_pad, -1)
        o = _attn_prompt(q.reshape(b, s_pad, -1), attend_p.cache["k"], attend_p.cache["v"], tables_p, pad)
        return o.reshape(b * s_pad, -1)

    attend_p.cache = {}
    h_p, conv_p, lat_p, rslab_p = _trunk(h0.reshape(b * s_pad, d), w, *shared, ret_p, ffn1_p, tables_p, attend_p)
    y_prompt = h_p.reshape(b, s_pad, d)[:, pad + N_META:]
    lat_prompt = lat_p.reshape(b, s_pad, KV_LORA)[:, pad:]
    rope_prompt = rslab_p.reshape(b, s_pad, SLAB)[:, pad:, MLA_NOPE:MLA_NOPE + MLA_ROPE]

    db = x_sample.shape[0]
    pos_s = jnp.full((1,), PAST_LEN)
    cos_s, sin_s = _rope_tables(pos_s, RET_DK)
    tables_s = _slab_rope_tables(pos_s)
    ret_s = _SampleRetention(state_retention, cos_s, sin_s)
    gates = []

    def ffn1_s(layer, xn, wg, wu, cw, cb):
        act, gate = _ffn1_tok(xn, wg, wu, cw, cb, state_conv[layer])
        return act, jnp.stack([state_conv[layer][:, 1], gate], axis=1)

    def attend_s(q, kv_ctx):
        c_f32, _, r_slab = kv_ctx
        q_lat, q_slab = _q_absorb(q, w["uk2"], tables_s)
        o_lat = _attn_sample(page_table, cache_kv_latent, cache_k_rope, q_lat.reshape(db, MLA_HEADS, KV_LORA),
                             q_slab.reshape(db, MLA_HEADS, SLAB), c_f32.reshape(db, 1, KV_LORA), r_slab.reshape(db, 1, SLAB))
        return _head_out(o_lat.reshape(db, MLA_HEADS * KV_LORA), w["uv2"])

    h_s, conv_s, lat_s, rslab_s = _trunk(x_sample.reshape(db, d), w, *shared, ret_s, ffn1_s, tables_s, attend_s)

    return (
        y_prompt,
        h_s.reshape(db, 1, d),
        jnp.stack(ret_p.states),
        ret_s.out,
        jnp.stack(conv_p),
        jnp.stack(conv_s),
        lat_prompt,
        lat_s.reshape(db, 1, KV_LORA),
        rope_prompt,
        rslab_s[:, MLA_NOPE:MLA_NOPE + MLA_ROPE].reshape(db, 1, MLA_ROPE),
    )
```

```python
import functools

import jax
import jax.numpy as jnp
from jax import lax
from jax.experimental import pallas as pl
from jax.experimental.pallas import tpu as pltpu

D_MODEL = 2048
SEQ = 4096
DEPTH = 4
PAST_LEN = 8192
PAGE_SIZE = 128
N_META = 16
N_A_LAYERS = DEPTH // 2
RET_HEADS = 8
RET_DK = D_MODEL // RET_HEADS
RET_DV = 2 * D_MODEL // RET_HEADS
RET_CHUNK = 128
MLA_HEADS = D_MODEL // 128
MLA_NOPE = 128
MLA_ROPE = 64
MLA_V = 128
Q_LORA = 512
KV_LORA = 512
D_FF = 11 * D_MODEL // 4
CONV_W = 3
ROPE_BASE = 10000.0
EPS = 1e-6

V7X_VMEM_BYTES = 64 * 1024 * 1024
VMEM_REQUEST_CAP = V7X_VMEM_BYTES - 8 * 1024 * 1024
LANES = 128
SUBLANES = 8
BF16_ROWS = 16
SLAB = 2 * LANES
NEG = -1e30

F32 = jnp.float32
BF16 = jnp.bfloat16


def _cparams(semantics, vmem_bytes):
    limit = int(min(max(vmem_bytes + (6 << 20), 16 << 20), VMEM_REQUEST_CAP))
    return pltpu.CompilerParams(dimension_semantics=semantics, vmem_limit_bytes=limit)


def _divisor(n, cap, mult):
    d = (min(n, cap) // mult) * mult
    while d >= mult:
        if n % d == 0:
            return d
        d -= mult
    return n


def _silu(x):
    return x * (1.0 / (1.0 + jnp.exp(-x)))


def _rms(x):
    return x * lax.rsqrt(jnp.mean(x * x, axis=-1, keepdims=True) + EPS)


def _rms_cast_body(x_ref, g_ref, o_ref):
    o_ref[...] = (_rms(x_ref[...]) * g_ref[...]).astype(o_ref.dtype)


def _rms_cast(x, g, out_dtype=BF16):
    m, d = x.shape
    tm = _divisor(m, 512, BF16_ROWS)
    return pl.pallas_call(
        _rms_cast_body,
        out_shape=jax.ShapeDtypeStruct((m, d), out_dtype),
        grid=(m // tm,),
        in_specs=[pl.BlockSpec((tm, d), lambda i: (i, 0)), pl.BlockSpec((1, d), lambda i: (0, 0))],
        out_specs=pl.BlockSpec((tm, d), lambda i: (i, 0)),
        compiler_params=_cparams(("parallel",), 2 * tm * d * 6 + tm * d * 8),
        name="rms_cast",
    )(x, g.reshape(1, d))


def _resid_norm_body(h_ref, y_ref, gp_ref, gn_ref, ho_ref, xo_ref):
    h = h_ref[...] + _rms(y_ref[...]) * gp_ref[...]
    ho_ref[...] = h
    xo_ref[...] = (_rms(h) * gn_ref[...]).astype(xo_ref.dtype)


def _resid_norm(h, y, g_post, g_next):
    m, d = h.shape
    tm = _divisor(m, 256, BF16_ROWS)
    row = pl.BlockSpec((tm, d), lambda i: (i, 0))
    vec = pl.BlockSpec((1, d), lambda i: (0, 0))
    return pl.pallas_call(
        _resid_norm_body,
        out_shape=(jax.ShapeDtypeStruct((m, d), F32), jax.ShapeDtypeStruct((m, d), BF16)),
        grid=(m // tm,),
        in_specs=[row, row, vec, vec],
        out_specs=(row, row),
        compiler_params=_cparams(("parallel",), 2 * tm * d * 14 + tm * d * 12),
        name="resid_norm",
    )(h, y, g_post.reshape(1, d), g_next.reshape(1, d))


def _mm_body(a_ref, w_ref, o_ref):
    a = a_ref[...].astype(BF16)
    o_ref[...] = jnp.dot(a, w_ref[...].astype(BF16), preferred_element_type=F32).astype(o_ref.dtype)


def _mm_slab_body(a_ref, w_ref, s_ref, o_ref, *, rep):
    a = a_ref[...].astype(BF16)
    acc = jnp.dot(a, w_ref[...].astype(BF16), preferred_element_type=F32)
    s = s_ref[...]
    o_ref[...] = (acc + jnp.concatenate([s] * rep, axis=1)).astype(o_ref.dtype)


def _mm_tiles(m, k, n, ab, wb, ob, mult, budget):
    def est(tm, tn):
        return 2 * (tm * k * ab + k * tn * wb + tm * tn * ob) + tm * tn * 4 + tm * k * 2 + k * tn * 2

    tn_min = min(n, max(mult, SLAB))
    tm = _divisor(m, 1408, BF16_ROWS)
    while True:
        tn = _divisor(n, 512 if m > 256 else 2048, mult)
        while est(tm, tn) > budget and tn > tn_min:
            tn = _divisor(n, tn - mult, mult)
        if est(tm, tn) <= budget or tm <= BF16_ROWS:
            return tm, tn, est(tm, tn)
        tm = _divisor(m, tm - BF16_ROWS, BF16_ROWS)


def _matmul(a, w, out_dtype, layer=None, slab=None, name="matmul"):
    m, k = a.shape
    n = w.shape[-1]
    mult = SLAB if slab is not None else LANES
    tm, tn, est = _mm_tiles(m, k, n, a.dtype.itemsize, w.dtype.itemsize, jnp.dtype(out_dtype).itemsize, mult, 46 << 20)
    if w.ndim == 3:
        w_spec = pl.BlockSpec((None, k, tn), lambda i, j: (layer, 0, j))
    else:
        w_spec = pl.BlockSpec((k, tn), lambda i, j: (0, j))
    in_specs = [pl.BlockSpec((tm, k), lambda i, j: (i, 0)), w_spec]
    args = [a, w]
    body = _mm_body
    if slab is not None:
        in_specs.append(pl.BlockSpec((tm, SLAB), lambda i, j: (i, 0)))
        args.append(slab)
        body = functools.partial(_mm_slab_body, rep=tn // SLAB)
    return pl.pallas_call(
        body,
        out_shape=jax.ShapeDtypeStruct((m, n), out_dtype),
        grid=(m // tm, n // tn),
        in_specs=in_specs,
        out_specs=pl.BlockSpec((tm, tn), lambda i, j: (i, j)),
        compiler_params=_cparams(("parallel", "parallel"), est),
        name=name,
    )(*args)


def _rope_tables(pos, d):
    inv = ROPE_BASE ** (-jnp.arange(0, d, 2, dtype=F32) / d)
    ang = pos.astype(F32)[:, None] * inv[None, :]
    return jnp.cos(ang), jnp.sin(ang)


def _slab_rope_tables(pos):
    cos, sin = _rope_tables(pos, MLA_ROPE)
    n, half = cos.shape
    one = jnp.ones((n, MLA_NOPE), F32)
    z_nope = jnp.zeros((n, MLA_NOPE), F32)
    z_half = jnp.zeros((n, half), F32)
    z_tail = jnp.zeros((n, SLAB - MLA_NOPE - MLA_ROPE), F32)
    c = jnp.concatenate([one, cos, cos, z_tail], axis=1)
    s1 = jnp.concatenate([z_nope, -sin, z_half, z_tail], axis=1)
    s2 = jnp.concatenate([z_nope, z_half, sin, z_tail], axis=1)
    return c, s1, s2


def _slab_rope(x, c, s1, s2):
    half = MLA_ROPE // 2
    return x * c + pltpu.roll(x, SLAB - half, 1) * s1 + pltpu.roll(x, half, 1) * s2


def _ret_rope(x, cos, sin):
    half = RET_DK // 2
    x1, x2 = x[:, :half], x[:, half:]
    return jnp.concatenate([x1 * cos - x2 * sin, x1 * sin + x2 * cos], axis=1)


def _ret_log_gamma():
    return jnp.log1p(-jnp.power(2.0, -5.0 - jnp.arange(RET_HEADS, dtype=F32)))


def _ret_prompt_body(lg_ref, q_ref, k_ref, v_ref, g_ref, cos_ref, sin_ref, o_ref, so_ref, st_ref, *, pad, n_chunks, hp):
    hg = pl.program_id(1)
    c = pl.program_id(2)
    chunk = q_ref.shape[0]

    @pl.when(c == 0)
    def _():
        st_ref[...] = jnp.zeros_like(st_ref)

    lead = jnp.where(c == 0, float(pad), 0.0)
    cos, sin = cos_ref[...], sin_ref[...]
    n_col = lax.broadcasted_iota(jnp.int32, (chunk, 1), 0).astype(F32)
    n_row = lax.broadcasted_iota(jnp.int32, (1, chunk), 1).astype(F32)
    diff = n_col - n_row
    for i in range(hp):
        lg = lg_ref[hg * hp + i]
        q = _ret_rope(q_ref[:, i * RET_DK:(i + 1) * RET_DK].astype(F32), cos, sin)
        k = _ret_rope(k_ref[:, i * RET_DK:(i + 1) * RET_DK].astype(F32), cos, sin) * (RET_DK ** -0.5)
        v = v_ref[:, i * RET_DV:(i + 1) * RET_DV]
        intra = jnp.where(diff >= 0, jnp.exp(lg * jnp.maximum(diff, 0.0)), 0.0)
        q_dec = jnp.exp(lg * (n_col + 1.0 - lead))
        k_dec = jnp.exp(lg * (chunk - 1.0 - n_col))
        s_dec = jnp.exp(jnp.full((1, 1), lg * (chunk - lead), F32))
        st = st_ref[i]
        scores = lax.dot_general(q.astype(BF16), k.astype(BF16), (((1,), (1,)), ((), ())),
                                 preferred_element_type=F32) * intra
        out = jnp.dot(scores.astype(BF16), v, preferred_element_type=F32)
        out = out + jnp.dot((q * q_dec).astype(BF16), st.astype(BF16), preferred_element_type=F32)
        kt = (k * k_dec).T.astype(BF16)
        st_ref[i] = st * s_dec + jnp.dot(kt, v, preferred_element_type=F32)
        g = g_ref[:, i * RET_DV:(i + 1) * RET_DV].astype(F32)
        o_ref[:, i * RET_DV:(i + 1) * RET_DV] = (_silu(g) * _rms(out)).astype(o_ref.dtype)

    @pl.when(c == n_chunks - 1)
    def _():
        so_ref[...] = st_ref[...]


def _ret_prompt(qkvg, cos, sin, pad):
    b, s, _ = qkvg.shape
    chunk = RET_CHUNK
    nc = s // chunk
    hp = _divisor(RET_HEADS, 4, 1)
    ng = RET_HEADS // hp
    body = functools.partial(_ret_prompt_body, pad=pad, n_chunks=nc, hp=hp)
    return pl.pallas_call(
        body,
        out_shape=(jax.ShapeDtypeStruct((b, s, RET_HEADS * RET_DV), BF16),
                   jax.ShapeDtypeStruct((b, RET_HEADS, RET_DK, RET_DV), F32)),
        grid=(b, ng, nc),
        in_specs=[
            pl.BlockSpec(memory_space=pltpu.SMEM),
            pl.BlockSpec((None, chunk, hp * RET_DK), lambda bi, h, c: (bi, c, h)),
            pl.BlockSpec((None, chunk, hp * RET_DK), lambda bi, h, c: (bi, c, ng + h)),
            pl.BlockSpec((None, chunk, hp * RET_DV), lambda bi, h, c: (bi, c, ng + h)),
            pl.BlockSpec((None, chunk, hp * RET_DV), lambda bi, h, c: (bi, c, 2 * ng + h)),
            pl.BlockSpec((chunk, RET_DK // 2), lambda bi, h, c: (c, 0)),
            pl.BlockSpec((chunk, RET_DK // 2), lambda bi, h, c: (c, 0)),
        ],
        out_specs=(
            pl.BlockSpec((None, chunk, hp * RET_DV), lambda bi, h, c: (bi, c, h)),
            pl.BlockSpec((None, hp, RET_DK, RET_DV), lambda bi, h, c: (bi, h, 0, 0)),
        ),
        scratch_shapes=[pltpu.VMEM((hp, RET_DK, RET_DV), F32)],
        compiler_params=_cparams(("parallel", "parallel", "arbitrary"), 16 << 20),
        name="ret_prompt",
    )(_ret_log_gamma(), qkvg, qkvg, qkvg, qkvg, cos, sin)


def _ret_sample_body(lg_ref, q_ref, k_ref, v_ref, g_ref, cos_ref, sin_ref, st_ref, o_ref, so_ref):
    h = pl.program_id(1)
    bt = q_ref.shape[0]
    gamma = jnp.exp(jnp.full((1, 1), lg_ref[h], F32))
    cos, sin = cos_ref[...], sin_ref[...]
    q = _ret_rope(q_ref[...], cos, sin)
    k = _ret_rope(k_ref[...], cos, sin) * (RET_DK ** -0.5)
    v = v_ref[...]
    qk = jnp.sum(q * k, axis=-1, keepdims=True)
    qg = (q * gamma).astype(BF16)
    eye = lax.broadcasted_iota(jnp.int32, (RET_DK, RET_DK), 0) == lax.broadcasted_iota(jnp.int32, (RET_DK, RET_DK), 1)
    rows = []
    for i in range(bt):
        st = st_ref[i]
        cross = jnp.dot(qg, st.astype(BF16), preferred_element_type=F32)
        rows.append(cross[i:i + 1])
        k_col = jnp.sum(jnp.where(eye, k[i:i + 1], 0.0), axis=1, keepdims=True)
        so_ref[i] = st * gamma + k_col * v[i:i + 1]
    out = qk * v + jnp.concatenate(rows, axis=0)
    o_ref[...] = _silu(g_ref[...]) * _rms(out)


def _ret_sample(qkvg, state, cos, sin, out_buf, layer):
    db = qkvg.shape[0]
    bt = SUBLANES
    kq = RET_HEADS
    n_layers = state.shape[0]
    st_block = (None, bt, None, RET_DK, RET_DV)
    in_specs = [
        pl.BlockSpec(memory_space=pltpu.SMEM),
        pl.BlockSpec((bt, RET_DK), lambda bi, h: (bi, h)),
        pl.BlockSpec((bt, RET_DK), lambda bi, h: (bi, kq + h)),
        pl.BlockSpec((bt, RET_DV), lambda bi, h: (bi, kq + h)),
        pl.BlockSpec((bt, RET_DV), lambda bi, h: (bi, 2 * kq + h)),
        pl.BlockSpec((1, RET_DK // 2), lambda bi, h: (0, 0)),
        pl.BlockSpec((1, RET_DK // 2), lambda bi, h: (0, 0)),
        pl.BlockSpec(st_block, lambda bi, h: (layer, bi, h, 0, 0)),
    ]
    args = [_ret_log_gamma(), qkvg, qkvg, qkvg, qkvg, cos, sin, state]
    aliases = {}
    body = _ret_sample_body
    if out_buf is not None:
        in_specs.append(pl.BlockSpec(memory_space=pl.ANY))
        args.append(out_buf)
        aliases = {len(args) - 1: 1}
        body = lambda *refs: _ret_sample_body(*refs[:8], *refs[9:])
    return pl.pallas_call(
        body,
        out_shape=(jax.ShapeDtypeStruct((db, RET_HEADS * RET_DV), F32),
                   jax.ShapeDtypeStruct((n_layers, db, RET_HEADS, RET_DK, RET_DV), F32)),
        grid=(db // bt, RET_HEADS),
        in_specs=in_specs,
        out_specs=(
            pl.BlockSpec((bt, RET_DV), lambda bi, h: (bi, h)),
            pl.BlockSpec(st_block, lambda bi, h: (layer, bi, h, 0, 0)),
        ),
        input_output_aliases=aliases,
        compiler_params=_cparams(("parallel", "parallel"), 4 * bt * RET_DK * RET_DV * 4 + (4 << 20)),
        name="ret_sample",
    )(*args)


def _conv_act(g2, g1, g0, up, cw_ref, cb_ref):
    conv = g2 * cw_ref[0:1, :] + g1 * cw_ref[1:2, :] + g0 * cw_ref[2:3, :] + cb_ref[...]
    return _silu(conv) * up


def _ffn1_seq_body(x_ref, wg_ref, wu_ref, cw_ref, cb_ref, act_ref, cs_ref, carry_ref):
    i = pl.program_id(1)
    j = pl.program_id(2)
    x = x_ref[...]
    gate = jnp.dot(x, wg_ref[...].astype(BF16), preferred_element_type=F32)
    up = jnp.dot(x, wu_ref[...].astype(BF16), preferred_element_type=F32)
    tm = gate.shape[0]
    head = 2 * SUBLANES
    act = _conv_act(pltpu.roll(gate, 2, 0), pltpu.roll(gate, 1, 0), gate, up, cw_ref, cb_ref)
    act_ref[head:, :] = act[head:].astype(act_ref.dtype)
    @pl.when(i == 0)
    def _():
        carry_ref[j] = jnp.zeros(carry_ref.shape[1:], F32)

    win = jnp.concatenate([carry_ref[j], gate[:head]], axis=0)
    w1 = pltpu.roll(win, 1, 0)[SUBLANES:]
    w2 = pltpu.roll(win, 2, 0)[SUBLANES:]
    act_ref[:head, :] = _conv_act(w2, w1, gate[:head], up[:head], cw_ref, cb_ref).astype(act_ref.dtype)
    tail = gate[tm - SUBLANES:]
    carry_ref[j] = tail
    cs_ref[...] = tail


def _ffn1_seq(x, w_gate, w_up, conv_w, conv_b, layer, n_seq):
    m, k = x.shape
    f = w_gate.shape[-1]
    s = m // n_seq
    tm = _divisor(s, 1408, BF16_ROWS)
    tn = _divisor(f, 512, LANES)
    ni, nj = s // tm, f // tn
    est = 2 * (tm * k * 2 + 2 * k * tn * 4 + tm * tn * 2) + 2 * k * tn * 2 + 4 * tm * tn * 4
    act, tails = pl.pallas_call(
        _ffn1_seq_body,
        out_shape=(jax.ShapeDtypeStruct((m, f), BF16), jax.ShapeDtypeStruct((n_seq, ni, SUBLANES, f), F32)),
        grid=(n_seq, ni, nj),
        in_specs=[
            pl.BlockSpec((tm, k), lambda b, i, j: (b * ni + i, 0)),
            pl.BlockSpec((None, k, tn), lambda b, i, j: (layer, 0, j)),
            pl.BlockSpec((None, k, tn), lambda b, i, j: (layer, 0, j)),
            pl.BlockSpec((None, CONV_W, tn), lambda b, i, j: (layer, 0, j)),
            pl.BlockSpec((None, 1, tn), lambda b, i, j: (layer, 0, j)),
        ],
        out_specs=(
            pl.BlockSpec((tm, tn), lambda b, i, j: (b * ni + i, j)),
            pl.BlockSpec((None, None, SUBLANES, tn), lambda b, i, j: (b, i, 0, j)),
        ),
        scratch_shapes=[pltpu.VMEM((nj, SUBLANES, tn), F32)],
        compiler_params=_cparams(("arbitrary", "arbitrary", "arbitrary"), est),
        name="ffn1_seq",
    )(x, w_gate, w_up, conv_w, conv_b.reshape(conv_b.shape[0], 1, f))
    return act, tails[:, ni - 1]


def _ffn1_tok_body(x_ref, wg_ref, wu_ref, cw_ref, cb_ref, s0_ref, s1_ref, act_ref, gate_ref):
    x = x_ref[...]
    gate = jnp.dot(x, wg_ref[...].astype(BF16), preferred_element_type=F32)
    up = jnp.dot(x, wu_ref[...].astype(BF16), preferred_element_type=F32)
    act_ref[...] = _conv_act(s0_ref[...], s1_ref[...], gate, up, cw_ref, cb_ref).astype(act_ref.dtype)
    gate_ref[...] = gate


def _ffn1_tok(x, w_gate, w_up, conv_w, conv_b, layer, conv_state):
    m, k = x.shape
    f = w_gate.shape[-1]
    tn = _divisor(f, 1024, LANES)
    nj = f // tn
    cs = conv_state.reshape(m, (CONV_W - 1) * f)
    est = 2 * (m * k * 2 + 2 * k * tn * 4 + 6 * m * tn * 4) + 2 * k * tn * 2
    return pl.pallas_call(
        _ffn1_tok_body,
        out_shape=(jax.ShapeDtypeStruct((m, f), BF16), jax.ShapeDtypeStruct((m, f), F32)),
        grid=(nj,),
        in_specs=[
            pl.BlockSpec((m, k), lambda j: (0, 0)),
            pl.BlockSpec((None, k, tn), lambda j: (layer, 0, j)),
            pl.BlockSpec((None, k, tn), lambda j: (layer, 0, j)),
            pl.BlockSpec((None, CONV_W, tn), lambda j: (layer, 0, j)),
            pl.BlockSpec((None, 1, tn), lambda j: (layer, 0, j)),
            pl.BlockSpec((m, tn), lambda j: (0, j)),
            pl.BlockSpec((m, tn), lambda j: (0, nj + j)),
        ],
        out_specs=(pl.BlockSpec((m, tn), lambda j: (0, j)), pl.BlockSpec((m, tn), lambda j: (0, j))),
        compiler_params=_cparams(("parallel",), est),
        name="ffn1_tok",
    )(x, w_gate, w_up, conv_w, conv_b.reshape(conv_b.shape[0], 1, f), cs, cs)


def _kv_post_body(y_ref, g_ref, c_ref, s1_ref, s2_ref, cf_ref, cb_ref, r_ref):
    y = y_ref[...]
    c = _rms(y[:, :KV_LORA]) * g_ref[...]
    cf_ref[...] = c
    cb_ref[...] = c.astype(cb_ref.dtype)
    r_ref[...] = _slab_rope(y[:, KV_LORA:], c_ref[...], s1_ref[...], s2_ref[...])


def _kv_post(y, g, tables):
    m = y.shape[0]
    t_rows = tables[0].shape[0]
    tm = _divisor(m if t_rows == 1 else t_rows, 512, BF16_ROWS)
    if t_rows == 1:
        tspec = pl.BlockSpec((1, SLAB), lambda i: (0, 0))
    else:
        nt = t_rows // tm
        tspec = pl.BlockSpec((tm, SLAB), lambda i: (i % nt, 0))
    return pl.pallas_call(
        _kv_post_body,
        out_shape=(jax.ShapeDtypeStruct((m, KV_LORA), F32), jax.ShapeDtypeStruct((m, KV_LORA), BF16),
                   jax.ShapeDtypeStruct((m, SLAB), F32)),
        grid=(m // tm,),
        in_specs=[pl.BlockSpec((tm, KV_LORA + SLAB), lambda i: (i, 0)), pl.BlockSpec((1, KV_LORA), lambda i: (0, 0)),
                  tspec, tspec, tspec],
        out_specs=(pl.BlockSpec((tm, KV_LORA), lambda i: (i, 0)), pl.BlockSpec((tm, KV_LORA), lambda i: (i, 0)),
                   pl.BlockSpec((tm, SLAB), lambda i: (i, 0))),
        compiler_params=_cparams(("parallel",), 16 << 20),
        name="kv_post",
    )(y, g.reshape(1, KV_LORA), *tables)


ATTN_BLOCK = 3 * LANES


def _v_up_t_body(c_ref, w_ref, o_ref):
    o_ref[...] = lax.dot_general(w_ref[...].astype(BF16), c_ref[...], (((1,), (1,)), ((), ())),
                                 preferred_element_type=F32).astype(o_ref.dtype)


def _v_up_t(c, w_uv_t, n_seq, tk):
    m, kl = c.shape
    n = w_uv_t.shape[0]
    nk = m // n_seq // tk
    tn = _divisor(n, 1024, LANES)
    return pl.pallas_call(
        _v_up_t_body,
        out_shape=jax.ShapeDtypeStruct((n_seq, nk, n, tk), BF16),
        grid=(n_seq, nk, n // tn),
        in_specs=[pl.BlockSpec((tk, kl), lambda b, kb, j: (b * nk + kb, 0)), pl.BlockSpec((tn, kl), lambda b, kb, j: (j, 0))],
        out_specs=pl.BlockSpec((None, None, tn, tk), lambda b, kb, j: (b, kb, j, 0)),
        compiler_params=_cparams(("parallel", "parallel", "parallel"), 16 << 20),
        name="v_up_t",
    )(c, w_uv_t)


def _attn_prompt_body(q_ref, k_ref, vt_ref, c_ref, s1_ref, s2_ref, o_ref, m_ref, l_ref, acc_ref, *, pad, blk, hp):
    qi = pl.program_id(2)
    scale = (MLA_NOPE + MLA_ROPE) ** -0.5
    tabs = (c_ref[...], s1_ref[...], s2_ref[...])
    q = [(_slab_rope(q_ref[:, i * SLAB:(i + 1) * SLAB], *tabs) * scale).astype(BF16) for i in range(hp)]
    m_ref[...] = jnp.full_like(m_ref, NEG)
    l_ref[...] = jnp.zeros_like(l_ref)
    acc_ref[...] = jnp.zeros_like(acc_ref)
    nt = (((1,), (1,)), ((), ()))

    def step(ki, kind):
        start = ki * blk if isinstance(ki, int) else pl.multiple_of(ki * blk, blk)
        for i in range(hp):
            kb = k_ref[pl.ds(start, blk), i * SLAB:(i + 1) * SLAB]
            s = lax.dot_general(kb, q[i], nt, preferred_element_type=F32)
            if kind == "first":
                s = s[pad:]
            elif kind == "diag":
                kpos = ki * blk + lax.broadcasted_iota(jnp.int32, (blk, 1), 0)
                qpos = qi * blk + lax.broadcasted_iota(jnp.int32, (1, blk), 1)
                s = jnp.where((kpos <= qpos) & ((kpos >= pad) | (qpos < pad)), s, NEG)
            m_old = m_ref[i]
            m_new = jnp.maximum(m_old, jnp.max(s, axis=0, keepdims=True))
            alpha = jnp.exp(m_old - m_new)
            p = jnp.exp(s - m_new)
            l_ref[i] = alpha * l_ref[i] + jnp.sum(p, axis=0, keepdims=True)
            pb = p.astype(BF16)
            if kind == "first":
                pb = jnp.concatenate([jnp.zeros((pad, blk), BF16), pb], axis=0)
            vt = vt_ref[ki, i * MLA_V:(i + 1) * MLA_V, :]
            acc_ref[i] = alpha * acc_ref[i] + jnp.dot(vt, pb, preferred_element_type=F32)
            m_ref[i] = m_new

    @pl.when(qi > 0)
    def _():
        step(0, "first")

    def mid(ki, carry):
        step(ki, "full")
        return carry

    lax.fori_loop(1, qi, mid, 0)
    step(qi, "diag")

    for i in range(hp):
        o_ref[:, i * MLA_V:(i + 1) * MLA_V] = (acc_ref[i] / l_ref[i]).T.astype(o_ref.dtype)


def _attn_prompt(q, k, vt, tables, pad):
    b, s, _ = q.shape
    nk, blk = vt.shape[1], vt.shape[3]
    hp = _divisor(MLA_HEADS, 2, 1)
    body = functools.partial(_attn_prompt_body, pad=pad, blk=blk, hp=hp)
    tspec = pl.BlockSpec((blk, SLAB), lambda bi, h, qi: (qi, 0))
    est = (2 * (blk * hp * SLAB * 4 + s * hp * SLAB * 2 + s * hp * MLA_V * 2 + 3 * blk * SLAB * 4 + blk * hp * MLA_V * 2)
           + 6 * hp * blk * blk * 4)
    return pl.pallas_call(
        body,
        out_shape=jax.ShapeDtypeStruct((b, s, MLA_HEADS * MLA_V), BF16),
        grid=(b, MLA_HEADS // hp, s // blk),
        in_specs=[
            pl.BlockSpec((None, blk, hp * SLAB), lambda bi, h, qi: (bi, qi, h)),
            pl.BlockSpec((None, s, hp * SLAB), lambda bi, h, qi: (bi, 0, h)),
            pl.BlockSpec((None, nk, hp * MLA_V, blk), lambda bi, h, qi: (bi, 0, h, 0)),
            tspec, tspec, tspec,
        ],
        out_specs=pl.BlockSpec((None, blk, hp * MLA_V), lambda bi, h, qi: (bi, qi, h)),
        scratch_shapes=[pltpu.VMEM((hp, 1, blk), F32), pltpu.VMEM((hp, 1, blk), F32), pltpu.VMEM((hp, MLA_V, blk), F32)],
        compiler_params=_cparams(("parallel", "parallel", "arbitrary"), est),
        name="attn_prompt",
    )(q, k, vt, *tables)


def _q_absorb_body(q_ref, w_ref, c_ref, s1_ref, s2_ref, ql_ref, qr_ref):
    scale = (MLA_NOPE + MLA_ROPE) ** -0.5
    slab = _slab_rope(q_ref[...], c_ref[...], s1_ref[...], s2_ref[...]) * scale
    qr_ref[...] = slab
    ql_ref[...] = lax.dot_general(slab[:, :MLA_NOPE].astype(BF16), w_ref[...].astype(BF16), (((1,), (1,)), ((), ())),
                                  preferred_element_type=F32)


def _q_absorb(q, w_uk2, tables):
    db = q.shape[0]
    tspec = pl.BlockSpec((1, SLAB), lambda h: (0, 0))
    return pl.pallas_call(
        _q_absorb_body,
        out_shape=(jax.ShapeDtypeStruct((db, MLA_HEADS * KV_LORA), F32), jax.ShapeDtypeStruct((db, MLA_HEADS * SLAB), F32)),
        grid=(MLA_HEADS,),
        in_specs=[pl.BlockSpec((db, SLAB), lambda h: (0, h)), pl.BlockSpec((KV_LORA, MLA_NOPE), lambda h: (0, h)),
                  tspec, tspec, tspec],
        out_specs=(pl.BlockSpec((db, KV_LORA), lambda h: (0, h)), pl.BlockSpec((db, SLAB), lambda h: (0, h))),
        compiler_params=_cparams(("parallel",), 8 << 20),
        name="q_absorb",
    )(q, w_uk2, *tables)


def _attn_sample_body(pt_ref, ql_ref, qs_ref, cn_ref, rn_ref, *refs, pages, n_groups, nb):
    n_page_refs = nb * pages
    lat_refs = refs[:n_page_refs]
    rope_refs = refs[n_page_refs:2 * n_page_refs]
    o_ref, m_ref, l_ref, acc_ref, latb_ref, ropeb_ref = refs[2 * n_page_refs:]
    g = pl.program_id(1)
    page = lat_refs[0].shape[0]

    @pl.when(g == 0)
    def _():
        m_ref[...] = jnp.full_like(m_ref, NEG)
        l_ref[...] = jnp.zeros_like(l_ref)
        acc_ref[...] = jnp.zeros_like(acc_ref)

    nt = (((1,), (1,)), ((), ()))
    for b in range(nb):
        for i in range(pages):
            latb_ref[b, i * page:(i + 1) * page, :] = lat_refs[b * pages + i][...].astype(BF16)
            ropeb_ref[b, :, i * page:(i + 1) * page] = rope_refs[b * pages + i][...].astype(BF16)
        ql = ql_ref[b]
        qr = qs_ref[b][:, MLA_NOPE:MLA_NOPE + MLA_ROPE]
        s = lax.dot_general(ql.astype(BF16), latb_ref[b], nt, preferred_element_type=F32)
        s = s + jnp.dot(qr.astype(BF16), ropeb_ref[b], preferred_element_type=F32)
        m_old = m_ref[b]
        m_new = jnp.maximum(m_old, jnp.max(s, axis=-1, keepdims=True))
        alpha = jnp.exp(m_old - m_new)
        p = jnp.exp(s - m_new)
        l_ref[b] = alpha * l_ref[b] + jnp.sum(p, axis=-1, keepdims=True)
        acc_ref[b] = alpha * acc_ref[b] + jnp.dot(p.astype(BF16), latb_ref[b], preferred_element_type=F32)
        m_ref[b] = m_new

    @pl.when(g == n_groups - 1)
    def _():
        for b in range(nb):
            ql = ql_ref[b]
            qr = qs_ref[b][:, MLA_NOPE:MLA_NOPE + MLA_ROPE]
            cn = cn_ref[b]
            rn = rn_ref[b][:, MLA_NOPE:MLA_NOPE + MLA_ROPE]
            s_new = jnp.sum(ql * cn, axis=-1, keepdims=True) + jnp.sum(qr * rn, axis=-1, keepdims=True)
            m_old = m_ref[b]
            m_new = jnp.maximum(m_old, s_new)
            alpha = jnp.exp(m_old - m_new)
            p_new = jnp.exp(s_new - m_new)
            l = alpha * l_ref[b] + p_new
            o_ref[b] = (alpha * acc_ref[b] + p_new * cn) / l


def _attn_sample(page_table, cache_lat, cache_rope_t, q_lat, q_slab, c_new, r_new):
    db, n_pages = page_table.shape
    page = cache_lat.shape[1]
    pages = _divisor(n_pages, 16, 1)
    nb = _divisor(db, 2, 1)
    n_groups = n_pages // pages
    body = functools.partial(_attn_sample_body, pages=pages, n_groups=n_groups, nb=nb)

    def page_spec(shape, b, i):
        return pl.BlockSpec((None,) + shape, lambda bi, g, pt: (pt[bi * nb + b, g * pages + i], 0, 0))

    def seq_spec(rows, width):
        return pl.BlockSpec((nb, rows, width), lambda bi, g, pt: (bi, 0, 0))

    in_specs = [seq_spec(MLA_HEADS, KV_LORA), seq_spec(MLA_HEADS, SLAB), seq_spec(1, KV_LORA), seq_spec(1, SLAB)]
    in_specs += [page_spec((page, KV_LORA), b, i) for b in range(nb) for i in range(pages)]
    in_specs += [page_spec((MLA_ROPE, page), b, i) for b in range(nb) for i in range(pages)]
    keys = pages * page
    est = nb * (2 * keys * (KV_LORA + MLA_ROPE) * 4 + 2 * keys * (KV_LORA + MLA_ROPE) * 2) + (4 << 20)
    return pl.pallas_call(
        body,
        out_shape=jax.ShapeDtypeStruct((db, MLA_HEADS, KV_LORA), F32),
        grid_spec=pltpu.PrefetchScalarGridSpec(
            num_scalar_prefetch=1,
            grid=(db // nb, n_groups),
            in_specs=in_specs,
            out_specs=seq_spec(MLA_HEADS, KV_LORA),
            scratch_shapes=[
                pltpu.VMEM((nb, MLA_HEADS, 1), F32), pltpu.VMEM((nb, MLA_HEADS, 1), F32),
                pltpu.VMEM((nb, MLA_HEADS, KV_LORA), F32),
                pltpu.VMEM((nb, keys, KV_LORA), BF16), pltpu.VMEM((nb, MLA_ROPE, keys), BF16),
            ],
        ),
        compiler_params=_cparams(("parallel", "arbitrary"), est),
        name="attn_sample",
    )(page_table, q_lat, q_slab, c_new, r_new, *([cache_lat] * (nb * pages)), *([cache_rope_t] * (nb * pages)))


def _head_out_body(o_ref, w_ref, y_ref):
    y_ref[...] = jnp.dot(o_ref[...].astype(BF16), w_ref[...].astype(BF16), preferred_element_type=F32).astype(y_ref.dtype)


def _head_out(o_lat, w_uv2):
    db = o_lat.shape[0]
    return pl.pallas_call(
        _head_out_body,
        out_shape=jax.ShapeDtypeStruct((db, MLA_HEADS * MLA_V), BF16),
        grid=(MLA_HEADS,),
        in_specs=[pl.BlockSpec((db, KV_LORA), lambda h: (0, h)), pl.BlockSpec((KV_LORA, MLA_V), lambda h: (0, h))],
        out_specs=pl.BlockSpec((db, MLA_V), lambda h: (0, h)),
        compiler_params=_cparams(("parallel",), 4 << 20),
        name="head_out",
    )(o_lat, w_uv2)


def _prep_weights(ret_w_qkvg, ret_w_o, ffn_w_gate, ffn_w_up, ffn_w_down, w_dkv, w_kr, w_uk, w_uv, mla_w_dq, mla_w_uq,
                  mla_w_o):
    tail = SLAB - MLA_NOPE - MLA_ROPE
    w_kv = jnp.concatenate([w_dkv, jnp.zeros((D_MODEL, MLA_NOPE), F32), w_kr, jnp.zeros((D_MODEL, tail), F32)], axis=1)
    w_uk_ext = jnp.pad(w_uk, ((0, 0), (0, 0), (0, SLAB - MLA_NOPE))).reshape(KV_LORA, MLA_HEADS * SLAB)
    n_b = mla_w_uq.shape[0]
    w_uq_ext = jnp.pad(mla_w_uq.reshape(n_b, Q_LORA, MLA_HEADS, MLA_NOPE + MLA_ROPE),
                       ((0, 0), (0, 0), (0, 0), (0, tail))).reshape(n_b, Q_LORA, MLA_HEADS * SLAB)
    w_uv2 = w_uv.reshape(KV_LORA, MLA_HEADS * MLA_V)
    return dict(
        qkvg=ret_w_qkvg, ret_o=ret_w_o, gate=ffn_w_gate, up=ffn_w_up, down=ffn_w_down, kv=w_kv, uk_ext=w_uk_ext,
        uk2=w_uk.reshape(KV_LORA, MLA_HEADS * MLA_NOPE), uv2=w_uv2, uv_t=w_uv2.T, dq=mla_w_dq, uq_ext=w_uq_ext,
        mla_o=mla_w_o,
    )


def _trunk(h, w, norm_g, ffn_conv_w, ffn_conv_b, kv_in_g, kv_norm_g, mla_q_norm_g, ret_mixer, ffn1, kv_tables, attend):
    conv_states = []
    c_f32 = r_slab = kv_ctx = None
    xn = _rms_cast(h, norm_g[0, 0])
    for layer in range(DEPTH):
        g = norm_g[layer]
        if layer < N_A_LAYERS:
            qkvg = _matmul(xn, w["qkvg"], ret_mixer.qkvg_dtype, layer=layer, name="qkvg")
            gated = ret_mixer(layer, qkvg)
            mix = _matmul(gated, w["ret_o"], F32, layer=layer, name="ret_o")
        else:
            j = layer - N_A_LAYERS
            cq = _matmul(xn, w["dq"], F32, layer=j, name="dq")
            cqn = _rms_cast(cq, mla_q_norm_g[j])
            q = _matmul(cqn, w["uq_ext"], F32, layer=j, name="uq")
            o = attend(q, kv_ctx)
            mix = _matmul(o, w["mla_o"], F32, layer=j, name="mla_o")
        h, xn = _resid_norm(h, mix, g[1], g[2])
        act, cs = ffn1(layer, xn, w["gate"], w["up"], ffn_conv_w, ffn_conv_b)
        conv_states.append(cs)
        ff = _matmul(act, w["down"], F32, layer=layer, name="down")
        g_next = norm_g[layer + 1, 0] if layer + 1 < DEPTH else g[3]
        h, xn = _resid_norm(h, ff, g[3], g_next)
        if layer == N_A_LAYERS - 1:
            hn = _rms_cast(h, kv_in_g)
            y = _matmul(hn, w["kv"], F32, name="kv_down")
            c_f32, c_bf16, r_slab = _kv_post(y, kv_norm_g, kv_tables)
            kv_ctx = (c_f32, c_bf16, r_slab)
    return h, conv_states, c_f32, r_slab


class _PromptRetention:
    qkvg_dtype = BF16

    def __init__(self, n_seq, pad, cos, sin):
        self.n_seq, self.pad, self.cos, self.sin = n_seq, pad, cos, sin
        self.states = []

    def __call__(self, layer, qkvg):
        m, n = qkvg.shape
        gated, state = _ret_prompt(qkvg.reshape(self.n_seq, m // self.n_seq, n), self.cos, self.sin, self.pad)
        self.states.append(state)
        return gated.reshape(m, -1)


class _SampleRetention:
    qkvg_dtype = F32

    def __init__(self, state, cos, sin):
        self.state, self.cos, self.sin = state, cos, sin
        self.out = None

    def __call__(self, layer, qkvg):
        gated, self.out = _ret_sample(qkvg, self.state, self.cos, self.sin, self.out, layer)
        return gated


def kernel(x_prompt, x_sample, state_retention, state_conv, cache_kv_latent, cache_k_rope, page_table, meta_tokens, norm_g,
           ret_w_qkvg, ret_w_o, ffn_w_gate, ffn_w_up, ffn_w_down, ffn_conv_w, ffn_conv_b, kv_in_g, w_dkv, kv_norm_g, w_kr,
           w_uk, w_uv, mla_w_dq, mla_q_norm_g, mla_w_uq, mla_w_o):
    w = _prep_weights(ret_w_qkvg, ret_w_o, ffn_w_gate, ffn_w_up, ffn_w_down, w_dkv, w_kr, w_uk, w_uv, mla_w_dq, mla_w_uq,
                      mla_w_o)
    shared = (norm_g, ffn_conv_w, ffn_conv_b, kv_in_g, kv_norm_g, mla_q_norm_g)

    b, seq, d = x_prompt.shape
    pad = RET_CHUNK - N_META
    s_pad = pad + N_META + seq
    h0 = jnp.concatenate([jnp.zeros((b, pad, d), F32), jnp.broadcast_to(meta_tokens[None], (b, N_META, d)), x_prompt], axis=1)
    pos_p = jnp.arange(s_pad) - pad
    cos_p, sin_p = _rope_tables(pos_p, RET_DK)
    tables_p = _slab_rope_tables(pos_p)
    ret_p = _PromptRetention(b, pad, cos_p, sin_p)

    def ffn1_p(layer, xn, wg, wu, cw, cb):
        act, tail = _ffn1_seq(xn, wg, wu, cw, cb, layer, b)
        return act, tail[:, SUBLANES - (CONV_W - 1):, :]

    def attend_p(q, kv_ctx):
        if "k" not in attend_p.cache:
            _, c_bf16, r_slab = kv_ctx
            attend_p.cache["k"] = _matmul(c_bf16, w["uk_ext"], BF16, slab=r_slab, name="k_up").reshape(b, s_pad, -1)
            attend_p.cache["vt"] = _v_up_t(c_bf16, w["uv_t"], b, _divisor(s_pad, ATTN_BLOCK, LANES))
        o = _attn_prompt(q.reshape(b, s_pad, -1), attend_p.cache["k"], attend_p.cache["vt"], tables_p, pad)
        return o.reshape(b * s_pad, -1)

    attend_p.cache = {}
    h_p, conv_p, lat_p, rslab_p = _trunk(h0.reshape(b * s_pad, d), w, *shared, ret_p, ffn1_p, tables_p, attend_p)
    y_prompt = h_p.reshape(b, s_pad, d)[:, pad + N_META:]
    lat_prompt = lat_p.reshape(b, s_pad, KV_LORA)[:, pad:]
    rope_prompt = rslab_p.reshape(b, s_pad, SLAB)[:, pad:, MLA_NOPE:MLA_NOPE + MLA_ROPE]

    db = x_sample.shape[0]
    pos_s = jnp.full((1,), PAST_LEN)
    cos_s, sin_s = _rope_tables(pos_s, RET_DK)
    tables_s = _slab_rope_tables(pos_s)
    ret_s = _SampleRetention(state_retention, cos_s, sin_s)
    cache_rope_t = jnp.swapaxes(cache_k_rope, 1, 2)

    def ffn1_s(layer, xn, wg, wu, cw, cb):
        act, gate = _ffn1_tok(xn, wg, wu, cw, cb, layer, state_conv[layer])
        return act, jnp.stack([state_conv[layer][:, 1], gate], axis=1)

    def attend_s(q, kv_ctx):
        c_f32, _, r_slab = kv_ctx
        q_lat, q_slab = _q_absorb(q, w["uk2"], tables_s)
        o_lat = _attn_sample(page_table, cache_kv_latent, cache_rope_t, q_lat.reshape(db, MLA_HEADS, KV_LORA),
                             q_slab.reshape(db, MLA_HEADS, SLAB), c_f32.reshape(db, 1, KV_LORA), r_slab.reshape(db, 1, SLAB))
        return _head_out(o_lat.reshape(db, MLA_HEADS * KV_LORA), w["uv2"])

    h_s, conv_s, lat_s, rslab_s = _trunk(x_sample.reshape(db, d), w, *shared, ret_s, ffn1_s, tables_s, attend_s)

    return (
        y_prompt,
        h_s.reshape(db, 1, d),
        jnp.stack(ret_p.states),
        ret_s.out,
        jnp.stack(conv_p),
        jnp.stack(conv_s),
        lat_prompt,
        lat_s.reshape(db, 1, KV_LORA),
        rope_prompt,
        rslab_s[:, MLA_NOPE:MLA_NOPE + MLA_ROPE].reshape(db, 1, MLA_ROPE),
    )
```

```python
import functools

import jax
import jax.numpy as jnp
from jax import lax
from jax.experimental import pallas as pl
from jax.experimental.pallas import tpu as pltpu

D_MODEL = 2048
SEQ = 4096
DEPTH = 4
PAST_LEN = 8192
PAGE_SIZE = 128
N_META = 16
N_A_LAYERS = DEPTH // 2
RET_HEADS = 8
RET_DK = D_MODEL // RET_HEADS
RET_DV = 2 * D_MODEL // RET_HEADS
RET_CHUNK = 128
MLA_HEADS = D_MODEL // 128
MLA_NOPE = 128
MLA_ROPE = 64
MLA_V = 128
Q_LORA = 512
KV_LORA = 512
D_FF = 11 * D_MODEL // 4
CONV_W = 3
ROPE_BASE = 10000.0
EPS = 1e-6

V7X_VMEM_BYTES = 64 * 1024 * 1024
VMEM_REQUEST_CAP = V7X_VMEM_BYTES - 8 * 1024 * 1024
LANES = 128
SUBLANES = 8
BF16_ROWS = 16
SLAB = 2 * LANES
NEG = -1e30

F32 = jnp.float32
BF16 = jnp.bfloat16


def _cparams(semantics, vmem_bytes):
    limit = int(min(max(vmem_bytes + (6 << 20), 16 << 20), VMEM_REQUEST_CAP))
    return pltpu.CompilerParams(dimension_semantics=semantics, vmem_limit_bytes=limit)


def _divisor(n, cap, mult):
    d = (min(n, cap) // mult) * mult
    while d >= mult:
        if n % d == 0:
            return d
        d -= mult
    return n


def _silu(x):
    return x * (1.0 / (1.0 + jnp.exp(-x)))


def _rms(x):
    return x * lax.rsqrt(jnp.mean(x * x, axis=-1, keepdims=True) + EPS)


def _rms_cast_body(x_ref, g_ref, o_ref):
    o_ref[...] = (_rms(x_ref[...]) * g_ref[...]).astype(o_ref.dtype)


def _rms_cast(x, g, out_dtype=BF16):
    m, d = x.shape
    tm = _divisor(m, 512, BF16_ROWS)
    return pl.pallas_call(
        _rms_cast_body,
        out_shape=jax.ShapeDtypeStruct((m, d), out_dtype),
        grid=(m // tm,),
        in_specs=[pl.BlockSpec((tm, d), lambda i: (i, 0)), pl.BlockSpec((1, d), lambda i: (0, 0))],
        out_specs=pl.BlockSpec((tm, d), lambda i: (i, 0)),
        compiler_params=_cparams(("parallel",), 2 * tm * d * 6 + tm * d * 8),
        name="rms_cast",
    )(x, g.reshape(1, d))


def _resid_norm_body(h_ref, y_ref, gp_ref, gn_ref, ho_ref, xo_ref):
    h = h_ref[...] + _rms(y_ref[...]) * gp_ref[...]
    ho_ref[...] = h
    xo_ref[...] = (_rms(h) * gn_ref[...]).astype(xo_ref.dtype)


def _resid_norm(h, y, g_post, g_next):
    m, d = h.shape
    tm = _divisor(m, 256, BF16_ROWS)
    row = pl.BlockSpec((tm, d), lambda i: (i, 0))
    vec = pl.BlockSpec((1, d), lambda i: (0, 0))
    return pl.pallas_call(
        _resid_norm_body,
        out_shape=(jax.ShapeDtypeStruct((m, d), F32), jax.ShapeDtypeStruct((m, d), BF16)),
        grid=(m // tm,),
        in_specs=[row, row, vec, vec],
        out_specs=(row, row),
        compiler_params=_cparams(("parallel",), 2 * tm * d * 14 + tm * d * 12),
        name="resid_norm",
    )(h, y, g_post.reshape(1, d), g_next.reshape(1, d))


def _mm_body(a_ref, w_ref, o_ref):
    a = a_ref[...].astype(BF16)
    o_ref[...] = jnp.dot(a, w_ref[...].astype(BF16), preferred_element_type=F32).astype(o_ref.dtype)


def _mm_slab_body(a_ref, w_ref, s_ref, o_ref, *, rep):
    a = a_ref[...].astype(BF16)
    acc = jnp.dot(a, w_ref[...].astype(BF16), preferred_element_type=F32)
    s = s_ref[...]
    o_ref[...] = (acc + jnp.concatenate([s] * rep, axis=1)).astype(o_ref.dtype)


def _mm_tiles(m, k, n, ab, wb, ob, mult, budget):
    def est(tm, tn):
        return 2 * (tm * k * ab + k * tn * wb + tm * tn * ob) + tm * tn * 4 + tm * k * 2 + k * tn * 2

    tn_min = min(n, max(mult, SLAB))
    tm = _divisor(m, 1408, BF16_ROWS)
    while True:
        tn = _divisor(n, 512 if m > 256 else 2048, mult)
        while est(tm, tn) > budget and tn > tn_min:
            tn = _divisor(n, tn - mult, mult)
        if est(tm, tn) <= budget or tm <= BF16_ROWS:
            return tm, tn, est(tm, tn)
        tm = _divisor(m, tm - BF16_ROWS, BF16_ROWS)


def _mm_rope_body(a_ref, w_ref, c_ref, s1_ref, s2_ref, o_ref, *, rep, scale):
    a = a_ref[...].astype(BF16)
    acc = jnp.dot(a, w_ref[...].astype(BF16), preferred_element_type=F32)
    tabs = (c_ref[...], s1_ref[...], s2_ref[...])
    for r in range(rep):
        o_ref[:, r * SLAB:(r + 1) * SLAB] = (_slab_rope(acc[:, r * SLAB:(r + 1) * SLAB], *tabs) * scale).astype(o_ref.dtype)


def _matmul(a, w, out_dtype, layer=None, slab=None, rope=None, name="matmul"):
    m, k = a.shape
    n = w.shape[-1]
    mult = SLAB if (slab is not None or rope is not None) else LANES
    tm, tn, est = _mm_tiles(m, k, n, a.dtype.itemsize, w.dtype.itemsize, jnp.dtype(out_dtype).itemsize, mult, 46 << 20)
    if w.ndim == 3:
        w_spec = pl.BlockSpec((None, k, tn), lambda i, j: (layer, 0, j))
    else:
        w_spec = pl.BlockSpec((k, tn), lambda i, j: (0, j))
    in_specs = [pl.BlockSpec((tm, k), lambda i, j: (i, 0)), w_spec]
    args = [a, w]
    body = _mm_body
    if slab is not None:
        in_specs.append(pl.BlockSpec((tm, SLAB), lambda i, j: (i, 0)))
        args.append(slab)
        body = functools.partial(_mm_slab_body, rep=tn // SLAB)
    if rope is not None:
        tables, scale = rope
        period = tables[0].shape[0]
        if period == 1:
            tspec = pl.BlockSpec((1, SLAB), lambda i, j: (0, 0))
        else:
            assert period % tm == 0, (period, tm)
            nt = period // tm
            tspec = pl.BlockSpec((tm, SLAB), lambda i, j: (i % nt, 0))
        in_specs += [tspec] * 3
        args += list(tables)
        body = functools.partial(_mm_rope_body, rep=tn // SLAB, scale=scale)
        est += 6 * tm * SLAB * 4
    return pl.pallas_call(
        body,
        out_shape=jax.ShapeDtypeStruct((m, n), out_dtype),
        grid=(m // tm, n // tn),
        in_specs=in_specs,
        out_specs=pl.BlockSpec((tm, tn), lambda i, j: (i, j)),
        compiler_params=_cparams(("parallel", "parallel"), est),
        name=name,
    )(*args)


def _mm_resid_body(a_ref, w_ref, h_ref, gp_ref, gn_ref, ho_ref, xo_ref, acc_ref, *, nk):
    kk = pl.program_id(1)

    @pl.when(kk == 0)
    def _():
        acc_ref[...] = jnp.zeros_like(acc_ref)

    acc_ref[...] += jnp.dot(a_ref[...].astype(BF16), w_ref[...].astype(BF16), preferred_element_type=F32)

    @pl.when(kk == nk - 1)
    def _():
        h = h_ref[...] + _rms(acc_ref[...]) * gp_ref[...]
        ho_ref[...] = h
        xo_ref[...] = (_rms(h) * gn_ref[...]).astype(xo_ref.dtype)


def _matmul_resid(a, w, layer, h, g_post, g_next, name="matmul_resid"):
    m, k = a.shape
    n = w.shape[-1]
    tk = _divisor(k, 512, LANES)
    nk = k // tk
    ab = a.dtype.itemsize

    def est(tm):
        return tm * n * 4 + 2 * (tm * tk * ab + tk * n * 4 + 2 * tm * n * 4 + tm * n * 2) + tk * n * 2 + tm * tk * 2

    tm = _divisor(m, 1408, BF16_ROWS)
    while est(tm) > (50 << 20) and tm > BF16_ROWS:
        tm = _divisor(m, tm - BF16_ROWS, BF16_ROWS)
    row = pl.BlockSpec((tm, n), lambda i, kk: (i, 0))
    vec = pl.BlockSpec((1, n), lambda i, kk: (0, 0))
    return pl.pallas_call(
        functools.partial(_mm_resid_body, nk=nk),
        out_shape=(jax.ShapeDtypeStruct((m, n), F32), jax.ShapeDtypeStruct((m, n), BF16)),
        grid=(m // tm, nk),
        in_specs=[
            pl.BlockSpec((tm, tk), lambda i, kk: (i, kk)),
            pl.BlockSpec((None, tk, n), lambda i, kk: (layer, kk, 0)),
            row, vec, vec,
        ],
        out_specs=(row, row),
        scratch_shapes=[pltpu.VMEM((tm, n), F32)],
        compiler_params=_cparams(("parallel", "arbitrary"), est(tm)),
        name=name,
    )(a, w, h, g_post.reshape(1, n), g_next.reshape(1, n))


def _rope_tables(pos, d):
    inv = ROPE_BASE ** (-jnp.arange(0, d, 2, dtype=F32) / d)
    ang = pos.astype(F32)[:, None] * inv[None, :]
    return jnp.cos(ang), jnp.sin(ang)


def _slab_rope_tables(pos):
    cos, sin = _rope_tables(pos, MLA_ROPE)
    n, half = cos.shape
    one = jnp.ones((n, MLA_NOPE), F32)
    z_nope = jnp.zeros((n, MLA_NOPE), F32)
    z_half = jnp.zeros((n, half), F32)
    z_tail = jnp.zeros((n, SLAB - MLA_NOPE - MLA_ROPE), F32)
    c = jnp.concatenate([one, cos, cos, z_tail], axis=1)
    s1 = jnp.concatenate([z_nope, -sin, z_half, z_tail], axis=1)
    s2 = jnp.concatenate([z_nope, z_half, sin, z_tail], axis=1)
    return c, s1, s2


def _slab_rope(x, c, s1, s2):
    half = MLA_ROPE // 2
    return x * c + pltpu.roll(x, SLAB - half, 1) * s1 + pltpu.roll(x, half, 1) * s2


def _ret_rope(x, cos, sin):
    half = RET_DK // 2
    x1, x2 = x[:, :half], x[:, half:]
    return jnp.concatenate([x1 * cos - x2 * sin, x1 * sin + x2 * cos], axis=1)


def _ret_log_gamma():
    return jnp.log1p(-jnp.power(2.0, -5.0 - jnp.arange(RET_HEADS, dtype=F32)))


def _ret_prompt_body(lg_ref, q_ref, k_ref, v_ref, g_ref, cos_ref, sin_ref, o_ref, so_ref, st_ref, *, pad, n_chunks, hp):
    hg = pl.program_id(1)
    c = pl.program_id(2)
    chunk = q_ref.shape[0]

    @pl.when(c == 0)
    def _():
        st_ref[...] = jnp.zeros_like(st_ref)

    lead = jnp.where(c == 0, float(pad), 0.0)
    cos, sin = cos_ref[...], sin_ref[...]
    n_col = lax.broadcasted_iota(jnp.int32, (chunk, 1), 0).astype(F32)
    n_row = lax.broadcasted_iota(jnp.int32, (1, chunk), 1).astype(F32)
    diff = n_col - n_row
    for i in range(hp):
        lg = lg_ref[hg * hp + i]
        q = _ret_rope(q_ref[:, i * RET_DK:(i + 1) * RET_DK].astype(F32), cos, sin)
        k = _ret_rope(k_ref[:, i * RET_DK:(i + 1) * RET_DK].astype(F32), cos, sin) * (RET_DK ** -0.5)
        v = v_ref[:, i * RET_DV:(i + 1) * RET_DV]
        intra = jnp.where(diff >= 0, jnp.exp(lg * jnp.maximum(diff, 0.0)), 0.0)
        q_dec = jnp.exp(lg * (n_col + 1.0 - lead))
        k_dec = jnp.exp(lg * (chunk - 1.0 - n_col))
        s_dec = jnp.exp(jnp.full((1, 1), lg * (chunk - lead), F32))
        st = st_ref[i]
        scores = lax.dot_general(q.astype(BF16), k.astype(BF16), (((1,), (1,)), ((), ())),
                                 preferred_element_type=F32) * intra
        out = jnp.dot(scores.astype(BF16), v, preferred_element_type=F32)
        out = out + jnp.dot((q * q_dec).astype(BF16), st.astype(BF16), preferred_element_type=F32)
        kt = (k * k_dec).T.astype(BF16)
        st_ref[i] = st * s_dec + jnp.dot(kt, v, preferred_element_type=F32)
        g = g_ref[:, i * RET_DV:(i + 1) * RET_DV].astype(F32)
        o_ref[:, i * RET_DV:(i + 1) * RET_DV] = (_silu(g) * _rms(out)).astype(o_ref.dtype)

    @pl.when(c == n_chunks - 1)
    def _():
        so_ref[...] = st_ref[...]


def _ret_prompt(qkvg, cos, sin, pad):
    b, s, _ = qkvg.shape
    chunk = RET_CHUNK
    nc = s // chunk
    hp = _divisor(RET_HEADS, 8, 1)
    ng = RET_HEADS // hp
    body = functools.partial(_ret_prompt_body, pad=pad, n_chunks=nc, hp=hp)
    return pl.pallas_call(
        body,
        out_shape=(jax.ShapeDtypeStruct((b, s, RET_HEADS * RET_DV), BF16),
                   jax.ShapeDtypeStruct((b, RET_HEADS, RET_DK, RET_DV), F32)),
        grid=(b, ng, nc),
        in_specs=[
            pl.BlockSpec(memory_space=pltpu.SMEM),
            pl.BlockSpec((None, chunk, hp * RET_DK), lambda bi, h, c: (bi, c, h)),
            pl.BlockSpec((None, chunk, hp * RET_DK), lambda bi, h, c: (bi, c, ng + h)),
            pl.BlockSpec((None, chunk, hp * RET_DV), lambda bi, h, c: (bi, c, ng + h)),
            pl.BlockSpec((None, chunk, hp * RET_DV), lambda bi, h, c: (bi, c, 2 * ng + h)),
            pl.BlockSpec((chunk, RET_DK // 2), lambda bi, h, c: (c, 0)),
            pl.BlockSpec((chunk, RET_DK // 2), lambda bi, h, c: (c, 0)),
        ],
        out_specs=(
            pl.BlockSpec((None, chunk, hp * RET_DV), lambda bi, h, c: (bi, c, h)),
            pl.BlockSpec((None, hp, RET_DK, RET_DV), lambda bi, h, c: (bi, h, 0, 0)),
        ),
        scratch_shapes=[pltpu.VMEM((hp, RET_DK, RET_DV), F32)],
        compiler_params=_cparams(("parallel", "parallel", "arbitrary"), 16 << 20),
        name="ret_prompt",
    )(_ret_log_gamma(), qkvg, qkvg, qkvg, qkvg, cos, sin)


def _ret_sample_body(lg_ref, q_ref, k_ref, v_ref, g_ref, cos_ref, sin_ref, st_ref, o_ref, so_ref):
    h = pl.program_id(1)
    bt = q_ref.shape[0]
    gamma = jnp.exp(jnp.full((1, 1), lg_ref[h], F32))
    cos, sin = cos_ref[...], sin_ref[...]
    q = _ret_rope(q_ref[...], cos, sin)
    k = _ret_rope(k_ref[...], cos, sin) * (RET_DK ** -0.5)
    v = v_ref[...]
    qk = jnp.sum(q * k, axis=-1, keepdims=True)
    qg = (q * gamma).astype(BF16)
    eye = lax.broadcasted_iota(jnp.int32, (RET_DK, RET_DK), 0) == lax.broadcasted_iota(jnp.int32, (RET_DK, RET_DK), 1)
    rows = []
    for i in range(bt):
        st = st_ref[i]
        cross = jnp.dot(qg, st.astype(BF16), preferred_element_type=F32)
        rows.append(cross[i:i + 1])
        k_col = jnp.sum(jnp.where(eye, k[i:i + 1], 0.0), axis=1, keepdims=True)
        so_ref[i] = st * gamma + k_col * v[i:i + 1]
    out = qk * v + jnp.concatenate(rows, axis=0)
    o_ref[...] = _silu(g_ref[...]) * _rms(out)


def _ret_sample(qkvg, state, cos, sin, out_buf, layer):
    db = qkvg.shape[0]
    bt = SUBLANES
    kq = RET_HEADS
    n_layers = state.shape[0]
    st_block = (None, bt, None, RET_DK, RET_DV)
    in_specs = [
        pl.BlockSpec(memory_space=pltpu.SMEM),
        pl.BlockSpec((bt, RET_DK), lambda bi, h: (bi, h)),
        pl.BlockSpec((bt, RET_DK), lambda bi, h: (bi, kq + h)),
        pl.BlockSpec((bt, RET_DV), lambda bi, h: (bi, kq + h)),
        pl.BlockSpec((bt, RET_DV), lambda bi, h: (bi, 2 * kq + h)),
        pl.BlockSpec((1, RET_DK // 2), lambda bi, h: (0, 0)),
        pl.BlockSpec((1, RET_DK // 2), lambda bi, h: (0, 0)),
        pl.BlockSpec(st_block, lambda bi, h: (layer, bi, h, 0, 0)),
    ]
    args = [_ret_log_gamma(), qkvg, qkvg, qkvg, qkvg, cos, sin, state]
    aliases = {}
    body = _ret_sample_body
    if out_buf is not None:
        in_specs.append(pl.BlockSpec(memory_space=pl.ANY))
        args.append(out_buf)
        aliases = {len(args) - 1: 1}
        body = lambda *refs: _ret_sample_body(*refs[:8], *refs[9:])
    return pl.pallas_call(
        body,
        out_shape=(jax.ShapeDtypeStruct((db, RET_HEADS * RET_DV), F32),
                   jax.ShapeDtypeStruct((n_layers, db, RET_HEADS, RET_DK, RET_DV), F32)),
        grid=(db // bt, RET_HEADS),
        in_specs=in_specs,
        out_specs=(
            pl.BlockSpec((bt, RET_DV), lambda bi, h: (bi, h)),
            pl.BlockSpec(st_block, lambda bi, h: (layer, bi, h, 0, 0)),
        ),
        input_output_aliases=aliases,
        compiler_params=_cparams(("parallel", "parallel"), 4 * bt * RET_DK * RET_DV * 4 + (4 << 20)),
        name="ret_sample",
    )(*args)


def _conv_act(g2, g1, g0, up, cw_ref, cb_ref):
    conv = g2 * cw_ref[0:1, :] + g1 * cw_ref[1:2, :] + g0 * cw_ref[2:3, :] + cb_ref[...]
    return _silu(conv) * up


def _ffn1_seq_body(x_ref, wg_ref, wu_ref, cw_ref, cb_ref, act_ref, cs_ref, carry_ref):
    i = pl.program_id(1)
    j = pl.program_id(2)
    x = x_ref[...]
    gate = jnp.dot(x, wg_ref[...].astype(BF16), preferred_element_type=F32)
    up = jnp.dot(x, wu_ref[...].astype(BF16), preferred_element_type=F32)
    tm = gate.shape[0]
    head = 2 * SUBLANES
    act = _conv_act(pltpu.roll(gate, 2, 0), pltpu.roll(gate, 1, 0), gate, up, cw_ref, cb_ref)
    act_ref[head:, :] = act[head:].astype(act_ref.dtype)
    @pl.when(i == 0)
    def _():
        carry_ref[j] = jnp.zeros(carry_ref.shape[1:], F32)

    win = jnp.concatenate([carry_ref[j], gate[:head]], axis=0)
    w1 = pltpu.roll(win, 1, 0)[SUBLANES:]
    w2 = pltpu.roll(win, 2, 0)[SUBLANES:]
    act_ref[:head, :] = _conv_act(w2, w1, gate[:head], up[:head], cw_ref, cb_ref).astype(act_ref.dtype)
    tail = gate[tm - SUBLANES:]
    carry_ref[j] = tail
    cs_ref[...] = tail


def _ffn1_seq(x, w_gate, w_up, conv_w, conv_b, layer, n_seq):
    m, k = x.shape
    f = w_gate.shape[-1]
    s = m // n_seq
    tm = _divisor(s, 1408, BF16_ROWS)
    tn = _divisor(f, 512, LANES)
    ni, nj = s // tm, f // tn
    est = 2 * (tm * k * 2 + 2 * k * tn * 4 + tm * tn * 2) + 2 * k * tn * 2 + 4 * tm * tn * 4
    act, tails = pl.pallas_call(
        _ffn1_seq_body,
        out_shape=(jax.ShapeDtypeStruct((m, f), BF16), jax.ShapeDtypeStruct((n_seq, ni, SUBLANES, f), F32)),
        grid=(n_seq, ni, nj),
        in_specs=[
            pl.BlockSpec((tm, k), lambda b, i, j: (b * ni + i, 0)),
            pl.BlockSpec((None, k, tn), lambda b, i, j: (layer, 0, j)),
            pl.BlockSpec((None, k, tn), lambda b, i, j: (layer, 0, j)),
            pl.BlockSpec((None, CONV_W, tn), lambda b, i, j: (layer, 0, j)),
            pl.BlockSpec((None, 1, tn), lambda b, i, j: (layer, 0, j)),
        ],
        out_specs=(
            pl.BlockSpec((tm, tn), lambda b, i, j: (b * ni + i, j)),
            pl.BlockSpec((None, None, SUBLANES, tn), lambda b, i, j: (b, i, 0, j)),
        ),
        scratch_shapes=[pltpu.VMEM((nj, SUBLANES, tn), F32)],
        compiler_params=_cparams(("arbitrary", "arbitrary", "arbitrary"), est),
        name="ffn1_seq",
    )(x, w_gate, w_up, conv_w, conv_b.reshape(conv_b.shape[0], 1, f))
    return act, tails[:, ni - 1]


def _ffn1_tok_body(x_ref, wg_ref, wu_ref, cw_ref, cb_ref, s0_ref, s1_ref, act_ref, gate_ref):
    x = x_ref[...]
    gate = jnp.dot(x, wg_ref[...].astype(BF16), preferred_element_type=F32)
    up = jnp.dot(x, wu_ref[...].astype(BF16), preferred_element_type=F32)
    act_ref[...] = _conv_act(s0_ref[...], s1_ref[...], gate, up, cw_ref, cb_ref).astype(act_ref.dtype)
    gate_ref[...] = gate


def _ffn1_tok(x, w_gate, w_up, conv_w, conv_b, layer, conv_state):
    m, k = x.shape
    f = w_gate.shape[-1]
    tn = _divisor(f, 1024, LANES)
    nj = f // tn
    cs = conv_state.reshape(m, (CONV_W - 1) * f)
    est = 2 * (m * k * 2 + 2 * k * tn * 4 + 6 * m * tn * 4) + 2 * k * tn * 2
    return pl.pallas_call(
        _ffn1_tok_body,
        out_shape=(jax.ShapeDtypeStruct((m, f), BF16), jax.ShapeDtypeStruct((m, f), F32)),
        grid=(nj,),
        in_specs=[
            pl.BlockSpec((m, k), lambda j: (0, 0)),
            pl.BlockSpec((None, k, tn), lambda j: (layer, 0, j)),
            pl.BlockSpec((None, k, tn), lambda j: (layer, 0, j)),
            pl.BlockSpec((None, CONV_W, tn), lambda j: (layer, 0, j)),
            pl.BlockSpec((None, 1, tn), lambda j: (layer, 0, j)),
            pl.BlockSpec((m, tn), lambda j: (0, j)),
            pl.BlockSpec((m, tn), lambda j: (0, nj + j)),
        ],
        out_specs=(pl.BlockSpec((m, tn), lambda j: (0, j)), pl.BlockSpec((m, tn), lambda j: (0, j))),
        compiler_params=_cparams(("parallel",), est),
        name="ffn1_tok",
    )(x, w_gate, w_up, conv_w, conv_b.reshape(conv_b.shape[0], 1, f), cs, cs)


def _kv_post_body(y_ref, g_ref, c_ref, s1_ref, s2_ref, cf_ref, cb_ref, r_ref):
    y = y_ref[...]
    c = _rms(y[:, :KV_LORA]) * g_ref[...]
    cf_ref[...] = c
    cb_ref[...] = c.astype(cb_ref.dtype)
    r_ref[...] = _slab_rope(y[:, KV_LORA:], c_ref[...], s1_ref[...], s2_ref[...])


def _kv_post(y, g, tables):
    m = y.shape[0]
    t_rows = tables[0].shape[0]
    tm = _divisor(m if t_rows == 1 else t_rows, 512, BF16_ROWS)
    if t_rows == 1:
        tspec = pl.BlockSpec((1, SLAB), lambda i: (0, 0))
    else:
        nt = t_rows // tm
        tspec = pl.BlockSpec((tm, SLAB), lambda i: (i % nt, 0))
    return pl.pallas_call(
        _kv_post_body,
        out_shape=(jax.ShapeDtypeStruct((m, KV_LORA), F32), jax.ShapeDtypeStruct((m, KV_LORA), BF16),
                   jax.ShapeDtypeStruct((m, SLAB), F32)),
        grid=(m // tm,),
        in_specs=[pl.BlockSpec((tm, KV_LORA + SLAB), lambda i: (i, 0)), pl.BlockSpec((1, KV_LORA), lambda i: (0, 0)),
                  tspec, tspec, tspec],
        out_specs=(pl.BlockSpec((tm, KV_LORA), lambda i: (i, 0)), pl.BlockSpec((tm, KV_LORA), lambda i: (i, 0)),
                   pl.BlockSpec((tm, SLAB), lambda i: (i, 0))),
        compiler_params=_cparams(("parallel",), 16 << 20),
        name="kv_post",
    )(y, g.reshape(1, KV_LORA), *tables)


ATTN_BLOCK = 3 * LANES


def _v_up_t_body(c_ref, w_ref, o_ref):
    o_ref[...] = lax.dot_general(w_ref[...].astype(BF16), c_ref[...], (((1,), (1,)), ((), ())),
                                 preferred_element_type=F32).astype(o_ref.dtype)


def _v_up_t(c, w_uv_t, n_seq, tk):
    m, kl = c.shape
    n = w_uv_t.shape[0]
    nk = m // n_seq // tk
    tn = _divisor(n, 1024, LANES)
    return pl.pallas_call(
        _v_up_t_body,
        out_shape=jax.ShapeDtypeStruct((n_seq, nk, n, tk), BF16),
        grid=(n_seq, nk, n // tn),
        in_specs=[pl.BlockSpec((tk, kl), lambda b, kb, j: (b * nk + kb, 0)), pl.BlockSpec((tn, kl), lambda b, kb, j: (j, 0))],
        out_specs=pl.BlockSpec((None, None, tn, tk), lambda b, kb, j: (b, kb, j, 0)),
        compiler_params=_cparams(("parallel", "parallel", "parallel"), 16 << 20),
        name="v_up_t",
    )(c, w_uv_t)


def _attn_prompt_body(q_ref, k_ref, vt_ref, bias_ref, o_ref, m_ref, l_ref, acc_ref, s_ref, *, pad, blk, hp):
    qi = pl.program_id(2)
    m_ref[...] = jnp.full_like(m_ref, NEG)
    l_ref[...] = jnp.zeros_like(l_ref)
    acc_ref[...] = jnp.zeros_like(acc_ref)
    nt = (((1,), (1,)), ((), ()))

    def scores(ki, slot):
        start = ki * blk if isinstance(ki, int) else pl.multiple_of(ki * blk, blk)
        for i in range(hp):
            kb = k_ref[pl.ds(start, blk), i * SLAB:(i + 1) * SLAB]
            s_ref[slot, i] = lax.dot_general(kb, q_ref[:, i * SLAB:(i + 1) * SLAB], nt, preferred_element_type=F32)

    def consume(ki, slot, kind):
        for i in range(hp):
            s = s_ref[slot, i]
            if kind == "first":
                s = s[pad:]
            elif kind == "diag":
                s = s + bias_ref[jnp.where(qi == 0, 0, 1)]
            m_old = m_ref[i]
            m_new = jnp.maximum(m_old, jnp.max(s, axis=0, keepdims=True))
            alpha = jnp.exp(m_old - m_new)
            p = jnp.exp(s - m_new)
            l_ref[i] = alpha * l_ref[i] + jnp.sum(p, axis=0, keepdims=True)
            pb = p.astype(BF16)
            if kind == "first":
                pb = jnp.concatenate([jnp.zeros((pad, blk), BF16), pb], axis=0)
            vt = vt_ref[ki, i * MLA_V:(i + 1) * MLA_V, :]
            acc_ref[i] = alpha * acc_ref[i] + jnp.dot(vt, pb, preferred_element_type=F32)
            m_ref[i] = m_new

    scores(0, 0)

    @pl.when(qi > 0)
    def _():
        scores(1, 1)
        consume(0, 0, "first")

    def mid(ki, carry):
        slot = lax.rem(ki, 2)
        consume(ki, slot, "full")
        scores(ki + 1, 1 - slot)
        return carry

    lax.fori_loop(1, qi, mid, 0)
    consume(qi, lax.rem(qi, 2), "diag")

    for i in range(hp):
        o_ref[:, i * MLA_V:(i + 1) * MLA_V] = (acc_ref[i] / l_ref[i]).T.astype(o_ref.dtype)


def _attn_bias(blk, pad):
    kpos = jnp.arange(blk)[:, None]
    qpos = jnp.arange(blk)[None, :]
    causal = kpos <= qpos
    first = causal & ((kpos >= pad) | (qpos < pad))
    return jnp.stack([jnp.where(first, 0.0, NEG), jnp.where(causal, 0.0, NEG)]).astype(F32)


def _attn_prompt(q, k, vt, pad):
    b, s, _ = q.shape
    nk, blk = vt.shape[1], vt.shape[3]
    hp = _divisor(MLA_HEADS, 2, 1)
    body = functools.partial(_attn_prompt_body, pad=pad, blk=blk, hp=hp)
    est = (2 * (blk * hp * SLAB * 2 + s * hp * SLAB * 2 + s * hp * MLA_V * 2 + 2 * blk * blk * 4 + blk * hp * MLA_V * 2)
           + 8 * hp * blk * blk * 4)
    return pl.pallas_call(
        body,
        out_shape=jax.ShapeDtypeStruct((b, s, MLA_HEADS * MLA_V), BF16),
        grid=(b, MLA_HEADS // hp, s // blk),
        in_specs=[
            pl.BlockSpec((None, blk, hp * SLAB), lambda bi, h, qi: (bi, qi, h)),
            pl.BlockSpec((None, s, hp * SLAB), lambda bi, h, qi: (bi, 0, h)),
            pl.BlockSpec((None, nk, hp * MLA_V, blk), lambda bi, h, qi: (bi, 0, h, 0)),
            pl.BlockSpec((2, blk, blk), lambda bi, h, qi: (0, 0, 0)),
        ],
        out_specs=pl.BlockSpec((None, blk, hp * MLA_V), lambda bi, h, qi: (bi, qi, h)),
        scratch_shapes=[pltpu.VMEM((hp, 1, blk), F32), pltpu.VMEM((hp, 1, blk), F32), pltpu.VMEM((hp, MLA_V, blk), F32),
                        pltpu.VMEM((2, hp, blk, blk), F32)],
        compiler_params=_cparams(("parallel", "parallel", "arbitrary"), est),
        name="attn_prompt",
    )(q, k, vt, _attn_bias(blk, pad))


def _q_absorb_body(q_ref, w_ref, ql_ref):
    ql_ref[...] = lax.dot_general(q_ref[...][:, :MLA_NOPE].astype(BF16), w_ref[...].astype(BF16), (((1,), (1,)), ((), ())),
                                  preferred_element_type=F32)


def _q_absorb(q, w_uk2):
    db = q.shape[0]
    return pl.pallas_call(
        _q_absorb_body,
        out_shape=jax.ShapeDtypeStruct((db, MLA_HEADS * KV_LORA), F32),
        grid=(MLA_HEADS,),
        in_specs=[pl.BlockSpec((db, SLAB), lambda h: (0, h)), pl.BlockSpec((KV_LORA, MLA_NOPE), lambda h: (0, h))],
        out_specs=pl.BlockSpec((db, KV_LORA), lambda h: (0, h)),
        compiler_params=_cparams(("parallel",), 8 << 20),
        name="q_absorb",
    )(q, w_uk2)


def _attn_sample_body(pt_ref, ql_ref, qs_ref, cn_ref, rn_ref, *refs, pages, n_groups, nb):
    n_page_refs = nb * pages
    lat_refs = refs[:n_page_refs]
    rope_refs = refs[n_page_refs:2 * n_page_refs]
    o_ref, m_ref, l_ref, acc_ref, latb_ref, ropeb_ref = refs[2 * n_page_refs:]
    g = pl.program_id(1)
    page = lat_refs[0].shape[0]

    @pl.when(g == 0)
    def _():
        m_ref[...] = jnp.full_like(m_ref, NEG)
        l_ref[...] = jnp.zeros_like(l_ref)
        acc_ref[...] = jnp.zeros_like(acc_ref)

    nt = (((1,), (1,)), ((), ()))
    for b in range(nb):
        for i in range(pages):
            latb_ref[b, i * page:(i + 1) * page, :] = lat_refs[b * pages + i][...].astype(BF16)
            ropeb_ref[b, :, i * page:(i + 1) * page] = rope_refs[b * pages + i][...].astype(BF16)
        ql = ql_ref[b]
        qr = qs_ref[b][:, MLA_NOPE:MLA_NOPE + MLA_ROPE]
        s = lax.dot_general(ql.astype(BF16), latb_ref[b], nt, preferred_element_type=F32)
        s = s + jnp.dot(qr.astype(BF16), ropeb_ref[b], preferred_element_type=F32)
        m_old = m_ref[b]
        m_new = jnp.maximum(m_old, jnp.max(s, axis=-1, keepdims=True))
        alpha = jnp.exp(m_old - m_new)
        p = jnp.exp(s - m_new)
        l_ref[b] = alpha * l_ref[b] + jnp.sum(p, axis=-1, keepdims=True)
        acc_ref[b] = alpha * acc_ref[b] + jnp.dot(p.astype(BF16), latb_ref[b], preferred_element_type=F32)
        m_ref[b] = m_new

    @pl.when(g == n_groups - 1)
    def _():
        for b in range(nb):
            ql = ql_ref[b]
            qr = qs_ref[b][:, MLA_NOPE:MLA_NOPE + MLA_ROPE]
            cn = cn_ref[b]
            rn = rn_ref[b][:, MLA_NOPE:MLA_NOPE + MLA_ROPE]
            s_new = jnp.sum(ql * cn, axis=-1, keepdims=True) + jnp.sum(qr * rn, axis=-1, keepdims=True)
            m_old = m_ref[b]
            m_new = jnp.maximum(m_old, s_new)
            alpha = jnp.exp(m_old - m_new)
            p_new = jnp.exp(s_new - m_new)
            l = alpha * l_ref[b] + p_new
            o_ref[b] = (alpha * acc_ref[b] + p_new * cn) / l


def _attn_sample(page_table, cache_lat, cache_rope_t, q_lat, q_slab, c_new, r_new):
    db, n_pages = page_table.shape
    page = cache_lat.shape[1]
    pages = _divisor(n_pages, 16, 1)
    nb = _divisor(db, 4, 1)
    n_groups = n_pages // pages
    body = functools.partial(_attn_sample_body, pages=pages, n_groups=n_groups, nb=nb)

    def page_spec(shape, b, i):
        return pl.BlockSpec((None,) + shape, lambda bi, g, pt: (pt[bi * nb + b, g * pages + i], 0, 0))

    def seq_spec(rows, width):
        return pl.BlockSpec((nb, rows, width), lambda bi, g, pt: (bi, 0, 0))

    in_specs = [seq_spec(MLA_HEADS, KV_LORA), seq_spec(MLA_HEADS, SLAB), seq_spec(1, KV_LORA), seq_spec(1, SLAB)]
    in_specs += [page_spec((page, KV_LORA), b, i) for b in range(nb) for i in range(pages)]
    in_specs += [page_spec((MLA_ROPE, page), b, i) for b in range(nb) for i in range(pages)]
    keys = pages * page
    est = nb * (2 * keys * (KV_LORA + MLA_ROPE) * 4 + 2 * keys * (KV_LORA + MLA_ROPE) * 2) + (4 << 20)
    return pl.pallas_call(
        body,
        out_shape=jax.ShapeDtypeStruct((db, MLA_HEADS, KV_LORA), F32),
        grid_spec=pltpu.PrefetchScalarGridSpec(
            num_scalar_prefetch=1,
            grid=(db // nb, n_groups),
            in_specs=in_specs,
            out_specs=seq_spec(MLA_HEADS, KV_LORA),
            scratch_shapes=[
                pltpu.VMEM((nb, MLA_HEADS, 1), F32), pltpu.VMEM((nb, MLA_HEADS, 1), F32),
                pltpu.VMEM((nb, MLA_HEADS, KV_LORA), F32),
                pltpu.VMEM((nb, keys, KV_LORA), BF16), pltpu.VMEM((nb, MLA_ROPE, keys), BF16),
            ],
        ),
        compiler_params=_cparams(("parallel", "arbitrary"), est),
        name="attn_sample",
    )(page_table, q_lat, q_slab, c_new, r_new, *([cache_lat] * (nb * pages)), *([cache_rope_t] * (nb * pages)))


def _head_out_body(o_ref, w_ref, y_ref):
    y_ref[...] = jnp.dot(o_ref[...].astype(BF16), w_ref[...].astype(BF16), preferred_element_type=F32).astype(y_ref.dtype)


def _head_out(o_lat, w_uv2):
    db = o_lat.shape[0]
    return pl.pallas_call(
        _head_out_body,
        out_shape=jax.ShapeDtypeStruct((db, MLA_HEADS * MLA_V), BF16),
        grid=(MLA_HEADS,),
        in_specs=[pl.BlockSpec((db, KV_LORA), lambda h: (0, h)), pl.BlockSpec((KV_LORA, MLA_V), lambda h: (0, h))],
        out_specs=pl.BlockSpec((db, MLA_V), lambda h: (0, h)),
        compiler_params=_cparams(("parallel",), 4 << 20),
        name="head_out",
    )(o_lat, w_uv2)


def _prep_weights(ret_w_qkvg, ret_w_o, ffn_w_gate, ffn_w_up, ffn_w_down, w_dkv, w_kr, w_uk, w_uv, mla_w_dq, mla_w_uq,
                  mla_w_o):
    tail = SLAB - MLA_NOPE - MLA_ROPE
    w_kv = jnp.concatenate([w_dkv, jnp.zeros((D_MODEL, MLA_NOPE), F32), w_kr, jnp.zeros((D_MODEL, tail), F32)], axis=1)
    w_uk_ext = jnp.pad(w_uk, ((0, 0), (0, 0), (0, SLAB - MLA_NOPE))).reshape(KV_LORA, MLA_HEADS * SLAB)
    n_b = mla_w_uq.shape[0]
    w_uq_ext = jnp.pad(mla_w_uq.reshape(n_b, Q_LORA, MLA_HEADS, MLA_NOPE + MLA_ROPE),
                       ((0, 0), (0, 0), (0, 0), (0, tail))).reshape(n_b, Q_LORA, MLA_HEADS * SLAB)
    w_uv2 = w_uv.reshape(KV_LORA, MLA_HEADS * MLA_V)
    return dict(
        qkvg=ret_w_qkvg, ret_o=ret_w_o, gate=ffn_w_gate, up=ffn_w_up, down=ffn_w_down, kv=w_kv, uk_ext=w_uk_ext,
        uk2=w_uk.reshape(KV_LORA, MLA_HEADS * MLA_NOPE), uv2=w_uv2, uv_t=w_uv2.T, dq=mla_w_dq, uq_ext=w_uq_ext,
        mla_o=mla_w_o,
    )


def _trunk(h, w, norm_g, ffn_conv_w, ffn_conv_b, kv_in_g, kv_norm_g, mla_q_norm_g, ret_mixer, ffn1, kv_tables, attend):
    conv_states = []
    c_f32 = r_slab = kv_ctx = None
    q_scale = (MLA_NOPE + MLA_ROPE) ** -0.5
    xn = _rms_cast(h, norm_g[0, 0])
    for layer in range(DEPTH):
        g = norm_g[layer]
        if layer < N_A_LAYERS:
            qkvg = _matmul(xn, w["qkvg"], ret_mixer.qkvg_dtype, layer=layer, name="qkvg")
            gated = ret_mixer(layer, qkvg)
            h, xn = _matmul_resid(gated, w["ret_o"], layer, h, g[1], g[2], name="ret_o")
        else:
            j = layer - N_A_LAYERS
            cq = _matmul(xn, w["dq"], F32, layer=j, name="dq")
            cqn = _rms_cast(cq, mla_q_norm_g[j])
            q = _matmul(cqn, w["uq_ext"], attend.q_dtype, layer=j, rope=(kv_tables, q_scale), name="uq")
            o = attend(q, kv_ctx)
            h, xn = _matmul_resid(o, w["mla_o"], j, h, g[1], g[2], name="mla_o")
        act, cs = ffn1(layer, xn, w["gate"], w["up"], ffn_conv_w, ffn_conv_b)
        conv_states.append(cs)
        g_next = norm_g[layer + 1, 0] if layer + 1 < DEPTH else g[3]
        h, xn = _matmul_resid(act, w["down"], layer, h, g[3], g_next, name="down")
        if layer == N_A_LAYERS - 1:
            hn = _rms_cast(h, kv_in_g)
            y = _matmul(hn, w["kv"], F32, name="kv_down")
            c_f32, c_bf16, r_slab = _kv_post(y, kv_norm_g, kv_tables)
            kv_ctx = (c_f32, c_bf16, r_slab)
    return h, conv_states, c_f32, r_slab


class _PromptRetention:
    qkvg_dtype = BF16

    def __init__(self, n_seq, pad, cos, sin):
        self.n_seq, self.pad, self.cos, self.sin = n_seq, pad, cos, sin
        self.states = []

    def __call__(self, layer, qkvg):
        m, n = qkvg.shape
        gated, state = _ret_prompt(qkvg.reshape(self.n_seq, m // self.n_seq, n), self.cos, self.sin, self.pad)
        self.states.append(state)
        return gated.reshape(m, -1)


class _SampleRetention:
    qkvg_dtype = F32

    def __init__(self, state, cos, sin):
        self.state, self.cos, self.sin = state, cos, sin
        self.out = None

    def __call__(self, layer, qkvg):
        gated, self.out = _ret_sample(qkvg, self.state, self.cos, self.sin, self.out, layer)
        return gated


def kernel(x_prompt, x_sample, state_retention, state_conv, cache_kv_latent, cache_k_rope, page_table, meta_tokens, norm_g,
           ret_w_qkvg, ret_w_o, ffn_w_gate, ffn_w_up, ffn_w_down, ffn_conv_w, ffn_conv_b, kv_in_g, w_dkv, kv_norm_g, w_kr,
           w_uk, w_uv, mla_w_dq, mla_q_norm_g, mla_w_uq, mla_w_o):
    w = _prep_weights(ret_w_qkvg, ret_w_o, ffn_w_gate, ffn_w_up, ffn_w_down, w_dkv, w_kr, w_uk, w_uv, mla_w_dq, mla_w_uq,
                      mla_w_o)
    shared = (norm_g, ffn_conv_w, ffn_conv_b, kv_in_g, kv_norm_g, mla_q_norm_g)

    b, seq, d = x_prompt.shape
    pad = RET_CHUNK - N_META
    s_pad = pad + N_META + seq
    h0 = jnp.concatenate([jnp.zeros((b, pad, d), F32), jnp.broadcast_to(meta_tokens[None], (b, N_META, d)), x_prompt], axis=1)
    pos_p = jnp.arange(s_pad) - pad
    cos_p, sin_p = _rope_tables(pos_p, RET_DK)
    tables_p = _slab_rope_tables(pos_p)
    ret_p = _PromptRetention(b, pad, cos_p, sin_p)

    def ffn1_p(layer, xn, wg, wu, cw, cb):
        act, tail = _ffn1_seq(xn, wg, wu, cw, cb, layer, b)
        return act, tail[:, SUBLANES - (CONV_W - 1):, :]

    def attend_p(q, kv_ctx):
        if "k" not in attend_p.cache:
            _, c_bf16, r_slab = kv_ctx
            attend_p.cache["k"] = _matmul(c_bf16, w["uk_ext"], BF16, slab=r_slab, name="k_up").reshape(b, s_pad, -1)
            attend_p.cache["vt"] = _v_up_t(c_bf16, w["uv_t"], b, _divisor(s_pad, ATTN_BLOCK, LANES))
        o = _attn_prompt(q.reshape(b, s_pad, -1), attend_p.cache["k"], attend_p.cache["vt"], pad)
        return o.reshape(b * s_pad, -1)

    attend_p.cache = {}
    attend_p.q_dtype = BF16
    h_p, conv_p, lat_p, rslab_p = _trunk(h0.reshape(b * s_pad, d), w, *shared, ret_p, ffn1_p, tables_p, attend_p)
    y_prompt = h_p.reshape(b, s_pad, d)[:, pad + N_META:]
    lat_prompt = lat_p.reshape(b, s_pad, KV_LORA)[:, pad:]
    rope_prompt = rslab_p.reshape(b, s_pad, SLAB)[:, pad:, MLA_NOPE:MLA_NOPE + MLA_ROPE]

    db = x_sample.shape[0]
    pos_s = jnp.full((1,), PAST_LEN)
    cos_s, sin_s = _rope_tables(pos_s, RET_DK)
    tables_s = _slab_rope_tables(pos_s)
    ret_s = _SampleRetention(state_retention, cos_s, sin_s)
    cache_rope_t = jnp.swapaxes(cache_k_rope, 1, 2)

    def ffn1_s(layer, xn, wg, wu, cw, cb):
        act, gate = _ffn1_tok(xn, wg, wu, cw, cb, layer, state_conv[layer])
        return act, jnp.stack([state_conv[layer][:, 1], gate], axis=1)

    def attend_s(q, kv_ctx):
        c_f32, _, r_slab = kv_ctx
        q_lat = _q_absorb(q, w["uk2"])
        o_lat = _attn_sample(page_table, cache_kv_latent, cache_rope_t, q_lat.reshape(db, MLA_HEADS, KV_LORA),
                             q.reshape(db, MLA_HEADS, SLAB), c_f32.reshape(db, 1, KV_LORA), r_slab.reshape(db, 1, SLAB))
        return _head_out(o_lat.reshape(db, MLA_HEADS * KV_LORA), w["uv2"])

    attend_s.q_dtype = F32

    h_s, conv_s, lat_s, rslab_s = _trunk(x_sample.reshape(db, d), w, *shared, ret_s, ffn1_s, tables_s, attend_s)

    return (
        y_prompt,
        h_s.reshape(db, 1, d),
        jnp.stack(ret_p.states),
        ret_s.out,
        jnp.stack(conv_p),
        jnp.stack(conv_s),
        lat_prompt,
        lat_s.reshape(db, 1, KV_LORA),
        rope_prompt,
        rslab_s[:, MLA_NOPE:MLA_NOPE + MLA_ROPE].reshape(db, 1, MLA_ROPE),
    )
```

```python
import functools

import jax
import jax.numpy as jnp
from jax import lax
from jax.experimental import pallas as pl
from jax.experimental.pallas import tpu as pltpu

D_MODEL = 2048
SEQ = 4096
DEPTH = 4
PAST_LEN = 8192
PAGE_SIZE = 128
N_META = 16
N_A_LAYERS = DEPTH // 2
RET_HEADS = 8
RET_DK = D_MODEL // RET_HEADS
RET_DV = 2 * D_MODEL // RET_HEADS
RET_CHUNK = 128
MLA_HEADS = D_MODEL // 128
MLA_NOPE = 128
MLA_ROPE = 64
MLA_V = 128
Q_LORA = 512
KV_LORA = 512
D_FF = 11 * D_MODEL // 4
CONV_W = 3
ROPE_BASE = 10000.0
EPS = 1e-6

V7X_VMEM_BYTES = 64 * 1024 * 1024
VMEM_REQUEST_CAP = V7X_VMEM_BYTES - 8 * 1024 * 1024
LANES = 128
SUBLANES = 8
BF16_ROWS = 16
SLAB = 2 * LANES
NEG = -1e30

F32 = jnp.float32
BF16 = jnp.bfloat16


def _cparams(semantics, vmem_bytes):
    limit = int(min(max(vmem_bytes + (6 << 20), 16 << 20), VMEM_REQUEST_CAP))
    return pltpu.CompilerParams(dimension_semantics=semantics, vmem_limit_bytes=limit)


def _divisor(n, cap, mult):
    d = (min(n, cap) // mult) * mult
    while d >= mult:
        if n % d == 0:
            return d
        d -= mult
    return n


def _silu(x):
    return x * (1.0 / (1.0 + jnp.exp(-x)))


def _rms(x):
    return x * lax.rsqrt(jnp.mean(x * x, axis=-1, keepdims=True) + EPS)


def _rms_cast_body(x_ref, g_ref, o_ref):
    o_ref[...] = (_rms(x_ref[...]) * g_ref[...]).astype(o_ref.dtype)


def _rms_cast(x, g, out_dtype=BF16):
    m, d = x.shape
    tm = _divisor(m, 512, BF16_ROWS)
    return pl.pallas_call(
        _rms_cast_body,
        out_shape=jax.ShapeDtypeStruct((m, d), out_dtype),
        grid=(m // tm,),
        in_specs=[pl.BlockSpec((tm, d), lambda i: (i, 0)), pl.BlockSpec((1, d), lambda i: (0, 0))],
        out_specs=pl.BlockSpec((tm, d), lambda i: (i, 0)),
        compiler_params=_cparams(("parallel",), 2 * tm * d * 6 + tm * d * 8),
        name="rms_cast",
    )(x, g.reshape(1, d))


def _resid_norm_body(h_ref, y_ref, gp_ref, gn_ref, ho_ref, xo_ref):
    h = h_ref[...] + _rms(y_ref[...]) * gp_ref[...]
    ho_ref[...] = h
    xo_ref[...] = (_rms(h) * gn_ref[...]).astype(xo_ref.dtype)


def _resid_norm(h, y, g_post, g_next):
    m, d = h.shape
    tm = _divisor(m, 256, BF16_ROWS)
    row = pl.BlockSpec((tm, d), lambda i: (i, 0))
    vec = pl.BlockSpec((1, d), lambda i: (0, 0))
    return pl.pallas_call(
        _resid_norm_body,
        out_shape=(jax.ShapeDtypeStruct((m, d), F32), jax.ShapeDtypeStruct((m, d), BF16)),
        grid=(m // tm,),
        in_specs=[row, row, vec, vec],
        out_specs=(row, row),
        compiler_params=_cparams(("parallel",), 2 * tm * d * 14 + tm * d * 12),
        name="resid_norm",
    )(h, y, g_post.reshape(1, d), g_next.reshape(1, d))


def _mm_body(a_ref, w_ref, o_ref):
    a = a_ref[...].astype(BF16)
    o_ref[...] = jnp.dot(a, w_ref[...].astype(BF16), preferred_element_type=F32).astype(o_ref.dtype)


def _mm_slab_body(a_ref, w_ref, s_ref, o_ref, *, rep):
    a = a_ref[...].astype(BF16)
    acc = jnp.dot(a, w_ref[...].astype(BF16), preferred_element_type=F32)
    s = s_ref[...]
    o_ref[...] = (acc + jnp.concatenate([s] * rep, axis=1)).astype(o_ref.dtype)


def _mm_tiles(m, k, n, ab, wb, ob, mult, budget):
    def est(tm, tn):
        return 2 * (tm * k * ab + k * tn * wb + tm * tn * ob) + tm * tn * 4 + tm * k * 2 + k * tn * 2

    tn_min = min(n, max(mult, SLAB))
    tm = _divisor(m, 1408, BF16_ROWS)
    while True:
        tn = _divisor(n, 1024 if m > 256 else 2048, mult)
        while est(tm, tn) > budget and tn > tn_min:
            tn = _divisor(n, tn - mult, mult)
        if est(tm, tn) <= budget or tm <= BF16_ROWS:
            return tm, tn, est(tm, tn)
        tm = _divisor(m, tm - BF16_ROWS, BF16_ROWS)


def _mm_rope_body(a_ref, w_ref, c_ref, s_ref, o_ref, *, rep, scale):
    a = a_ref[...].astype(BF16)
    acc = jnp.dot(a, w_ref[...].astype(BF16), preferred_element_type=F32)
    c, s = c_ref[...] * scale, s_ref[...] * scale
    for r in range(rep):
        x = acc[:, 2 * r * SLAB:(2 * r + 1) * SLAB]
        xs = acc[:, (2 * r + 1) * SLAB:(2 * r + 2) * SLAB]
        o_ref[:, r * SLAB:(r + 1) * SLAB] = (x * c + xs * s).astype(o_ref.dtype)


def _matmul(a, w, out_dtype, layer=None, slab=None, rope=None, name="matmul"):
    m, k = a.shape
    n = w.shape[-1]
    mult = 2 * SLAB if rope is not None else SLAB if slab is not None else LANES
    tm, tn, est = _mm_tiles(m, k, n, a.dtype.itemsize, w.dtype.itemsize, jnp.dtype(out_dtype).itemsize, mult, 46 << 20)
    n_out, tn_out = (n // 2, tn // 2) if rope is not None else (n, tn)
    if w.ndim == 3:
        w_spec = pl.BlockSpec((None, k, tn), lambda i, j: (layer, 0, j))
    else:
        w_spec = pl.BlockSpec((k, tn), lambda i, j: (0, j))
    in_specs = [pl.BlockSpec((tm, k), lambda i, j: (i, 0)), w_spec]
    args = [a, w]
    body = _mm_body
    if slab is not None:
        in_specs.append(pl.BlockSpec((tm, SLAB), lambda i, j: (i, 0)))
        args.append(slab)
        body = functools.partial(_mm_slab_body, rep=tn // SLAB)
    if rope is not None:
        tables, scale = rope
        period = tables[0].shape[0]
        if period == 1:
            tspec = pl.BlockSpec((1, SLAB), lambda i, j: (0, 0))
        else:
            assert period % tm == 0, (period, tm)
            nt = period // tm
            tspec = pl.BlockSpec((tm, SLAB), lambda i, j: (i % nt, 0))
        c, s1, s2 = tables
        in_specs += [tspec] * 2
        args += [c, s1 + s2]
        body = functools.partial(_mm_rope_body, rep=tn_out // SLAB, scale=scale)
        est += 4 * tm * SLAB * 4
    return pl.pallas_call(
        body,
        out_shape=jax.ShapeDtypeStruct((m, n_out), out_dtype),
        grid=(m // tm, n // tn),
        in_specs=in_specs,
        out_specs=pl.BlockSpec((tm, tn_out), lambda i, j: (i, j)),
        compiler_params=_cparams(("parallel", "parallel"), est),
        name=name,
    )(*args)


def _mm_resid_body(a_ref, w_ref, h_ref, gp_ref, gn_ref, ho_ref, xo_ref, acc_ref, *, nk):
    kk = pl.program_id(1)

    def prod():
        return jnp.dot(a_ref[...].astype(BF16), w_ref[...].astype(BF16), preferred_element_type=F32)

    @pl.when(kk == 0)
    def _():
        acc_ref[...] = prod()

    @pl.when(kk > 0)
    def _():
        acc_ref[...] += prod()

    @pl.when(kk == nk - 1)
    def _():
        h = h_ref[...] + _rms(acc_ref[...]) * gp_ref[...]
        ho_ref[...] = h
        xo_ref[...] = (_rms(h) * gn_ref[...]).astype(xo_ref.dtype)


def _matmul_resid(a, w, layer, h, g_post, g_next, name="matmul_resid"):
    m, k = a.shape
    n = w.shape[-1]
    ab, wb = a.dtype.itemsize, w.dtype.itemsize
    tk = _divisor(k, 2048 // wb, LANES)
    nk = k // tk

    def est(tm):
        return tm * n * 4 + 2 * (tm * tk * ab + tk * n * wb + 2 * tm * n * 4 + tm * n * 2) + tk * n * 2 + tm * tk * 2

    tm = _divisor(m, 1408, BF16_ROWS)
    while est(tm) > (50 << 20) and tm > BF16_ROWS:
        tm = _divisor(m, tm - BF16_ROWS, BF16_ROWS)
    row = pl.BlockSpec((tm, n), lambda i, kk: (i, 0))
    vec = pl.BlockSpec((1, n), lambda i, kk: (0, 0))
    return pl.pallas_call(
        functools.partial(_mm_resid_body, nk=nk),
        out_shape=(jax.ShapeDtypeStruct((m, n), F32), jax.ShapeDtypeStruct((m, n), BF16)),
        grid=(m // tm, nk),
        in_specs=[
            pl.BlockSpec((tm, tk), lambda i, kk: (i, kk)),
            pl.BlockSpec((None, tk, n), lambda i, kk: (layer, kk, 0)),
            row, vec, vec,
        ],
        out_specs=(row, row),
        scratch_shapes=[pltpu.VMEM((tm, n), F32)],
        compiler_params=_cparams(("parallel", "arbitrary"), est(tm)),
        name=name,
    )(a, w, h, g_post.reshape(1, n), g_next.reshape(1, n))


def _rope_tables(pos, d):
    inv = ROPE_BASE ** (-jnp.arange(0, d, 2, dtype=F32) / d)
    ang = pos.astype(F32)[:, None] * inv[None, :]
    return jnp.cos(ang), jnp.sin(ang)


def _slab_rope_tables(pos):
    cos, sin = _rope_tables(pos, MLA_ROPE)
    n, half = cos.shape
    one = jnp.ones((n, MLA_NOPE), F32)
    z_nope = jnp.zeros((n, MLA_NOPE), F32)
    z_half = jnp.zeros((n, half), F32)
    z_tail = jnp.zeros((n, SLAB - MLA_NOPE - MLA_ROPE), F32)
    c = jnp.concatenate([one, cos, cos, z_tail], axis=1)
    s1 = jnp.concatenate([z_nope, -sin, z_half, z_tail], axis=1)
    s2 = jnp.concatenate([z_nope, z_half, sin, z_tail], axis=1)
    return c, s1, s2


def _swapped_slabs(w):
    half = MLA_ROPE // 2
    nope, x1, x2 = w[..., :MLA_NOPE], w[..., MLA_NOPE:MLA_NOPE + half], w[..., MLA_NOPE + half:]
    z_tail = jnp.zeros(w.shape[:-1] + (SLAB - MLA_NOPE - MLA_ROPE,), w.dtype)
    return jnp.concatenate([nope, x1, x2, z_tail, jnp.zeros_like(nope), x2, x1, z_tail], axis=-1)


def _slab_rope(x, c, s1, s2):
    half = MLA_ROPE // 2
    return x * c + pltpu.roll(x, SLAB - half, 1) * s1 + pltpu.roll(x, half, 1) * s2


def _ret_rope(x, cos, sin):
    half = RET_DK // 2
    x1, x2 = x[:, :half], x[:, half:]
    return jnp.concatenate([x1 * cos - x2 * sin, x1 * sin + x2 * cos], axis=1)


def _ret_log_gamma():
    return jnp.log1p(-jnp.power(2.0, -5.0 - jnp.arange(RET_HEADS, dtype=F32)))


def _ret_prompt_body(lg_ref, q_ref, k_ref, v_ref, g_ref, cos_ref, sin_ref, o_ref, so_ref, st_ref, *, pad, n_chunks, hp):
    hg = pl.program_id(1)
    c = pl.program_id(2)
    chunk = q_ref.shape[0]

    @pl.when(c == 0)
    def _():
        st_ref[...] = jnp.zeros_like(st_ref)

    lead = jnp.where(c == 0, float(pad), 0.0)
    cos, sin = cos_ref[...], sin_ref[...]
    n_col = lax.broadcasted_iota(jnp.int32, (chunk, 1), 0).astype(F32)
    n_row = lax.broadcasted_iota(jnp.int32, (1, chunk), 1).astype(F32)
    diff = n_col - n_row
    for i in range(hp):
        lg = lg_ref[hg * hp + i]
        q = _ret_rope(q_ref[:, i * RET_DK:(i + 1) * RET_DK].astype(F32), cos, sin)
        k = _ret_rope(k_ref[:, i * RET_DK:(i + 1) * RET_DK].astype(F32), cos, sin) * (RET_DK ** -0.5)
        v = v_ref[:, i * RET_DV:(i + 1) * RET_DV]
        intra = jnp.where(diff >= 0, jnp.exp(lg * jnp.maximum(diff, 0.0)), 0.0)
        q_dec = jnp.exp(lg * (n_col + 1.0 - lead))
        k_dec = jnp.exp(lg * (chunk - 1.0 - n_col))
        s_dec = jnp.exp(jnp.full((1, 1), lg * (chunk - lead), F32))
        st = st_ref[i]
        scores = lax.dot_general(q.astype(BF16), k.astype(BF16), (((1,), (1,)), ((), ())),
                                 preferred_element_type=F32) * intra
        out = jnp.dot(scores.astype(BF16), v, preferred_element_type=F32)
        out = out + jnp.dot((q * q_dec).astype(BF16), st.astype(BF16), preferred_element_type=F32)
        kt = (k * k_dec).T.astype(BF16)
        st_ref[i] = st * s_dec + jnp.dot(kt, v, preferred_element_type=F32)
        g = g_ref[:, i * RET_DV:(i + 1) * RET_DV].astype(F32)
        o_ref[:, i * RET_DV:(i + 1) * RET_DV] = (_silu(g) * _rms(out)).astype(o_ref.dtype)

    @pl.when(c == n_chunks - 1)
    def _():
        so_ref[...] = st_ref[...]


def _ret_prompt(qkvg, cos, sin, pad):
    b, s, _ = qkvg.shape
    chunk = RET_CHUNK
    nc = s // chunk
    hp = _divisor(RET_HEADS, 8, 1)
    ng = RET_HEADS // hp
    body = functools.partial(_ret_prompt_body, pad=pad, n_chunks=nc, hp=hp)
    return pl.pallas_call(
        body,
        out_shape=(jax.ShapeDtypeStruct((b, s, RET_HEADS * RET_DV), BF16),
                   jax.ShapeDtypeStruct((b, RET_HEADS, RET_DK, RET_DV), F32)),
        grid=(b, ng, nc),
        in_specs=[
            pl.BlockSpec(memory_space=pltpu.SMEM),
            pl.BlockSpec((None, chunk, hp * RET_DK), lambda bi, h, c: (bi, c, h)),
            pl.BlockSpec((None, chunk, hp * RET_DK), lambda bi, h, c: (bi, c, ng + h)),
            pl.BlockSpec((None, chunk, hp * RET_DV), lambda bi, h, c: (bi, c, ng + h)),
            pl.BlockSpec((None, chunk, hp * RET_DV), lambda bi, h, c: (bi, c, 2 * ng + h)),
            pl.BlockSpec((chunk, RET_DK // 2), lambda bi, h, c: (c, 0)),
            pl.BlockSpec((chunk, RET_DK // 2), lambda bi, h, c: (c, 0)),
        ],
        out_specs=(
            pl.BlockSpec((None, chunk, hp * RET_DV), lambda bi, h, c: (bi, c, h)),
            pl.BlockSpec((None, hp, RET_DK, RET_DV), lambda bi, h, c: (bi, h, 0, 0)),
        ),
        scratch_shapes=[pltpu.VMEM((hp, RET_DK, RET_DV), F32)],
        compiler_params=_cparams(("parallel", "parallel", "arbitrary"), 16 << 20),
        name="ret_prompt",
    )(_ret_log_gamma(), qkvg, qkvg, qkvg, qkvg, cos, sin)


def _ret_sample_body(lg_ref, q_ref, k_ref, v_ref, g_ref, cos_ref, sin_ref, st_ref, o_ref, so_ref):
    h = pl.program_id(1)
    bt = q_ref.shape[0]
    gamma = jnp.exp(jnp.full((1, 1), lg_ref[h], F32))
    cos, sin = cos_ref[...], sin_ref[...]
    q = _ret_rope(q_ref[...], cos, sin)
    k = _ret_rope(k_ref[...], cos, sin) * (RET_DK ** -0.5)
    v = v_ref[...]
    qk = jnp.sum(q * k, axis=-1, keepdims=True)
    qg = (q * gamma).astype(BF16)
    eye = lax.broadcasted_iota(jnp.int32, (RET_DK, RET_DK), 0) == lax.broadcasted_iota(jnp.int32, (RET_DK, RET_DK), 1)
    rows = []
    for i in range(bt):
        st = st_ref[i]
        cross = jnp.dot(qg, st.astype(BF16), preferred_element_type=F32)
        rows.append(cross[i:i + 1])
        k_col = jnp.sum(jnp.where(eye, k[i:i + 1], 0.0), axis=1, keepdims=True)
        so_ref[i] = st * gamma + k_col * v[i:i + 1]
    out = qk * v + jnp.concatenate(rows, axis=0)
    o_ref[...] = _silu(g_ref[...]) * _rms(out)


def _ret_sample(qkvg, state, cos, sin, out_buf, layer):
    db = qkvg.shape[0]
    bt = SUBLANES
    kq = RET_HEADS
    n_layers = state.shape[0]
    st_block = (None, bt, None, RET_DK, RET_DV)
    in_specs = [
        pl.BlockSpec(memory_space=pltpu.SMEM),
        pl.BlockSpec((bt, RET_DK), lambda bi, h: (bi, h)),
        pl.BlockSpec((bt, RET_DK), lambda bi, h: (bi, kq + h)),
        pl.BlockSpec((bt, RET_DV), lambda bi, h: (bi, kq + h)),
        pl.BlockSpec((bt, RET_DV), lambda bi, h: (bi, 2 * kq + h)),
        pl.BlockSpec((1, RET_DK // 2), lambda bi, h: (0, 0)),
        pl.BlockSpec((1, RET_DK // 2), lambda bi, h: (0, 0)),
        pl.BlockSpec(st_block, lambda bi, h: (layer, bi, h, 0, 0)),
    ]
    args = [_ret_log_gamma(), qkvg, qkvg, qkvg, qkvg, cos, sin, state]
    aliases = {}
    body = _ret_sample_body
    if out_buf is not None:
        in_specs.append(pl.BlockSpec(memory_space=pl.ANY))
        args.append(out_buf)
        aliases = {len(args) - 1: 1}
        body = lambda *refs: _ret_sample_body(*refs[:8], *refs[9:])
    return pl.pallas_call(
        body,
        out_shape=(jax.ShapeDtypeStruct((db, RET_HEADS * RET_DV), F32),
                   jax.ShapeDtypeStruct((n_layers, db, RET_HEADS, RET_DK, RET_DV), F32)),
        grid=(db // bt, RET_HEADS),
        in_specs=in_specs,
        out_specs=(
            pl.BlockSpec((bt, RET_DV), lambda bi, h: (bi, h)),
            pl.BlockSpec(st_block, lambda bi, h: (layer, bi, h, 0, 0)),
        ),
        input_output_aliases=aliases,
        compiler_params=_cparams(("parallel", "parallel"), 4 * bt * RET_DK * RET_DV * 4 + (4 << 20)),
        name="ret_sample",
    )(*args)


def _conv_act(g2, g1, g0, up, cw_ref, cb_ref):
    conv = g2 * cw_ref[0:1, :] + g1 * cw_ref[1:2, :] + g0 * cw_ref[2:3, :] + cb_ref[...]
    return _silu(conv) * up


def _ffn1_seq_body(x_ref, wg_ref, wu_ref, cw_ref, cb_ref, act_ref, cs_ref, carry_ref):
    i = pl.program_id(1)
    j = pl.program_id(2)
    x = x_ref[...]
    gate = jnp.dot(x, wg_ref[...].astype(BF16), preferred_element_type=F32)
    up = jnp.dot(x, wu_ref[...].astype(BF16), preferred_element_type=F32)
    tm = gate.shape[0]
    head = 2 * SUBLANES
    act = _conv_act(pltpu.roll(gate, 2, 0), pltpu.roll(gate, 1, 0), gate, up, cw_ref, cb_ref)
    act_ref[head:, :] = act[head:].astype(act_ref.dtype)
    @pl.when(i == 0)
    def _():
        carry_ref[j] = jnp.zeros(carry_ref.shape[1:], F32)

    win = jnp.concatenate([carry_ref[j], gate[:head]], axis=0)
    w1 = pltpu.roll(win, 1, 0)[SUBLANES:]
    w2 = pltpu.roll(win, 2, 0)[SUBLANES:]
    act_ref[:head, :] = _conv_act(w2, w1, gate[:head], up[:head], cw_ref, cb_ref).astype(act_ref.dtype)
    tail = gate[tm - SUBLANES:]
    carry_ref[j] = tail
    cs_ref[...] = tail


def _ffn1_seq(x, w_gate, w_up, conv_w, conv_b, layer, n_seq):
    m, k = x.shape
    f = w_gate.shape[-1]
    s = m // n_seq
    tm = _divisor(s, 1408, BF16_ROWS)
    tn = _divisor(f, 512, LANES)
    ni, nj = s // tm, f // tn
    est = 2 * (tm * k * 2 + 2 * k * tn * 4 + tm * tn * 2) + 2 * k * tn * 2 + 4 * tm * tn * 4
    act, tails = pl.pallas_call(
        _ffn1_seq_body,
        out_shape=(jax.ShapeDtypeStruct((m, f), BF16), jax.ShapeDtypeStruct((n_seq, ni, SUBLANES, f), F32)),
        grid=(n_seq, ni, nj),
        in_specs=[
            pl.BlockSpec((tm, k), lambda b, i, j: (b * ni + i, 0)),
            pl.BlockSpec((None, k, tn), lambda b, i, j: (layer, 0, j)),
            pl.BlockSpec((None, k, tn), lambda b, i, j: (layer, 0, j)),
            pl.BlockSpec((None, CONV_W, tn), lambda b, i, j: (layer, 0, j)),
            pl.BlockSpec((None, 1, tn), lambda b, i, j: (layer, 0, j)),
        ],
        out_specs=(
            pl.BlockSpec((tm, tn), lambda b, i, j: (b * ni + i, j)),
            pl.BlockSpec((None, None, SUBLANES, tn), lambda b, i, j: (b, i, 0, j)),
        ),
        scratch_shapes=[pltpu.VMEM((nj, SUBLANES, tn), F32)],
        compiler_params=_cparams(("arbitrary", "arbitrary", "arbitrary"), est),
        name="ffn1_seq",
    )(x, w_gate, w_up, conv_w, conv_b.reshape(conv_b.shape[0], 1, f))
    return act, tails[:, ni - 1]


def _ffn1_tok_body(x_ref, wg_ref, wu_ref, cw_ref, cb_ref, s0_ref, s1_ref, act_ref, gate_ref):
    x = x_ref[...]
    gate = jnp.dot(x, wg_ref[...].astype(BF16), preferred_element_type=F32)
    up = jnp.dot(x, wu_ref[...].astype(BF16), preferred_element_type=F32)
    act_ref[...] = _conv_act(s0_ref[...], s1_ref[...], gate, up, cw_ref, cb_ref).astype(act_ref.dtype)
    gate_ref[...] = gate


def _ffn1_tok(x, w_gate, w_up, conv_w, conv_b, layer, conv_state):
    m, k = x.shape
    f = w_gate.shape[-1]
    tn = _divisor(f, 1024, LANES)
    nj = f // tn
    cs = conv_state.reshape(m, (CONV_W - 1) * f)
    est = 2 * (m * k * 2 + 2 * k * tn * 4 + 6 * m * tn * 4) + 2 * k * tn * 2
    return pl.pallas_call(
        _ffn1_tok_body,
        out_shape=(jax.ShapeDtypeStruct((m, f), BF16), jax.ShapeDtypeStruct((m, f), F32)),
        grid=(nj,),
        in_specs=[
            pl.BlockSpec((m, k), lambda j: (0, 0)),
            pl.BlockSpec((None, k, tn), lambda j: (layer, 0, j)),
            pl.BlockSpec((None, k, tn), lambda j: (layer, 0, j)),
            pl.BlockSpec((None, CONV_W, tn), lambda j: (layer, 0, j)),
            pl.BlockSpec((None, 1, tn), lambda j: (layer, 0, j)),
            pl.BlockSpec((m, tn), lambda j: (0, j)),
            pl.BlockSpec((m, tn), lambda j: (0, nj + j)),
        ],
        out_specs=(pl.BlockSpec((m, tn), lambda j: (0, j)), pl.BlockSpec((m, tn), lambda j: (0, j))),
        compiler_params=_cparams(("parallel",), est),
        name="ffn1_tok",
    )(x, w_gate, w_up, conv_w, conv_b.reshape(conv_b.shape[0], 1, f), cs, cs)


def _kv_post_body(y_ref, g_ref, c_ref, s1_ref, s2_ref, cf_ref, cb_ref, r_ref):
    y = y_ref[...]
    c = _rms(y[:, :KV_LORA]) * g_ref[...]
    cf_ref[...] = c
    cb_ref[...] = c.astype(cb_ref.dtype)
    r_ref[...] = _slab_rope(y[:, KV_LORA:], c_ref[...], s1_ref[...], s2_ref[...])


def _kv_post(y, g, tables):
    m = y.shape[0]
    t_rows = tables[0].shape[0]
    tm = _divisor(m if t_rows == 1 else t_rows, 512, BF16_ROWS)
    if t_rows == 1:
        tspec = pl.BlockSpec((1, SLAB), lambda i: (0, 0))
    else:
        nt = t_rows // tm
        tspec = pl.BlockSpec((tm, SLAB), lambda i: (i % nt, 0))
    return pl.pallas_call(
        _kv_post_body,
        out_shape=(jax.ShapeDtypeStruct((m, KV_LORA), F32), jax.ShapeDtypeStruct((m, KV_LORA), BF16),
                   jax.ShapeDtypeStruct((m, SLAB), F32)),
        grid=(m // tm,),
        in_specs=[pl.BlockSpec((tm, KV_LORA + SLAB), lambda i: (i, 0)), pl.BlockSpec((1, KV_LORA), lambda i: (0, 0)),
                  tspec, tspec, tspec],
        out_specs=(pl.BlockSpec((tm, KV_LORA), lambda i: (i, 0)), pl.BlockSpec((tm, KV_LORA), lambda i: (i, 0)),
                   pl.BlockSpec((tm, SLAB), lambda i: (i, 0))),
        compiler_params=_cparams(("parallel",), 16 << 20),
        name="kv_post",
    )(y, g.reshape(1, KV_LORA), *tables)


ATTN_BLOCK = 3 * LANES


def _v_up_t_body(c_ref, w_ref, o_ref):
    o_ref[...] = lax.dot_general(w_ref[...].astype(BF16), c_ref[...], (((1,), (1,)), ((), ())),
                                 preferred_element_type=F32).astype(o_ref.dtype)


def _v_up_t(c, w_uv_t, n_seq, tk):
    m, kl = c.shape
    n = w_uv_t.shape[0]
    nk = m // n_seq // tk
    tn = _divisor(n, 1024, LANES)
    return pl.pallas_call(
        _v_up_t_body,
        out_shape=jax.ShapeDtypeStruct((n_seq, nk, n, tk), BF16),
        grid=(n_seq, nk, n // tn),
        in_specs=[pl.BlockSpec((tk, kl), lambda b, kb, j: (b * nk + kb, 0)), pl.BlockSpec((tn, kl), lambda b, kb, j: (j, 0))],
        out_specs=pl.BlockSpec((None, None, tn, tk), lambda b, kb, j: (b, kb, j, 0)),
        compiler_params=_cparams(("parallel", "parallel", "parallel"), 16 << 20),
        name="v_up_t",
    )(c, w_uv_t)


def _attn_prompt_body(q_ref, k_ref, vt_ref, bias_ref, o_ref, m_ref, l_ref, acc_ref, s_ref, *, pad, blk, hp):
    qi = pl.program_id(2)
    m_ref[...] = jnp.full_like(m_ref, NEG)
    l_ref[...] = jnp.zeros_like(l_ref)
    acc_ref[...] = jnp.zeros_like(acc_ref)
    nt = (((1,), (1,)), ((), ()))

    def scores(ki, slot):
        start = ki * blk if isinstance(ki, int) else pl.multiple_of(ki * blk, blk)
        for i in range(hp):
            kb = k_ref[pl.ds(start, blk), i * SLAB:(i + 1) * SLAB]
            s_ref[slot, i] = lax.dot_general(kb, q_ref[:, i * SLAB:(i + 1) * SLAB], nt, preferred_element_type=F32)

    def consume(ki, slot, kind):
        for i in range(hp):
            s = s_ref[slot, i]
            if kind == "first":
                s = s[pad:]
            elif kind == "diag":
                s = s + bias_ref[jnp.where(qi == 0, 0, 1)]
            m_old = m_ref[i]
            m_new = jnp.maximum(m_old, jnp.max(s, axis=0, keepdims=True))
            alpha = jnp.exp(m_old - m_new)
            p = jnp.exp(s - m_new)
            l_ref[i] = alpha * l_ref[i] + jnp.sum(p, axis=0, keepdims=True)
            pb = p.astype(BF16)
            if kind == "first":
                pb = jnp.concatenate([jnp.zeros((pad, blk), BF16), pb], axis=0)
            vt = vt_ref[ki, i * MLA_V:(i + 1) * MLA_V, :]
            acc_ref[i] = alpha * acc_ref[i] + jnp.dot(vt, pb, preferred_element_type=F32)
            m_ref[i] = m_new

    scores(0, 0)

    @pl.when(qi > 0)
    def _():
        scores(1, 1)
        consume(0, 0, "first")

    def mid(ki, carry):
        slot = lax.rem(ki, 2)
        consume(ki, slot, "full")
        scores(ki + 1, 1 - slot)
        return carry

    lax.fori_loop(1, qi, mid, 0)
    consume(qi, lax.rem(qi, 2), "diag")

    for i in range(hp):
        o_ref[:, i * MLA_V:(i + 1) * MLA_V] = (acc_ref[i] / l_ref[i]).T.astype(o_ref.dtype)


def _attn_bias(blk, pad):
    kpos = jnp.arange(blk)[:, None]
    qpos = jnp.arange(blk)[None, :]
    causal = kpos <= qpos
    first = causal & ((kpos >= pad) | (qpos < pad))
    return jnp.stack([jnp.where(first, 0.0, NEG), jnp.where(causal, 0.0, NEG)]).astype(F32)


def _attn_prompt(q, k, vt, pad):
    b, s, _ = q.shape
    nk, blk = vt.shape[1], vt.shape[3]
    hp = _divisor(MLA_HEADS, 4, 1)
    body = functools.partial(_attn_prompt_body, pad=pad, blk=blk, hp=hp)
    est = (2 * (blk * hp * SLAB * 2 + s * hp * SLAB * 2 + s * hp * MLA_V * 2 + 2 * blk * blk * 4 + blk * hp * MLA_V * 2)
           + 8 * hp * blk * blk * 4)
    return pl.pallas_call(
        body,
        out_shape=jax.ShapeDtypeStruct((b, s, MLA_HEADS * MLA_V), BF16),
        grid=(b, MLA_HEADS // hp, s // blk),
        in_specs=[
            pl.BlockSpec((None, blk, hp * SLAB), lambda bi, h, qi: (bi, qi, h)),
            pl.BlockSpec((None, s, hp * SLAB), lambda bi, h, qi: (bi, 0, h)),
            pl.BlockSpec((None, nk, hp * MLA_V, blk), lambda bi, h, qi: (bi, 0, h, 0)),
            pl.BlockSpec((2, blk, blk), lambda bi, h, qi: (0, 0, 0)),
        ],
        out_specs=pl.BlockSpec((None, blk, hp * MLA_V), lambda bi, h, qi: (bi, qi, h)),
        scratch_shapes=[pltpu.VMEM((hp, 1, blk), F32), pltpu.VMEM((hp, 1, blk), F32), pltpu.VMEM((hp, MLA_V, blk), F32),
                        pltpu.VMEM((2, hp, blk, blk), F32)],
        compiler_params=_cparams(("parallel", "parallel", "arbitrary"), est),
        name="attn_prompt",
    )(q, k, vt, _attn_bias(blk, pad))


def _q_absorb_body(q_ref, w_ref, ql_ref):
    ql_ref[...] = lax.dot_general(q_ref[...][:, :MLA_NOPE].astype(BF16), w_ref[...].astype(BF16), (((1,), (1,)), ((), ())),
                                  preferred_element_type=F32)


def _q_absorb(q, w_uk2):
    db = q.shape[0]
    return pl.pallas_call(
        _q_absorb_body,
        out_shape=jax.ShapeDtypeStruct((db, MLA_HEADS * KV_LORA), F32),
        grid=(MLA_HEADS,),
        in_specs=[pl.BlockSpec((db, SLAB), lambda h: (0, h)), pl.BlockSpec((KV_LORA, MLA_NOPE), lambda h: (0, h))],
        out_specs=pl.BlockSpec((db, KV_LORA), lambda h: (0, h)),
        compiler_params=_cparams(("parallel",), 8 << 20),
        name="q_absorb",
    )(q, w_uk2)


def _attn_sample_body(pt_ref, ql_ref, qs_ref, cn_ref, rn_ref, *refs, pages, n_groups, nb):
    n_page_refs = nb * pages
    lat_refs = refs[:n_page_refs]
    rope_refs = refs[n_page_refs:2 * n_page_refs]
    o_ref, m_ref, l_ref, acc_ref, latb_ref, ropeb_ref = refs[2 * n_page_refs:]
    g = pl.program_id(1)
    page = lat_refs[0].shape[0]

    @pl.when(g == 0)
    def _():
        m_ref[...] = jnp.full_like(m_ref, NEG)
        l_ref[...] = jnp.zeros_like(l_ref)
        acc_ref[...] = jnp.zeros_like(acc_ref)

    nt = (((1,), (1,)), ((), ()))
    for b in range(nb):
        for i in range(pages):
            latb_ref[b, i * page:(i + 1) * page, :] = lat_refs[b * pages + i][...].astype(BF16)
            ropeb_ref[b, :, i * page:(i + 1) * page] = rope_refs[b * pages + i][...].astype(BF16)
        ql = ql_ref[b]
        qr = qs_ref[b][:, MLA_NOPE:MLA_NOPE + MLA_ROPE]
        s = lax.dot_general(ql.astype(BF16), latb_ref[b], nt, preferred_element_type=F32)
        s = s + jnp.dot(qr.astype(BF16), ropeb_ref[b], preferred_element_type=F32)
        m_old = m_ref[b]
        m_new = jnp.maximum(m_old, jnp.max(s, axis=-1, keepdims=True))
        alpha = jnp.exp(m_old - m_new)
        p = jnp.exp(s - m_new)
        l_ref[b] = alpha * l_ref[b] + jnp.sum(p, axis=-1, keepdims=True)
        acc_ref[b] = alpha * acc_ref[b] + jnp.dot(p.astype(BF16), latb_ref[b], preferred_element_type=F32)
        m_ref[b] = m_new

    @pl.when(g == n_groups - 1)
    def _():
        for b in range(nb):
            ql = ql_ref[b]
            qr = qs_ref[b][:, MLA_NOPE:MLA_NOPE + MLA_ROPE]
            cn = cn_ref[b]
            rn = rn_ref[b][:, MLA_NOPE:MLA_NOPE + MLA_ROPE]
            s_new = jnp.sum(ql * cn, axis=-1, keepdims=True) + jnp.sum(qr * rn, axis=-1, keepdims=True)
            m_old = m_ref[b]
            m_new = jnp.maximum(m_old, s_new)
            alpha = jnp.exp(m_old - m_new)
            p_new = jnp.exp(s_new - m_new)
            l = alpha * l_ref[b] + p_new
            o_ref[b] = (alpha * acc_ref[b] + p_new * cn) / l


def _attn_sample(page_table, cache_lat, cache_rope_t, q_lat, q_slab, c_new, r_new):
    db, n_pages = page_table.shape
    page = cache_lat.shape[1]
    pages = _divisor(n_pages, 16, 1)
    nb = _divisor(db, 4, 1)
    n_groups = n_pages // pages
    body = functools.partial(_attn_sample_body, pages=pages, n_groups=n_groups, nb=nb)

    def page_spec(shape, b, i):
        return pl.BlockSpec((None,) + shape, lambda bi, g, pt: (pt[bi * nb + b, g * pages + i], 0, 0))

    def seq_spec(rows, width):
        return pl.BlockSpec((nb, rows, width), lambda bi, g, pt: (bi, 0, 0))

    in_specs = [seq_spec(MLA_HEADS, KV_LORA), seq_spec(MLA_HEADS, SLAB), seq_spec(1, KV_LORA), seq_spec(1, SLAB)]
    in_specs += [page_spec((page, KV_LORA), b, i) for b in range(nb) for i in range(pages)]
    in_specs += [page_spec((MLA_ROPE, page), b, i) for b in range(nb) for i in range(pages)]
    keys = pages * page
    est = nb * (2 * keys * (KV_LORA + MLA_ROPE) * 4 + 2 * keys * (KV_LORA + MLA_ROPE) * 2) + (4 << 20)
    return pl.pallas_call(
        body,
        out_shape=jax.ShapeDtypeStruct((db, MLA_HEADS, KV_LORA), F32),
        grid_spec=pltpu.PrefetchScalarGridSpec(
            num_scalar_prefetch=1,
            grid=(db // nb, n_groups),
            in_specs=in_specs,
            out_specs=seq_spec(MLA_HEADS, KV_LORA),
            scratch_shapes=[
                pltpu.VMEM((nb, MLA_HEADS, 1), F32), pltpu.VMEM((nb, MLA_HEADS, 1), F32),
                pltpu.VMEM((nb, MLA_HEADS, KV_LORA), F32),
                pltpu.VMEM((nb, keys, KV_LORA), BF16), pltpu.VMEM((nb, MLA_ROPE, keys), BF16),
            ],
        ),
        compiler_params=_cparams(("parallel", "arbitrary"), est),
        name="attn_sample",
    )(page_table, q_lat, q_slab, c_new, r_new, *([cache_lat] * (nb * pages)), *([cache_rope_t] * (nb * pages)))


def _head_out_body(o_ref, w_ref, y_ref):
    y_ref[...] = jnp.dot(o_ref[...].astype(BF16), w_ref[...].astype(BF16), preferred_element_type=F32).astype(y_ref.dtype)


def _head_out(o_lat, w_uv2):
    db = o_lat.shape[0]
    return pl.pallas_call(
        _head_out_body,
        out_shape=jax.ShapeDtypeStruct((db, MLA_HEADS * MLA_V), BF16),
        grid=(MLA_HEADS,),
        in_specs=[pl.BlockSpec((db, KV_LORA), lambda h: (0, h)), pl.BlockSpec((KV_LORA, MLA_V), lambda h: (0, h))],
        out_specs=pl.BlockSpec((db, MLA_V), lambda h: (0, h)),
        compiler_params=_cparams(("parallel",), 4 << 20),
        name="head_out",
    )(o_lat, w_uv2)


def _prep_weights(ret_w_qkvg, ret_w_o, ffn_w_gate, ffn_w_up, ffn_w_down, w_dkv, w_kr, w_uk, w_uv, mla_w_dq, mla_w_uq,
                  mla_w_o):
    tail = SLAB - MLA_NOPE - MLA_ROPE
    w_kv = jnp.concatenate([w_dkv, jnp.zeros((D_MODEL, MLA_NOPE), F32), w_kr, jnp.zeros((D_MODEL, tail), F32)], axis=1)
    w_uk_ext = jnp.pad(w_uk, ((0, 0), (0, 0), (0, SLAB - MLA_NOPE))).reshape(KV_LORA, MLA_HEADS * SLAB)
    n_b = mla_w_uq.shape[0]
    w_uq_ext = _swapped_slabs(mla_w_uq.reshape(n_b, Q_LORA, MLA_HEADS, MLA_NOPE + MLA_ROPE)).reshape(n_b, Q_LORA, -1)
    w_uv2 = w_uv.reshape(KV_LORA, MLA_HEADS * MLA_V)
    return dict(
        qkvg=ret_w_qkvg, ret_o=ret_w_o, gate=ffn_w_gate, up=ffn_w_up, down=ffn_w_down, kv=w_kv, uk_ext=w_uk_ext,
        uk2=w_uk.reshape(KV_LORA, MLA_HEADS * MLA_NOPE), uv2=w_uv2, uv_t=w_uv2.T, dq=mla_w_dq, uq_ext=w_uq_ext,
        mla_o=mla_w_o,
    )


def _trunk(h, w, norm_g, ffn_conv_w, ffn_conv_b, kv_in_g, kv_norm_g, mla_q_norm_g, ret_mixer, ffn1, kv_tables, attend):
    conv_states = []
    c_f32 = r_slab = kv_ctx = None
    q_scale = (MLA_NOPE + MLA_ROPE) ** -0.5
    xn = _rms_cast(h, norm_g[0, 0])
    for layer in range(DEPTH):
        g = norm_g[layer]
        if layer < N_A_LAYERS:
            qkvg = _matmul(xn, w["qkvg"], ret_mixer.qkvg_dtype, layer=layer, name="qkvg")
            gated = ret_mixer(layer, qkvg)
            h, xn = _matmul_resid(gated, w["ret_o"], layer, h, g[1], g[2], name="ret_o")
        else:
            j = layer - N_A_LAYERS
            cq = _matmul(xn, w["dq"], F32, layer=j, name="dq")
            cqn = _rms_cast(cq, mla_q_norm_g[j])
            q = _matmul(cqn, w["uq_ext"], attend.q_dtype, layer=j, rope=(kv_tables, q_scale), name="uq")
            o = attend(q, kv_ctx)
            h, xn = _matmul_resid(o, w["mla_o"], j, h, g[1], g[2], name="mla_o")
        act, cs = ffn1(layer, xn, w["gate"], w["up"], ffn_conv_w, ffn_conv_b)
        conv_states.append(cs)
        g_next = norm_g[layer + 1, 0] if layer + 1 < DEPTH else g[3]
        h, xn = _matmul_resid(act, w["down"], layer, h, g[3], g_next, name="down")
        if layer == N_A_LAYERS - 1:
            hn = _rms_cast(h, kv_in_g)
            y = _matmul(hn, w["kv"], F32, name="kv_down")
            c_f32, c_bf16, r_slab = _kv_post(y, kv_norm_g, kv_tables)
            kv_ctx = (c_f32, c_bf16, r_slab)
    return h, conv_states, c_f32, r_slab


class _PromptRetention:
    qkvg_dtype = BF16

    def __init__(self, n_seq, pad, cos, sin):
        self.n_seq, self.pad, self.cos, self.sin = n_seq, pad, cos, sin
        self.states = []

    def __call__(self, layer, qkvg):
        m, n = qkvg.shape
        gated, state = _ret_prompt(qkvg.reshape(self.n_seq, m // self.n_seq, n), self.cos, self.sin, self.pad)
        self.states.append(state)
        return gated.reshape(m, -1)


class _SampleRetention:
    qkvg_dtype = F32

    def __init__(self, state, cos, sin):
        self.state, self.cos, self.sin = state, cos, sin
        self.out = None

    def __call__(self, layer, qkvg):
        gated, self.out = _ret_sample(qkvg, self.state, self.cos, self.sin, self.out, layer)
        return gated


def kernel(x_prompt, x_sample, state_retention, state_conv, cache_kv_latent, cache_k_rope, page_table, meta_tokens, norm_g,
           ret_w_qkvg, ret_w_o, ffn_w_gate, ffn_w_up, ffn_w_down, ffn_conv_w, ffn_conv_b, kv_in_g, w_dkv, kv_norm_g, w_kr,
           w_uk, w_uv, mla_w_dq, mla_q_norm_g, mla_w_uq, mla_w_o):
    w = _prep_weights(ret_w_qkvg, ret_w_o, ffn_w_gate, ffn_w_up, ffn_w_down, w_dkv, w_kr, w_uk, w_uv, mla_w_dq, mla_w_uq,
                      mla_w_o)
    shared = (norm_g, ffn_conv_w, ffn_conv_b, kv_in_g, kv_norm_g, mla_q_norm_g)

    b, seq, d = x_prompt.shape
    pad = RET_CHUNK - N_META
    s_pad = pad + N_META + seq
    h0 = jnp.concatenate([jnp.zeros((b, pad, d), F32), jnp.broadcast_to(meta_tokens[None], (b, N_META, d)), x_prompt], axis=1)
    pos_p = jnp.arange(s_pad) - pad
    cos_p, sin_p = _rope_tables(pos_p, RET_DK)
    tables_p = _slab_rope_tables(pos_p)
    ret_p = _PromptRetention(b, pad, cos_p, sin_p)

    def ffn1_p(layer, xn, wg, wu, cw, cb):
        act, tail = _ffn1_seq(xn, wg, wu, cw, cb, layer, b)
        return act, tail[:, SUBLANES - (CONV_W - 1):, :]

    def attend_p(q, kv_ctx):
        if "k" not in attend_p.cache:
            _, c_bf16, r_slab = kv_ctx
            attend_p.cache["k"] = _matmul(c_bf16, w["uk_ext"], BF16, slab=r_slab, name="k_up").reshape(b, s_pad, -1)
            attend_p.cache["vt"] = _v_up_t(c_bf16, w["uv_t"], b, _divisor(s_pad, ATTN_BLOCK, LANES))
        o = _attn_prompt(q.reshape(b, s_pad, -1), attend_p.cache["k"], attend_p.cache["vt"], pad)
        return o.reshape(b * s_pad, -1)

    attend_p.cache = {}
    attend_p.q_dtype = BF16
    h_p, conv_p, lat_p, rslab_p = _trunk(h0.reshape(b * s_pad, d), w, *shared, ret_p, ffn1_p, tables_p, attend_p)
    y_prompt = h_p.reshape(b, s_pad, d)[:, pad + N_META:]
    lat_prompt = lat_p.reshape(b, s_pad, KV_LORA)[:, pad:]
    rope_prompt = rslab_p.reshape(b, s_pad, SLAB)[:, pad:, MLA_NOPE:MLA_NOPE + MLA_ROPE]

    db = x_sample.shape[0]
    pos_s = jnp.full((1,), PAST_LEN)
    cos_s, sin_s = _rope_tables(pos_s, RET_DK)
    tables_s = _slab_rope_tables(pos_s)
    ret_s = _SampleRetention(state_retention, cos_s, sin_s)
    cache_rope_t = jnp.swapaxes(cache_k_rope, 1, 2)

    def ffn1_s(layer, xn, wg, wu, cw, cb):
        act, gate = _ffn1_tok(xn, wg, wu, cw, cb, layer, state_conv[layer])
        return act, jnp.stack([state_conv[layer][:, 1], gate], axis=1)

    def attend_s(q, kv_ctx):
        c_f32, _, r_slab = kv_ctx
        q_lat = _q_absorb(q, w["uk2"])
        o_lat = _attn_sample(page_table, cache_kv_latent, cache_rope_t, q_lat.reshape(db, MLA_HEADS, KV_LORA),
                             q.reshape(db, MLA_HEADS, SLAB), c_f32.reshape(db, 1, KV_LORA), r_slab.reshape(db, 1, SLAB))
        return _head_out(o_lat.reshape(db, MLA_HEADS * KV_LORA), w["uv2"])

    attend_s.q_dtype = F32

    h_s, conv_s, lat_s, rslab_s = _trunk(x_sample.reshape(db, d), w, *shared, ret_s, ffn1_s, tables_s, attend_s)

    return (
        y_prompt,
        h_s.reshape(db, 1, d),
        jnp.stack(ret_p.states),
        ret_s.out,
        jnp.stack(conv_p),
        jnp.stack(conv_s),
        lat_prompt,
        lat_s.reshape(db, 1, KV_LORA),
        rope_prompt,
        rslab_s[:, MLA_NOPE:MLA_NOPE + MLA_ROPE].reshape(db, 1, MLA_ROPE),
    )
```

```python
import functools

import jax
import jax.numpy as jnp
from jax import lax
from jax.experimental import pallas as pl
from jax.experimental.pallas import tpu as pltpu

D_MODEL = 2048
SEQ = 4096
DEPTH = 4
PAST_LEN = 8192
PAGE_SIZE = 128
N_META = 16
N_A_LAYERS = DEPTH // 2
RET_HEADS = 8
RET_DK = D_MODEL // RET_HEADS
RET_DV = 2 * D_MODEL // RET_HEADS
RET_CHUNK = 128
MLA_HEADS = D_MODEL // 128
MLA_NOPE = 128
MLA_ROPE = 64
MLA_V = 128
Q_LORA = 512
KV_LORA = 512
D_FF = 11 * D_MODEL // 4
CONV_W = 3
ROPE_BASE = 10000.0
EPS = 1e-6

V7X_VMEM_BYTES = 64 * 1024 * 1024
VMEM_REQUEST_CAP = V7X_VMEM_BYTES - 8 * 1024 * 1024
LANES = 128
SUBLANES = 8
BF16_ROWS = 16
SLAB = 2 * LANES
NEG = -1e30

F32 = jnp.float32
BF16 = jnp.bfloat16


def _cparams(semantics, vmem_bytes):
    limit = int(min(max(vmem_bytes + (6 << 20), 16 << 20), VMEM_REQUEST_CAP))
    return pltpu.CompilerParams(dimension_semantics=semantics, vmem_limit_bytes=limit)


def _divisor(n, cap, mult):
    d = (min(n, cap) // mult) * mult
    while d >= mult:
        if n % d == 0:
            return d
        d -= mult
    return n


def _silu(x):
    return x * (1.0 / (1.0 + jnp.exp(-x)))


def _rms(x):
    return x * lax.rsqrt(jnp.mean(x * x, axis=-1, keepdims=True) + EPS)


def _rms_cast_body(x_ref, g_ref, o_ref):
    o_ref[...] = (_rms(x_ref[...]) * g_ref[...]).astype(o_ref.dtype)


def _rms_cast(x, g, out_dtype=BF16):
    m, d = x.shape
    tm = _divisor(m, 512, BF16_ROWS)
    return pl.pallas_call(
        _rms_cast_body,
        out_shape=jax.ShapeDtypeStruct((m, d), out_dtype),
        grid=(m // tm,),
        in_specs=[pl.BlockSpec((tm, d), lambda i: (i, 0)), pl.BlockSpec((1, d), lambda i: (0, 0))],
        out_specs=pl.BlockSpec((tm, d), lambda i: (i, 0)),
        compiler_params=_cparams(("parallel",), 2 * tm * d * 6 + tm * d * 8),
        name="rms_cast",
    )(x, g.reshape(1, d))


def _embed_norm_body(x_ref, meta_ref, g_ref, h_ref, xo_ref, *, pad):
    def emit(rows):
        h_ref[...] = rows
        xo_ref[...] = (_rms(rows) * g_ref[...]).astype(xo_ref.dtype)

    @pl.when(pl.program_id(1) == 0)
    def _():
        emit(jnp.concatenate([jnp.zeros((pad, meta_ref.shape[1]), F32), meta_ref[...]], axis=0))

    @pl.when(pl.program_id(1) > 0)
    def _():
        emit(x_ref[...])


def _embed_norm(x, meta, g, pad):
    b, seq, d = x.shape
    blk = pad + meta.shape[0]
    assert seq % blk == 0, (seq, blk)
    nblk = seq // blk + 1
    out = pl.BlockSpec((blk, d), lambda bi, i: (bi * nblk + i, 0))
    return pl.pallas_call(
        functools.partial(_embed_norm_body, pad=pad),
        out_shape=(jax.ShapeDtypeStruct((b * nblk * blk, d), F32), jax.ShapeDtypeStruct((b * nblk * blk, d), BF16)),
        grid=(b, nblk),
        in_specs=[pl.BlockSpec((None, blk, d), lambda bi, i: (bi, jnp.maximum(i - 1, 0), 0)),
                  pl.BlockSpec(meta.shape, lambda bi, i: (0, 0)), pl.BlockSpec((1, d), lambda bi, i: (0, 0))],
        out_specs=(out, out),
        compiler_params=_cparams(("parallel", "arbitrary"), 16 * blk * d * 4),
        name="embed_norm",
    )(x, meta, g.reshape(1, d))


def _resid_norm_body(h_ref, y_ref, gp_ref, gn_ref, ho_ref, xo_ref):
    h = h_ref[...] + _rms(y_ref[...]) * gp_ref[...]
    ho_ref[...] = h
    xo_ref[...] = (_rms(h) * gn_ref[...]).astype(xo_ref.dtype)


def _resid_norm(h, y, g_post, g_next):
    m, d = h.shape
    tm = _divisor(m, 256, BF16_ROWS)
    row = pl.BlockSpec((tm, d), lambda i: (i, 0))
    vec = pl.BlockSpec((1, d), lambda i: (0, 0))
    return pl.pallas_call(
        _resid_norm_body,
        out_shape=(jax.ShapeDtypeStruct((m, d), F32), jax.ShapeDtypeStruct((m, d), BF16)),
        grid=(m // tm,),
        in_specs=[row, row, vec, vec],
        out_specs=(row, row),
        compiler_params=_cparams(("parallel",), 2 * tm * d * 14 + tm * d * 12),
        name="resid_norm",
    )(h, y, g_post.reshape(1, d), g_next.reshape(1, d))


def _mm_body(a_ref, w_ref, o_ref):
    a = a_ref[...].astype(BF16)
    o_ref[...] = jnp.dot(a, w_ref[...].astype(BF16), preferred_element_type=F32).astype(o_ref.dtype)


def _mm_slab_body(a_ref, w_ref, s_ref, o_ref, *, rep):
    a = a_ref[...].astype(BF16)
    acc = jnp.dot(a, w_ref[...].astype(BF16), preferred_element_type=F32)
    s = s_ref[...]
    o_ref[...] = (acc + jnp.concatenate([s] * rep, axis=1)).astype(o_ref.dtype)


def _mm_tiles(m, k, n, ab, wb, ob, mult, budget):
    def est(tm, tn):
        return 2 * (tm * k * ab + k * tn * wb + tm * tn * ob) + tm * tn * 4 + tm * k * 2 + k * tn * 2

    tn_min = min(n, max(mult, SLAB))
    tm = _divisor(m, 1408, BF16_ROWS)
    while True:
        tn = _divisor(n, 1024 if m > 256 else 2048, mult)
        while est(tm, tn) > budget and tn > tn_min:
            tn = _divisor(n, tn - mult, mult)
        if est(tm, tn) <= budget or tm <= BF16_ROWS:
            return tm, tn, est(tm, tn)
        tm = _divisor(m, tm - BF16_ROWS, BF16_ROWS)


def _mm_rope_body(a_ref, w_ref, c_ref, s_ref, o_ref, *, rep, scale):
    a = a_ref[...].astype(BF16)
    acc = jnp.dot(a, w_ref[...].astype(BF16), preferred_element_type=F32)
    c, s = c_ref[...] * scale, s_ref[...] * scale
    for r in range(rep):
        x = acc[:, 2 * r * SLAB:(2 * r + 1) * SLAB]
        xs = acc[:, (2 * r + 1) * SLAB:(2 * r + 2) * SLAB]
        o_ref[:, r * SLAB:(r + 1) * SLAB] = (x * c + xs * s).astype(o_ref.dtype)


def _matmul(a, w, out_dtype, layer=None, slab=None, rope=None, name="matmul"):
    m, k = a.shape
    n = w.shape[-1]
    mult = 2 * SLAB if rope is not None else SLAB if slab is not None else LANES
    tm, tn, est = _mm_tiles(m, k, n, a.dtype.itemsize, w.dtype.itemsize, jnp.dtype(out_dtype).itemsize, mult, 46 << 20)
    n_out, tn_out = (n // 2, tn // 2) if rope is not None else (n, tn)
    if w.ndim == 3:
        w_spec = pl.BlockSpec((None, k, tn), lambda i, j: (layer, 0, j))
    else:
        w_spec = pl.BlockSpec((k, tn), lambda i, j: (0, j))
    in_specs = [pl.BlockSpec((tm, k), lambda i, j: (i, 0)), w_spec]
    args = [a, w]
    body = _mm_body
    if slab is not None:
        in_specs.append(pl.BlockSpec((tm, SLAB), lambda i, j: (i, 0)))
        args.append(slab)
        body = functools.partial(_mm_slab_body, rep=tn // SLAB)
    if rope is not None:
        tables, scale = rope
        period = tables[0].shape[0]
        if period == 1:
            tspec = pl.BlockSpec((1, SLAB), lambda i, j: (0, 0))
        else:
            assert period % tm == 0, (period, tm)
            nt = period // tm
            tspec = pl.BlockSpec((tm, SLAB), lambda i, j: (i % nt, 0))
        c, s1, s2 = tables
        in_specs += [tspec] * 2
        args += [c, s1 + s2]
        body = functools.partial(_mm_rope_body, rep=tn_out // SLAB, scale=scale)
        est += 4 * tm * SLAB * 4
    return pl.pallas_call(
        body,
        out_shape=jax.ShapeDtypeStruct((m, n_out), out_dtype),
        grid=(m // tm, n // tn),
        in_specs=in_specs,
        out_specs=pl.BlockSpec((tm, tn_out), lambda i, j: (i, j)),
        compiler_params=_cparams(("parallel", "parallel"), est),
        name=name,
    )(*args)


def _mm_resid_body(a_ref, w_ref, h_ref, gp_ref, gn_ref, ho_ref, xo_ref, acc_ref, *, nk):
    kk = pl.program_id(1)

    def prod():
        return jnp.dot(a_ref[...].astype(BF16), w_ref[...].astype(BF16), preferred_element_type=F32)

    @pl.when(kk == 0)
    def _():
        acc_ref[...] = prod()

    @pl.when(kk > 0)
    def _():
        acc_ref[...] += prod()

    @pl.when(kk == nk - 1)
    def _():
        h = h_ref[...] + _rms(acc_ref[...]) * gp_ref[...]
        ho_ref[...] = h
        xo_ref[...] = (_rms(h) * gn_ref[...]).astype(xo_ref.dtype)


def _matmul_resid(a, w, layer, h, g_post, g_next, name="matmul_resid"):
    m, k = a.shape
    n = w.shape[-1]
    ab, wb = a.dtype.itemsize, w.dtype.itemsize
    tk = _divisor(k, (6 << 20) // (n * wb), LANES)
    nk = k // tk

    def est(tm):
        casts = (tk * n * 2 if wb != 2 else 0) + (tm * tk * 2 if ab != 2 else 0)
        return tm * n * 4 + 2 * (tm * tk * ab + tk * n * wb + 2 * tm * n * 4 + tm * n * 2) + casts

    tm = _divisor(m, 1408, BF16_ROWS)
    while est(tm) > (50 << 20) and tm > BF16_ROWS:
        tm = _divisor(m, tm - BF16_ROWS, BF16_ROWS)
    row = pl.BlockSpec((tm, n), lambda i, kk: (i, 0))
    vec = pl.BlockSpec((1, n), lambda i, kk: (0, 0))
    return pl.pallas_call(
        functools.partial(_mm_resid_body, nk=nk),
        out_shape=(jax.ShapeDtypeStruct((m, n), F32), jax.ShapeDtypeStruct((m, n), BF16)),
        grid=(m // tm, nk),
        in_specs=[
            pl.BlockSpec((tm, tk), lambda i, kk: (i, kk)),
            pl.BlockSpec((None, tk, n), lambda i, kk: (layer, kk, 0)),
            row, vec, vec,
        ],
        out_specs=(row, row),
        scratch_shapes=[pltpu.VMEM((tm, n), F32)],
        compiler_params=_cparams(("parallel", "arbitrary"), est(tm)),
        name=name,
    )(a, w, h, g_post.reshape(1, n), g_next.reshape(1, n))


def _rope_tables(pos, d):
    inv = ROPE_BASE ** (-jnp.arange(0, d, 2, dtype=F32) / d)
    ang = pos.astype(F32)[:, None] * inv[None, :]
    return jnp.cos(ang), jnp.sin(ang)


def _slab_rope_tables(pos):
    cos, sin = _rope_tables(pos, MLA_ROPE)
    n, half = cos.shape
    one = jnp.ones((n, MLA_NOPE), F32)
    z_nope = jnp.zeros((n, MLA_NOPE), F32)
    z_half = jnp.zeros((n, half), F32)
    z_tail = jnp.zeros((n, SLAB - MLA_NOPE - MLA_ROPE), F32)
    c = jnp.concatenate([one, cos, cos, z_tail], axis=1)
    s1 = jnp.concatenate([z_nope, -sin, z_half, z_tail], axis=1)
    s2 = jnp.concatenate([z_nope, z_half, sin, z_tail], axis=1)
    return c, s1, s2


def _swapped_slabs(w):
    half = MLA_ROPE // 2
    nope, x1, x2 = w[..., :MLA_NOPE], w[..., MLA_NOPE:MLA_NOPE + half], w[..., MLA_NOPE + half:]
    z_tail = jnp.zeros(w.shape[:-1] + (SLAB - MLA_NOPE - MLA_ROPE,), w.dtype)
    return jnp.concatenate([nope, x1, x2, z_tail, jnp.zeros_like(nope), x2, x1, z_tail], axis=-1)


def _slab_rope(x, c, s1, s2):
    half = MLA_ROPE // 2
    return x * c + pltpu.roll(x, SLAB - half, 1) * s1 + pltpu.roll(x, half, 1) * s2


def _ret_rope(x, cos, sin):
    half = RET_DK // 2
    x1, x2 = x[:, :half], x[:, half:]
    return jnp.concatenate([x1 * cos - x2 * sin, x1 * sin + x2 * cos], axis=1)


def _ret_log_gamma():
    return jnp.log1p(-jnp.power(2.0, -5.0 - jnp.arange(RET_HEADS, dtype=F32)))


def _ret_prompt_body(lg_ref, q_ref, k_ref, v_ref, g_ref, cos_ref, sin_ref, o_ref, so_ref, st_ref, *, pad, n_chunks, hp):
    hg = pl.program_id(1)
    c = pl.program_id(2)
    chunk = q_ref.shape[0]

    @pl.when(c == 0)
    def _():
        st_ref[...] = jnp.zeros_like(st_ref)

    lead = jnp.where(c == 0, float(pad), 0.0)
    cos, sin = cos_ref[...], sin_ref[...]
    n_col = lax.broadcasted_iota(jnp.int32, (chunk, 1), 0).astype(F32)
    n_row = lax.broadcasted_iota(jnp.int32, (1, chunk), 1).astype(F32)
    diff = n_col - n_row
    for i in range(hp):
        lg = lg_ref[hg * hp + i]
        q = _ret_rope(q_ref[:, i * RET_DK:(i + 1) * RET_DK].astype(F32), cos, sin)
        k = _ret_rope(k_ref[:, i * RET_DK:(i + 1) * RET_DK].astype(F32), cos, sin) * (RET_DK ** -0.5)
        v = v_ref[:, i * RET_DV:(i + 1) * RET_DV]
        intra = jnp.where(diff >= 0, jnp.exp(lg * jnp.maximum(diff, 0.0)), 0.0)
        q_dec = jnp.exp(lg * (n_col + 1.0 - lead))
        k_dec = jnp.exp(lg * (chunk - 1.0 - n_col))
        s_dec = jnp.exp(jnp.full((1, 1), lg * (chunk - lead), F32))
        st = st_ref[i]
        scores = lax.dot_general(q.astype(BF16), k.astype(BF16), (((1,), (1,)), ((), ())),
                                 preferred_element_type=F32) * intra
        out = jnp.dot(scores.astype(BF16), v, preferred_element_type=F32)
        out = out + jnp.dot((q * q_dec).astype(BF16), st.astype(BF16), preferred_element_type=F32)
        kt = (k * k_dec).T.astype(BF16)
        st_ref[i] = st * s_dec + jnp.dot(kt, v, preferred_element_type=F32)
        g = g_ref[:, i * RET_DV:(i + 1) * RET_DV].astype(F32)
        o_ref[:, i * RET_DV:(i + 1) * RET_DV] = (_silu(g) * _rms(out)).astype(o_ref.dtype)

    @pl.when(c == n_chunks - 1)
    def _():
        so_ref[...] = st_ref[...]


def _ret_prompt(qkvg, cos, sin, pad):
    b, s, _ = qkvg.shape
    chunk = RET_CHUNK
    nc = s // chunk
    hp = _divisor(RET_HEADS, 8, 1)
    ng = RET_HEADS // hp
    body = functools.partial(_ret_prompt_body, pad=pad, n_chunks=nc, hp=hp)
    return pl.pallas_call(
        body,
        out_shape=(jax.ShapeDtypeStruct((b, s, RET_HEADS * RET_DV), BF16),
                   jax.ShapeDtypeStruct((b, RET_HEADS, RET_DK, RET_DV), F32)),
        grid=(b, ng, nc),
        in_specs=[
            pl.BlockSpec(memory_space=pltpu.SMEM),
            pl.BlockSpec((None, chunk, hp * RET_DK), lambda bi, h, c: (bi, c, h)),
            pl.BlockSpec((None, chunk, hp * RET_DK), lambda bi, h, c: (bi, c, ng + h)),
            pl.BlockSpec((None, chunk, hp * RET_DV), lambda bi, h, c: (bi, c, ng + h)),
            pl.BlockSpec((None, chunk, hp * RET_DV), lambda bi, h, c: (bi, c, 2 * ng + h)),
            pl.BlockSpec((chunk, RET_DK // 2), lambda bi, h, c: (c, 0)),
            pl.BlockSpec((chunk, RET_DK // 2), lambda bi, h, c: (c, 0)),
        ],
        out_specs=(
            pl.BlockSpec((None, chunk, hp * RET_DV), lambda bi, h, c: (bi, c, h)),
            pl.BlockSpec((None, hp, RET_DK, RET_DV), lambda bi, h, c: (bi, h, 0, 0)),
        ),
        scratch_shapes=[pltpu.VMEM((hp, RET_DK, RET_DV), F32)],
        compiler_params=_cparams(("parallel", "parallel", "arbitrary"), 16 << 20),
        name="ret_prompt",
    )(_ret_log_gamma(), qkvg, qkvg, qkvg, qkvg, cos, sin)


def _ret_sample_body(lg_ref, q_ref, k_ref, v_ref, g_ref, cos_ref, sin_ref, st_ref, o_ref, so_ref):
    h = pl.program_id(1)
    bt = q_ref.shape[0]
    gamma = jnp.exp(jnp.full((1, 1), lg_ref[h], F32))
    cos, sin = cos_ref[...], sin_ref[...]
    q = _ret_rope(q_ref[...], cos, sin)
    k = _ret_rope(k_ref[...], cos, sin) * (RET_DK ** -0.5)
    v = v_ref[...]
    qk = jnp.sum(q * k, axis=-1, keepdims=True)
    qg = (q * gamma).astype(BF16)
    eye = lax.broadcasted_iota(jnp.int32, (RET_DK, RET_DK), 0) == lax.broadcasted_iota(jnp.int32, (RET_DK, RET_DK), 1)
    rows = []
    for i in range(bt):
        st = st_ref[i]
        cross = jnp.dot(qg, st.astype(BF16), preferred_element_type=F32)
        rows.append(cross[i:i + 1])
        k_col = jnp.sum(jnp.where(eye, k[i:i + 1], 0.0), axis=1, keepdims=True)
        so_ref[i] = st * gamma + k_col * v[i:i + 1]
    out = qk * v + jnp.concatenate(rows, axis=0)
    o_ref[...] = _silu(g_ref[...]) * _rms(out)


def _ret_sample(qkvg, state, cos, sin, out_buf, layer):
    db = qkvg.shape[0]
    bt = SUBLANES
    kq = RET_HEADS
    n_layers = state.shape[0]
    st_block = (None, bt, None, RET_DK, RET_DV)
    in_specs = [
        pl.BlockSpec(memory_space=pltpu.SMEM),
        pl.BlockSpec((bt, RET_DK), lambda bi, h: (bi, h)),
        pl.BlockSpec((bt, RET_DK), lambda bi, h: (bi, kq + h)),
        pl.BlockSpec((bt, RET_DV), lambda bi, h: (bi, kq + h)),
        pl.BlockSpec((bt, RET_DV), lambda bi, h: (bi, 2 * kq + h)),
        pl.BlockSpec((1, RET_DK // 2), lambda bi, h: (0, 0)),
        pl.BlockSpec((1, RET_DK // 2), lambda bi, h: (0, 0)),
        pl.BlockSpec(st_block, lambda bi, h: (layer, bi, h, 0, 0)),
    ]
    args = [_ret_log_gamma(), qkvg, qkvg, qkvg, qkvg, cos, sin, state]
    aliases = {}
    body = _ret_sample_body
    if out_buf is not None:
        in_specs.append(pl.BlockSpec(memory_space=pl.ANY))
        args.append(out_buf)
        aliases = {len(args) - 1: 1}
        body = lambda *refs: _ret_sample_body(*refs[:8], *refs[9:])
    return pl.pallas_call(
        body,
        out_shape=(jax.ShapeDtypeStruct((db, RET_HEADS * RET_DV), F32),
                   jax.ShapeDtypeStruct((n_layers, db, RET_HEADS, RET_DK, RET_DV), F32)),
        grid=(db // bt, RET_HEADS),
        in_specs=in_specs,
        out_specs=(
            pl.BlockSpec((bt, RET_DV), lambda bi, h: (bi, h)),
            pl.BlockSpec(st_block, lambda bi, h: (layer, bi, h, 0, 0)),
        ),
        input_output_aliases=aliases,
        compiler_params=_cparams(("parallel", "parallel"), 4 * bt * RET_DK * RET_DV * 4 + (4 << 20)),
        name="ret_sample",
    )(*args)


def _conv_act(g2, g1, g0, up, cw_ref, cb_ref):
    conv = g2 * cw_ref[0:1, :] + g1 * cw_ref[1:2, :] + g0 * cw_ref[2:3, :] + cb_ref[...]
    return _silu(conv) * up


def _ffn1_seq_body(x_ref, wg_ref, wu_ref, cw_ref, cb_ref, act_ref, cs_ref, carry_ref):
    i = pl.program_id(1)
    j = pl.program_id(2)
    x = x_ref[...]
    gate = jnp.dot(x, wg_ref[...].astype(BF16), preferred_element_type=F32)
    up = jnp.dot(x, wu_ref[...].astype(BF16), preferred_element_type=F32)
    tm = gate.shape[0]
    head = 2 * SUBLANES
    act = _conv_act(pltpu.roll(gate, 2, 0), pltpu.roll(gate, 1, 0), gate, up, cw_ref, cb_ref)
    act_ref[head:, :] = act[head:].astype(act_ref.dtype)
    @pl.when(i == 0)
    def _():
        carry_ref[j] = jnp.zeros(carry_ref.shape[1:], F32)

    win = jnp.concatenate([carry_ref[j], gate[:head]], axis=0)
    w1 = pltpu.roll(win, 1, 0)[SUBLANES:]
    w2 = pltpu.roll(win, 2, 0)[SUBLANES:]
    act_ref[:head, :] = _conv_act(w2, w1, gate[:head], up[:head], cw_ref, cb_ref).astype(act_ref.dtype)
    tail = gate[tm - SUBLANES:]
    carry_ref[j] = tail
    cs_ref[...] = tail


def _ffn1_seq(x, w_gate, w_up, conv_w, conv_b, layer, n_seq):
    m, k = x.shape
    f = w_gate.shape[-1]
    s = m // n_seq
    tm = _divisor(s, 1408, BF16_ROWS)
    tn = _divisor(f, 512, LANES)
    ni, nj = s // tm, f // tn
    est = 2 * (tm * k * 2 + 2 * k * tn * 4 + tm * tn * 2) + 2 * k * tn * 2 + 4 * tm * tn * 4
    act, tails = pl.pallas_call(
        _ffn1_seq_body,
        out_shape=(jax.ShapeDtypeStruct((m, f), BF16), jax.ShapeDtypeStruct((n_seq, ni, SUBLANES, f), F32)),
        grid=(n_seq, ni, nj),
        in_specs=[
            pl.BlockSpec((tm, k), lambda b, i, j: (b * ni + i, 0)),
            pl.BlockSpec((None, k, tn), lambda b, i, j: (layer, 0, j)),
            pl.BlockSpec((None, k, tn), lambda b, i, j: (layer, 0, j)),
            pl.BlockSpec((None, CONV_W, tn), lambda b, i, j: (layer, 0, j)),
            pl.BlockSpec((None, 1, tn), lambda b, i, j: (layer, 0, j)),
        ],
        out_specs=(
            pl.BlockSpec((tm, tn), lambda b, i, j: (b * ni + i, j)),
            pl.BlockSpec((None, None, SUBLANES, tn), lambda b, i, j: (b, i, 0, j)),
        ),
        scratch_shapes=[pltpu.VMEM((nj, SUBLANES, tn), F32)],
        compiler_params=_cparams(("arbitrary", "arbitrary", "arbitrary"), est),
        name="ffn1_seq",
    )(x, w_gate, w_up, conv_w, conv_b.reshape(conv_b.shape[0], 1, f))
    return act, tails[:, ni - 1]


def _ffn1_tok_body(x_ref, wg_ref, wu_ref, cw_ref, cb_ref, s0_ref, s1_ref, act_ref, gate_ref):
    x = x_ref[...]
    gate = jnp.dot(x, wg_ref[...].astype(BF16), preferred_element_type=F32)
    up = jnp.dot(x, wu_ref[...].astype(BF16), preferred_element_type=F32)
    act_ref[...] = _conv_act(s0_ref[...], s1_ref[...], gate, up, cw_ref, cb_ref).astype(act_ref.dtype)
    gate_ref[...] = gate


def _ffn1_tok(x, w_gate, w_up, conv_w, conv_b, layer, conv_state):
    m, k = x.shape
    f = w_gate.shape[-1]
    tn = _divisor(f, 1024, LANES)
    nj = f // tn
    cs = conv_state.reshape(m, (CONV_W - 1) * f)
    est = 2 * (m * k * 2 + 2 * k * tn * 4 + 6 * m * tn * 4) + 2 * k * tn * 2
    return pl.pallas_call(
        _ffn1_tok_body,
        out_shape=(jax.ShapeDtypeStruct((m, f), BF16), jax.ShapeDtypeStruct((m, f), F32)),
        grid=(nj,),
        in_specs=[
            pl.BlockSpec((m, k), lambda j: (0, 0)),
            pl.BlockSpec((None, k, tn), lambda j: (layer, 0, j)),
            pl.BlockSpec((None, k, tn), lambda j: (layer, 0, j)),
            pl.BlockSpec((None, CONV_W, tn), lambda j: (layer, 0, j)),
            pl.BlockSpec((None, 1, tn), lambda j: (layer, 0, j)),
            pl.BlockSpec((m, tn), lambda j: (0, j)),
            pl.BlockSpec((m, tn), lambda j: (0, nj + j)),
        ],
        out_specs=(pl.BlockSpec((m, tn), lambda j: (0, j)), pl.BlockSpec((m, tn), lambda j: (0, j))),
        compiler_params=_cparams(("parallel",), est),
        name="ffn1_tok",
    )(x, w_gate, w_up, conv_w, conv_b.reshape(conv_b.shape[0], 1, f), cs, cs)


def _kv_post_body(y_ref, g_ref, c_ref, s1_ref, s2_ref, cf_ref, cb_ref, r_ref):
    y = y_ref[...]
    c = _rms(y[:, :KV_LORA]) * g_ref[...]
    cf_ref[...] = c
    cb_ref[...] = c.astype(cb_ref.dtype)
    r_ref[...] = _slab_rope(y[:, KV_LORA:], c_ref[...], s1_ref[...], s2_ref[...])


def _kv_post(y, g, tables):
    m = y.shape[0]
    t_rows = tables[0].shape[0]
    tm = _divisor(m if t_rows == 1 else t_rows, 512, BF16_ROWS)
    if t_rows == 1:
        tspec = pl.BlockSpec((1, SLAB), lambda i: (0, 0))
    else:
        nt = t_rows // tm
        tspec = pl.BlockSpec((tm, SLAB), lambda i: (i % nt, 0))
    return pl.pallas_call(
        _kv_post_body,
        out_shape=(jax.ShapeDtypeStruct((m, KV_LORA), F32), jax.ShapeDtypeStruct((m, KV_LORA), BF16),
                   jax.ShapeDtypeStruct((m, SLAB), F32)),
        grid=(m // tm,),
        in_specs=[pl.BlockSpec((tm, KV_LORA + SLAB), lambda i: (i, 0)), pl.BlockSpec((1, KV_LORA), lambda i: (0, 0)),
                  tspec, tspec, tspec],
        out_specs=(pl.BlockSpec((tm, KV_LORA), lambda i: (i, 0)), pl.BlockSpec((tm, KV_LORA), lambda i: (i, 0)),
                   pl.BlockSpec((tm, SLAB), lambda i: (i, 0))),
        compiler_params=_cparams(("parallel",), 16 << 20),
        name="kv_post",
    )(y, g.reshape(1, KV_LORA), *tables)


ATTN_BLOCK = 3 * LANES


def _v_up_t_body(c_ref, w_ref, o_ref):
    o_ref[...] = lax.dot_general(w_ref[...].astype(BF16), c_ref[...], (((1,), (1,)), ((), ())),
                                 preferred_element_type=F32).astype(o_ref.dtype)


def _v_up_t(c, w_uv_t, n_seq, tk):
    m, kl = c.shape
    n = w_uv_t.shape[0]
    nk = m // n_seq // tk
    tn = _divisor(n, 1024, LANES)
    return pl.pallas_call(
        _v_up_t_body,
        out_shape=jax.ShapeDtypeStruct((n_seq, nk, n, tk), BF16),
        grid=(n_seq, nk, n // tn),
        in_specs=[pl.BlockSpec((tk, kl), lambda b, kb, j: (b * nk + kb, 0)), pl.BlockSpec((tn, kl), lambda b, kb, j: (j, 0))],
        out_specs=pl.BlockSpec((None, None, tn, tk), lambda b, kb, j: (b, kb, j, 0)),
        compiler_params=_cparams(("parallel", "parallel", "parallel"), 16 << 20),
        name="v_up_t",
    )(c, w_uv_t)


def _attn_prompt_body(q_ref, k_ref, vt_ref, bias_ref, o_ref, m_ref, l_ref, acc_ref, s_ref, *, pad, blk, hp):
    qi = pl.program_id(2)
    m_ref[...] = jnp.full_like(m_ref, NEG)
    l_ref[...] = jnp.zeros_like(l_ref)
    acc_ref[...] = jnp.zeros_like(acc_ref)
    nt = (((1,), (1,)), ((), ()))

    def scores(ki, slot):
        start = ki * blk if isinstance(ki, int) else pl.multiple_of(ki * blk, blk)
        for i in range(hp):
            kb = k_ref[pl.ds(start, blk), i * SLAB:(i + 1) * SLAB]
            s_ref[slot, i] = lax.dot_general(kb, q_ref[:, i * SLAB:(i + 1) * SLAB], nt, preferred_element_type=F32)

    def consume(ki, slot, kind):
        for i in range(hp):
            s = s_ref[slot, i]
            if kind == "first":
                s = s[pad:]
            elif kind == "diag":
                s = s + bias_ref[jnp.where(qi == 0, 0, 1)]
            m_old = m_ref[i]
            m_new = jnp.maximum(m_old, jnp.max(s, axis=0, keepdims=True))
            alpha = jnp.exp(m_old - m_new)
            p = jnp.exp(s - m_new)
            l_ref[i] = alpha * l_ref[i] + jnp.sum(p, axis=0, keepdims=True)
            pb = p.astype(BF16)
            if kind == "first":
                pb = jnp.concatenate([jnp.zeros((pad, blk), BF16), pb], axis=0)
            vt = vt_ref[ki, i * MLA_V:(i + 1) * MLA_V, :]
            acc_ref[i] = alpha * acc_ref[i] + jnp.dot(vt, pb, preferred_element_type=F32)
            m_ref[i] = m_new

    scores(0, 0)

    @pl.when(qi > 0)
    def _():
        scores(1, 1)
        consume(0, 0, "first")

    def mid(ki, carry):
        slot = lax.rem(ki, 2)
        consume(ki, slot, "full")
        scores(ki + 1, 1 - slot)
        return carry

    lax.fori_loop(1, qi, mid, 0)
    consume(qi, lax.rem(qi, 2), "diag")

    for i in range(hp):
        o_ref[:, i * MLA_V:(i + 1) * MLA_V] = (acc_ref[i] / l_ref[i]).T.astype(o_ref.dtype)


def _attn_bias(blk, pad):
    kpos = jnp.arange(blk)[:, None]
    qpos = jnp.arange(blk)[None, :]
    causal = kpos <= qpos
    first = causal & ((kpos >= pad) | (qpos < pad))
    return jnp.stack([jnp.where(first, 0.0, NEG), jnp.where(causal, 0.0, NEG)]).astype(F32)


def _attn_prompt(q, k, vt, pad):
    b, s, _ = q.shape
    nk, blk = vt.shape[1], vt.shape[3]
    hp = _divisor(MLA_HEADS, 4, 1)
    body = functools.partial(_attn_prompt_body, pad=pad, blk=blk, hp=hp)
    est = (2 * (blk * hp * SLAB * 2 + s * hp * SLAB * 2 + s * hp * MLA_V * 2 + 2 * blk * blk * 4 + blk * hp * MLA_V * 2)
           + 8 * hp * blk * blk * 4)
    return pl.pallas_call(
        body,
        out_shape=jax.ShapeDtypeStruct((b, s, MLA_HEADS * MLA_V), BF16),
        grid=(b, MLA_HEADS // hp, s // blk),
        in_specs=[
            pl.BlockSpec((None, blk, hp * SLAB), lambda bi, h, qi: (bi, qi, h)),
            pl.BlockSpec((None, s, hp * SLAB), lambda bi, h, qi: (bi, 0, h)),
            pl.BlockSpec((None, nk, hp * MLA_V, blk), lambda bi, h, qi: (bi, 0, h, 0)),
            pl.BlockSpec((2, blk, blk), lambda bi, h, qi: (0, 0, 0)),
        ],
        out_specs=pl.BlockSpec((None, blk, hp * MLA_V), lambda bi, h, qi: (bi, qi, h)),
        scratch_shapes=[pltpu.VMEM((hp, 1, blk), F32), pltpu.VMEM((hp, 1, blk), F32), pltpu.VMEM((hp, MLA_V, blk), F32),
                        pltpu.VMEM((2, hp, blk, blk), F32)],
        compiler_params=_cparams(("parallel", "parallel", "arbitrary"), est),
        name="attn_prompt",
    )(q, k, vt, _attn_bias(blk, pad))


def _q_absorb_body(q_ref, w_ref, ql_ref):
    ql_ref[...] = lax.dot_general(q_ref[...][:, :MLA_NOPE].astype(BF16), w_ref[...].astype(BF16), (((1,), (1,)), ((), ())),
                                  preferred_element_type=F32)


def _q_absorb(q, w_uk2):
    db = q.shape[0]
    return pl.pallas_call(
        _q_absorb_body,
        out_shape=jax.ShapeDtypeStruct((db, MLA_HEADS * KV_LORA), F32),
        grid=(MLA_HEADS,),
        in_specs=[pl.BlockSpec((db, SLAB), lambda h: (0, h)), pl.BlockSpec((KV_LORA, MLA_NOPE), lambda h: (0, h))],
        out_specs=pl.BlockSpec((db, KV_LORA), lambda h: (0, h)),
        compiler_params=_cparams(("parallel",), 8 << 20),
        name="q_absorb",
    )(q, w_uk2)


def _attn_sample_body(pt_ref, ql_ref, qs_ref, cn_ref, rn_ref, *refs, pages, n_groups, nb):
    n_page_refs = nb * pages
    lat_refs = refs[:n_page_refs]
    rope_refs = refs[n_page_refs:2 * n_page_refs]
    o_ref, m_ref, l_ref, acc_ref, latb_ref, ropeb_ref = refs[2 * n_page_refs:]
    g = pl.program_id(1)
    page = lat_refs[0].shape[0]

    @pl.when(g == 0)
    def _():
        m_ref[...] = jnp.full_like(m_ref, NEG)
        l_ref[...] = jnp.zeros_like(l_ref)
        acc_ref[...] = jnp.zeros_like(acc_ref)

    nt = (((1,), (1,)), ((), ()))
    for b in range(nb):
        for i in range(pages):
            latb_ref[b, i * page:(i + 1) * page, :] = lat_refs[b * pages + i][...].astype(BF16)
            ropeb_ref[b, :, i * page:(i + 1) * page] = rope_refs[b * pages + i][...].astype(BF16)
        ql = ql_ref[b]
        qr = qs_ref[b][:, MLA_NOPE:MLA_NOPE + MLA_ROPE]
        s = lax.dot_general(ql.astype(BF16), latb_ref[b], nt, preferred_element_type=F32)
        s = s + jnp.dot(qr.astype(BF16), ropeb_ref[b], preferred_element_type=F32)
        m_old = m_ref[b]
        m_new = jnp.maximum(m_old, jnp.max(s, axis=-1, keepdims=True))
        alpha = jnp.exp(m_old - m_new)
        p = jnp.exp(s - m_new)
        l_ref[b] = alpha * l_ref[b] + jnp.sum(p, axis=-1, keepdims=True)
        acc_ref[b] = alpha * acc_ref[b] + jnp.dot(p.astype(BF16), latb_ref[b], preferred_element_type=F32)
        m_ref[b] = m_new

    @pl.when(g == n_groups - 1)
    def _():
        for b in range(nb):
            ql = ql_ref[b]
            qr = qs_ref[b][:, MLA_NOPE:MLA_NOPE + MLA_ROPE]
            cn = cn_ref[b]
            rn = rn_ref[b][:, MLA_NOPE:MLA_NOPE + MLA_ROPE]
            s_new = jnp.sum(ql * cn, axis=-1, keepdims=True) + jnp.sum(qr * rn, axis=-1, keepdims=True)
            m_old = m_ref[b]
            m_new = jnp.maximum(m_old, s_new)
            alpha = jnp.exp(m_old - m_new)
            p_new = jnp.exp(s_new - m_new)
            l = alpha * l_ref[b] + p_new
            o_ref[b] = (alpha * acc_ref[b] + p_new * cn) / l


def _attn_sample(page_table, cache_lat, cache_rope_t, q_lat, q_slab, c_new, r_new):
    db, n_pages = page_table.shape
    page = cache_lat.shape[1]
    pages = _divisor(n_pages, 16, 1)
    nb = _divisor(db, 4, 1)
    n_groups = n_pages // pages
    body = functools.partial(_attn_sample_body, pages=pages, n_groups=n_groups, nb=nb)

    page_table = page_table.reshape(db // nb, nb, n_groups, pages).transpose(0, 2, 1, 3).reshape(db // nb, n_groups, nb * pages)

    def page_spec(shape, b, i):
        return pl.BlockSpec((None,) + shape, lambda bi, g, pt: (pt[bi, g, b * pages + i], 0, 0))

    def seq_spec(rows, width):
        return pl.BlockSpec((nb, rows, width), lambda bi, g, pt: (bi, 0, 0))

    in_specs = [seq_spec(MLA_HEADS, KV_LORA), seq_spec(MLA_HEADS, SLAB), seq_spec(1, KV_LORA), seq_spec(1, SLAB)]
    in_specs += [page_spec((page, KV_LORA), b, i) for b in range(nb) for i in range(pages)]
    in_specs += [page_spec((MLA_ROPE, page), b, i) for b in range(nb) for i in range(pages)]
    keys = pages * page
    est = nb * (2 * keys * (KV_LORA + MLA_ROPE) * 4 + 2 * keys * (KV_LORA + MLA_ROPE) * 2) + (4 << 20)
    return pl.pallas_call(
        body,
        out_shape=jax.ShapeDtypeStruct((db, MLA_HEADS, KV_LORA), F32),
        grid_spec=pltpu.PrefetchScalarGridSpec(
            num_scalar_prefetch=1,
            grid=(db // nb, n_groups),
            in_specs=in_specs,
            out_specs=seq_spec(MLA_HEADS, KV_LORA),
            scratch_shapes=[
                pltpu.VMEM((nb, MLA_HEADS, 1), F32), pltpu.VMEM((nb, MLA_HEADS, 1), F32),
                pltpu.VMEM((nb, MLA_HEADS, KV_LORA), F32),
                pltpu.VMEM((nb, keys, KV_LORA), BF16), pltpu.VMEM((nb, MLA_ROPE, keys), BF16),
            ],
        ),
        compiler_params=_cparams(("parallel", "arbitrary"), est),
        name="attn_sample",
    )(page_table, q_lat, q_slab, c_new, r_new, *([cache_lat] * (nb * pages)), *([cache_rope_t] * (nb * pages)))


def _head_out_body(o_ref, w_ref, y_ref):
    y_ref[...] = jnp.dot(o_ref[...].astype(BF16), w_ref[...].astype(BF16), preferred_element_type=F32).astype(y_ref.dtype)


def _head_out(o_lat, w_uv2):
    db = o_lat.shape[0]
    return pl.pallas_call(
        _head_out_body,
        out_shape=jax.ShapeDtypeStruct((db, MLA_HEADS * MLA_V), BF16),
        grid=(MLA_HEADS,),
        in_specs=[pl.BlockSpec((db, KV_LORA), lambda h: (0, h)), pl.BlockSpec((KV_LORA, MLA_V), lambda h: (0, h))],
        out_specs=pl.BlockSpec((db, MLA_V), lambda h: (0, h)),
        compiler_params=_cparams(("parallel",), 4 << 20),
        name="head_out",
    )(o_lat, w_uv2)


def _prep_weights(ret_w_qkvg, ret_w_o, ffn_w_gate, ffn_w_up, ffn_w_down, w_dkv, w_kr, w_uk, w_uv, mla_w_dq, mla_w_uq,
                  mla_w_o):
    tail = SLAB - MLA_NOPE - MLA_ROPE
    w_kv = jnp.concatenate([w_dkv, jnp.zeros((D_MODEL, MLA_NOPE), F32), w_kr, jnp.zeros((D_MODEL, tail), F32)], axis=1)
    w_uk_ext = jnp.pad(w_uk, ((0, 0), (0, 0), (0, SLAB - MLA_NOPE))).reshape(KV_LORA, MLA_HEADS * SLAB)
    n_b = mla_w_uq.shape[0]
    w_uq_ext = _swapped_slabs(mla_w_uq.reshape(n_b, Q_LORA, MLA_HEADS, MLA_NOPE + MLA_ROPE)).reshape(n_b, Q_LORA, -1)
    w_uv2 = w_uv.reshape(KV_LORA, MLA_HEADS * MLA_V)
    return dict(
        qkvg=ret_w_qkvg, ret_o=ret_w_o.astype(BF16), gate=ffn_w_gate, up=ffn_w_up, down=ffn_w_down.astype(BF16), kv=w_kv,
        uk_ext=w_uk_ext, uk2=w_uk.reshape(KV_LORA, MLA_HEADS * MLA_NOPE), uv2=w_uv2, uv_t=w_uv2.T, dq=mla_w_dq,
        uq_ext=w_uq_ext, mla_o=mla_w_o.astype(BF16),
    )


def _trunk(h, xn, w, norm_g, ffn_conv_w, ffn_conv_b, kv_in_g, kv_norm_g, mla_q_norm_g, ret_mixer, ffn1, kv_tables, attend):
    conv_states = []
    c_f32 = r_slab = kv_ctx = None
    q_scale = (MLA_NOPE + MLA_ROPE) ** -0.5
    for layer in range(DEPTH):
        g = norm_g[layer]
        if layer < N_A_LAYERS:
            qkvg = _matmul(xn, w["qkvg"], ret_mixer.qkvg_dtype, layer=layer, name="qkvg")
            gated = ret_mixer(layer, qkvg)
            h, xn = _matmul_resid(gated, w["ret_o"], layer, h, g[1], g[2], name="ret_o")
        else:
            j = layer - N_A_LAYERS
            cq = _matmul(xn, w["dq"], F32, layer=j, name="dq")
            cqn = _rms_cast(cq, mla_q_norm_g[j])
            q = _matmul(cqn, w["uq_ext"], attend.q_dtype, layer=j, rope=(kv_tables, q_scale), name="uq")
            o = attend(q, kv_ctx)
            h, xn = _matmul_resid(o, w["mla_o"], j, h, g[1], g[2], name="mla_o")
        act, cs = ffn1(layer, xn, w["gate"], w["up"], ffn_conv_w, ffn_conv_b)
        conv_states.append(cs)
        g_next = norm_g[layer + 1, 0] if layer + 1 < DEPTH else g[3]
        h, xn = _matmul_resid(act, w["down"], layer, h, g[3], g_next, name="down")
        if layer == N_A_LAYERS - 1:
            hn = _rms_cast(h, kv_in_g)
            y = _matmul(hn, w["kv"], F32, name="kv_down")
            c_f32, c_bf16, r_slab = _kv_post(y, kv_norm_g, kv_tables)
            kv_ctx = (c_f32, c_bf16, r_slab)
    return h, conv_states, c_f32, r_slab


class _PromptRetention:
    qkvg_dtype = BF16

    def __init__(self, n_seq, pad, cos, sin):
        self.n_seq, self.pad, self.cos, self.sin = n_seq, pad, cos, sin
        self.states = []

    def __call__(self, layer, qkvg):
        m, n = qkvg.shape
        gated, state = _ret_prompt(qkvg.reshape(self.n_seq, m // self.n_seq, n), self.cos, self.sin, self.pad)
        self.states.append(state)
        return gated.reshape(m, -1)


class _SampleRetention:
    qkvg_dtype = F32

    def __init__(self, state, cos, sin):
        self.state, self.cos, self.sin = state, cos, sin
        self.out = None

    def __call__(self, layer, qkvg):
        gated, self.out = _ret_sample(qkvg, self.state, self.cos, self.sin, self.out, layer)
        return gated


def kernel(x_prompt, x_sample, state_retention, state_conv, cache_kv_latent, cache_k_rope, page_table, meta_tokens, norm_g,
           ret_w_qkvg, ret_w_o, ffn_w_gate, ffn_w_up, ffn_w_down, ffn_conv_w, ffn_conv_b, kv_in_g, w_dkv, kv_norm_g, w_kr,
           w_uk, w_uv, mla_w_dq, mla_q_norm_g, mla_w_uq, mla_w_o):
    w = _prep_weights(ret_w_qkvg, ret_w_o, ffn_w_gate, ffn_w_up, ffn_w_down, w_dkv, w_kr, w_uk, w_uv, mla_w_dq, mla_w_uq,
                      mla_w_o)
    shared = (norm_g, ffn_conv_w, ffn_conv_b, kv_in_g, kv_norm_g, mla_q_norm_g)

    b, seq, d = x_prompt.shape
    pad = RET_CHUNK - N_META
    s_pad = pad + N_META + seq
    h0, xn0 = _embed_norm(x_prompt, meta_tokens, norm_g[0, 0], pad)
    pos_p = jnp.arange(s_pad) - pad
    cos_p, sin_p = _rope_tables(pos_p, RET_DK)
    tables_p = _slab_rope_tables(pos_p)
    ret_p = _PromptRetention(b, pad, cos_p, sin_p)

    def ffn1_p(layer, xn, wg, wu, cw, cb):
        act, tail = _ffn1_seq(xn, wg, wu, cw, cb, layer, b)
        return act, tail[:, SUBLANES - (CONV_W - 1):, :]

    def attend_p(q, kv_ctx):
        if "k" not in attend_p.cache:
            _, c_bf16, r_slab = kv_ctx
            attend_p.cache["k"] = _matmul(c_bf16, w["uk_ext"], BF16, slab=r_slab, name="k_up").reshape(b, s_pad, -1)
            attend_p.cache["vt"] = _v_up_t(c_bf16, w["uv_t"], b, _divisor(s_pad, ATTN_BLOCK, LANES))
        o = _attn_prompt(q.reshape(b, s_pad, -1), attend_p.cache["k"], attend_p.cache["vt"], pad)
        return o.reshape(b * s_pad, -1)

    attend_p.cache = {}
    attend_p.q_dtype = BF16
    h_p, conv_p, lat_p, rslab_p = _trunk(h0, xn0, w, *shared, ret_p, ffn1_p, tables_p, attend_p)
    y_prompt = h_p.reshape(b, s_pad, d)[:, pad + N_META:]
    lat_prompt = lat_p.reshape(b, s_pad, KV_LORA)[:, pad:]
    rope_prompt = rslab_p.reshape(b, s_pad, SLAB)[:, pad:, MLA_NOPE:MLA_NOPE + MLA_ROPE]

    db = x_sample.shape[0]
    pos_s = jnp.full((1,), PAST_LEN)
    cos_s, sin_s = _rope_tables(pos_s, RET_DK)
    tables_s = _slab_rope_tables(pos_s)
    ret_s = _SampleRetention(state_retention, cos_s, sin_s)
    cache_rope_t = jnp.swapaxes(cache_k_rope, 1, 2)

    def ffn1_s(layer, xn, wg, wu, cw, cb):
        act, gate = _ffn1_tok(xn, wg, wu, cw, cb, layer, state_conv[layer])
        return act, jnp.stack([state_conv[layer][:, 1], gate], axis=1)

    def attend_s(q, kv_ctx):
        c_f32, _, r_slab = kv_ctx
        q_lat = _q_absorb(q, w["uk2"])
        o_lat = _attn_sample(page_table, cache_kv_latent, cache_rope_t, q_lat.reshape(db, MLA_HEADS, KV_LORA),
                             q.reshape(db, MLA_HEADS, SLAB), c_f32.reshape(db, 1, KV_LORA), r_slab.reshape(db, 1, SLAB))
        return _head_out(o_lat.reshape(db, MLA_HEADS * KV_LORA), w["uv2"])

    attend_s.q_dtype = F32

    hs0 = x_sample.reshape(db, d)
    h_s, conv_s, lat_s, rslab_s = _trunk(hs0, _rms_cast(hs0, norm_g[0, 0]), w, *shared, ret_s, ffn1_s, tables_s, attend_s)

    return (
        y_prompt,
        h_s.reshape(db, 1, d),
        jnp.stack(ret_p.states),
        ret_s.out,
        jnp.stack(conv_p),
        jnp.stack(conv_s),
        lat_prompt,
        lat_s.reshape(db, 1, KV_LORA),
        rope_prompt,
        rslab_s[:, MLA_NOPE:MLA_NOPE + MLA_ROPE].reshape(db, 1, MLA_ROPE),
    )
```

```python
import functools

import jax
import jax.numpy as jnp
from jax import lax
from jax.experimental import pallas as pl
from jax.experimental.pallas import tpu as pltpu

D_MODEL = 2048
SEQ = 4096
DEPTH = 4
PAST_LEN = 8192
PAGE_SIZE = 128
N_META = 16
N_A_LAYERS = DEPTH // 2
RET_HEADS = 8
RET_DK = D_MODEL // RET_HEADS
RET_DV = 2 * D_MODEL // RET_HEADS
RET_CHUNK = 128
MLA_HEADS = D_MODEL // 128
MLA_NOPE = 128
MLA_ROPE = 64
MLA_V = 128
Q_LORA = 512
KV_LORA = 512
D_FF = 11 * D_MODEL // 4
CONV_W = 3
ROPE_BASE = 10000.0
EPS = 1e-6

V7X_VMEM_BYTES = 64 * 1024 * 1024
VMEM_REQUEST_CAP = V7X_VMEM_BYTES - 8 * 1024 * 1024
LANES = 128
SUBLANES = 8
BF16_ROWS = 16
SLAB = 2 * LANES
NEG = -1e30

F32 = jnp.float32
BF16 = jnp.bfloat16


def _cparams(semantics, vmem_bytes):
    limit = int(min(max(vmem_bytes + (6 << 20), 16 << 20), VMEM_REQUEST_CAP))
    return pltpu.CompilerParams(dimension_semantics=semantics, vmem_limit_bytes=limit)


def _divisor(n, cap, mult):
    d = (min(n, cap) // mult) * mult
    while d >= mult:
        if n % d == 0:
            return d
        d -= mult
    return n


def _silu(x):
    return x * (1.0 / (1.0 + jnp.exp(-x)))


def _rms(x):
    return x * lax.rsqrt(jnp.mean(x * x, axis=-1, keepdims=True) + EPS)


def _rms_cast_body(x_ref, g_ref, o_ref):
    o_ref[...] = (_rms(x_ref[...]) * g_ref[...]).astype(o_ref.dtype)


def _rms_cast(x, g, out_dtype=BF16):
    m, d = x.shape
    tm = _divisor(m, 512, BF16_ROWS)
    return pl.pallas_call(
        _rms_cast_body,
        out_shape=jax.ShapeDtypeStruct((m, d), out_dtype),
        grid=(m // tm,),
        in_specs=[pl.BlockSpec((tm, d), lambda i: (i, 0)), pl.BlockSpec((1, d), lambda i: (0, 0))],
        out_specs=pl.BlockSpec((tm, d), lambda i: (i, 0)),
        compiler_params=_cparams(("parallel",), 2 * tm * d * 6 + tm * d * 8),
        name="rms_cast",
    )(x, g.reshape(1, d))


def _embed_norm_body(x_ref, meta_ref, g_ref, h_ref, xo_ref, *, pad):
    def emit(rows):
        h_ref[...] = rows
        xo_ref[...] = (_rms(rows) * g_ref[...]).astype(xo_ref.dtype)

    @pl.when(pl.program_id(1) == 0)
    def _():
        emit(jnp.concatenate([jnp.zeros((pad, meta_ref.shape[1]), F32), meta_ref[...]], axis=0))

    @pl.when(pl.program_id(1) > 0)
    def _():
        emit(x_ref[...])


def _embed_norm(x, meta, g, pad):
    b, seq, d = x.shape
    blk = pad + meta.shape[0]
    assert seq % blk == 0, (seq, blk)
    nblk = seq // blk + 1
    out = pl.BlockSpec((blk, d), lambda bi, i: (bi * nblk + i, 0))
    return pl.pallas_call(
        functools.partial(_embed_norm_body, pad=pad),
        out_shape=(jax.ShapeDtypeStruct((b * nblk * blk, d), F32), jax.ShapeDtypeStruct((b * nblk * blk, d), BF16)),
        grid=(b, nblk),
        in_specs=[pl.BlockSpec((None, blk, d), lambda bi, i: (bi, jnp.maximum(i - 1, 0), 0)),
                  pl.BlockSpec(meta.shape, lambda bi, i: (0, 0)), pl.BlockSpec((1, d), lambda bi, i: (0, 0))],
        out_specs=(out, out),
        compiler_params=_cparams(("parallel", "arbitrary"), 16 * blk * d * 4),
        name="embed_norm",
    )(x, meta, g.reshape(1, d))


def _resid_norm_body(h_ref, y_ref, gp_ref, gn_ref, ho_ref, xo_ref):
    h = h_ref[...] + _rms(y_ref[...]) * gp_ref[...]
    ho_ref[...] = h
    xo_ref[...] = (_rms(h) * gn_ref[...]).astype(xo_ref.dtype)


def _resid_norm(h, y, g_post, g_next):
    m, d = h.shape
    tm = _divisor(m, 256, BF16_ROWS)
    row = pl.BlockSpec((tm, d), lambda i: (i, 0))
    vec = pl.BlockSpec((1, d), lambda i: (0, 0))
    return pl.pallas_call(
        _resid_norm_body,
        out_shape=(jax.ShapeDtypeStruct((m, d), F32), jax.ShapeDtypeStruct((m, d), BF16)),
        grid=(m // tm,),
        in_specs=[row, row, vec, vec],
        out_specs=(row, row),
        compiler_params=_cparams(("parallel",), 2 * tm * d * 14 + tm * d * 12),
        name="resid_norm",
    )(h, y, g_post.reshape(1, d), g_next.reshape(1, d))


def _mm_body(a_ref, w_ref, o_ref):
    a = a_ref[...].astype(BF16)
    o_ref[...] = jnp.dot(a, w_ref[...].astype(BF16), preferred_element_type=F32).astype(o_ref.dtype)


def _mm_slab_body(a_ref, w_ref, s_ref, o_ref, *, rep):
    a = a_ref[...].astype(BF16)
    acc = jnp.dot(a, w_ref[...].astype(BF16), preferred_element_type=F32)
    s = s_ref[...]
    o_ref[...] = (acc + jnp.concatenate([s] * rep, axis=1)).astype(o_ref.dtype)


def _mm_tiles(m, k, n, ab, wb, ob, mult, budget):
    def est(tm, tn):
        return 2 * (tm * k * ab + k * tn * wb + tm * tn * ob) + tm * tn * 4 + tm * k * 2 + k * tn * 2

    tn_min = min(n, max(mult, SLAB))
    tm = _divisor(m, 1408, BF16_ROWS)
    while True:
        tn = _divisor(n, 1024 if m > 256 else 2048, mult)
        while est(tm, tn) > budget and tn > tn_min:
            tn = _divisor(n, tn - mult, mult)
        if est(tm, tn) <= budget or tm <= BF16_ROWS:
            return tm, tn, est(tm, tn)
        tm = _divisor(m, tm - BF16_ROWS, BF16_ROWS)


def _mm_rope_body(a_ref, w_ref, c_ref, s_ref, o_ref, *, rep, scale):
    a = a_ref[...].astype(BF16)
    acc = jnp.dot(a, w_ref[...].astype(BF16), preferred_element_type=F32)
    c, s = c_ref[...] * scale, s_ref[...] * scale
    for r in range(rep):
        x = acc[:, 2 * r * SLAB:(2 * r + 1) * SLAB]
        xs = acc[:, (2 * r + 1) * SLAB:(2 * r + 2) * SLAB]
        o_ref[:, r * SLAB:(r + 1) * SLAB] = (x * c + xs * s).astype(o_ref.dtype)


def _matmul(a, w, out_dtype, layer=None, slab=None, rope=None, name="matmul"):
    m, k = a.shape
    n = w.shape[-1]
    mult = 2 * SLAB if rope is not None else SLAB if slab is not None else LANES
    period = rope[0][0].shape[0] if rope is not None else 1
    tm, tn, est = _mm_tiles(m if period == 1 else period, k, n, a.dtype.itemsize, w.dtype.itemsize,
                            jnp.dtype(out_dtype).itemsize, mult, 46 << 20)
    n_out, tn_out = (n // 2, tn // 2) if rope is not None else (n, tn)
    if w.ndim == 3:
        w_spec = pl.BlockSpec((None, k, tn), lambda i, j: (layer, 0, j))
    else:
        w_spec = pl.BlockSpec((k, tn), lambda i, j: (0, j))
    in_specs = [pl.BlockSpec((tm, k), lambda i, j: (i, 0)), w_spec]
    args = [a, w]
    body = _mm_body
    if slab is not None:
        in_specs.append(pl.BlockSpec((tm, SLAB), lambda i, j: (i, 0)))
        args.append(slab)
        body = functools.partial(_mm_slab_body, rep=tn // SLAB)
    if rope is not None:
        tables, scale = rope
        period = tables[0].shape[0]
        if period == 1:
            tspec = pl.BlockSpec((1, SLAB), lambda i, j: (0, 0))
        else:
            assert period % tm == 0, (period, tm)
            nt = period // tm
            tspec = pl.BlockSpec((tm, SLAB), lambda i, j: (i % nt, 0))
        c, s1, s2 = tables
        in_specs += [tspec] * 2
        args += [c, s1 + s2]
        body = functools.partial(_mm_rope_body, rep=tn_out // SLAB, scale=scale)
        est += 4 * tm * SLAB * 4
    return pl.pallas_call(
        body,
        out_shape=jax.ShapeDtypeStruct((m, n_out), out_dtype),
        grid=(m // tm, n // tn),
        in_specs=in_specs,
        out_specs=pl.BlockSpec((tm, tn_out), lambda i, j: (i, j)),
        compiler_params=_cparams(("parallel", "parallel"), est),
        name=name,
    )(*args)


def _mm_resid_body(a_ref, w_ref, h_ref, gp_ref, gn_ref, ho_ref, xo_ref, acc_ref, *, nk):
    kk = pl.program_id(1)

    def prod():
        return jnp.dot(a_ref[...].astype(BF16), w_ref[...].astype(BF16), preferred_element_type=F32)

    @pl.when(kk == 0)
    def _():
        acc_ref[...] = prod()

    @pl.when(kk > 0)
    def _():
        acc_ref[...] += prod()

    @pl.when(kk == nk - 1)
    def _():
        h = h_ref[...] + _rms(acc_ref[...]) * gp_ref[...]
        ho_ref[...] = h
        xo_ref[...] = (_rms(h) * gn_ref[...]).astype(xo_ref.dtype)


def _matmul_resid(a, w, layer, h, g_post, g_next, name="matmul_resid"):
    m, k = a.shape
    n = w.shape[-1]
    ab, wb = a.dtype.itemsize, w.dtype.itemsize
    tk = _divisor(k, (6 << 20) // (n * wb), LANES)
    nk = k // tk

    def est(tm):
        casts = (tk * n * 2 if wb != 2 else 0) + (tm * tk * 2 if ab != 2 else 0)
        return tm * n * 4 + 2 * (tm * tk * ab + tk * n * wb + 2 * tm * n * 4 + tm * n * 2) + casts

    tm = _divisor(m, 1408, BF16_ROWS)
    while est(tm) > (50 << 20) and tm > BF16_ROWS:
        tm = _divisor(m, tm - BF16_ROWS, BF16_ROWS)
    row = pl.BlockSpec((tm, n), lambda i, kk: (i, 0))
    vec = pl.BlockSpec((1, n), lambda i, kk: (0, 0))
    return pl.pallas_call(
        functools.partial(_mm_resid_body, nk=nk),
        out_shape=(jax.ShapeDtypeStruct((m, n), F32), jax.ShapeDtypeStruct((m, n), BF16)),
        grid=(m // tm, nk),
        in_specs=[
            pl.BlockSpec((tm, tk), lambda i, kk: (i, kk)),
            pl.BlockSpec((None, tk, n), lambda i, kk: (layer, kk, 0)),
            row, vec, vec,
        ],
        out_specs=(row, row),
        scratch_shapes=[pltpu.VMEM((tm, n), F32)],
        compiler_params=_cparams(("parallel", "arbitrary"), est(tm)),
        name=name,
    )(a, w, h, g_post.reshape(1, n), g_next.reshape(1, n))


def _rope_tables(pos, d):
    inv = ROPE_BASE ** (-jnp.arange(0, d, 2, dtype=F32) / d)
    ang = pos.astype(F32)[:, None] * inv[None, :]
    return jnp.cos(ang), jnp.sin(ang)


def _slab_rope_tables(pos):
    cos, sin = _rope_tables(pos, MLA_ROPE)
    n, half = cos.shape
    one = jnp.ones((n, MLA_NOPE), F32)
    z_nope = jnp.zeros((n, MLA_NOPE), F32)
    z_half = jnp.zeros((n, half), F32)
    z_tail = jnp.zeros((n, SLAB - MLA_NOPE - MLA_ROPE), F32)
    c = jnp.concatenate([one, cos, cos, z_tail], axis=1)
    s1 = jnp.concatenate([z_nope, -sin, z_half, z_tail], axis=1)
    s2 = jnp.concatenate([z_nope, z_half, sin, z_tail], axis=1)
    return c, s1, s2


def _swapped_slabs(w):
    half = MLA_ROPE // 2
    nope, x1, x2 = w[..., :MLA_NOPE], w[..., MLA_NOPE:MLA_NOPE + half], w[..., MLA_NOPE + half:]
    z_tail = jnp.zeros(w.shape[:-1] + (SLAB - MLA_NOPE - MLA_ROPE,), w.dtype)
    return jnp.concatenate([nope, x1, x2, z_tail, jnp.zeros_like(nope), x2, x1, z_tail], axis=-1)


def _slab_rope(x, c, s1, s2):
    half = MLA_ROPE // 2
    return x * c + pltpu.roll(x, SLAB - half, 1) * s1 + pltpu.roll(x, half, 1) * s2


def _ret_rope(x, cos, sin):
    half = RET_DK // 2
    x1, x2 = x[:, :half], x[:, half:]
    return jnp.concatenate([x1 * cos - x2 * sin, x1 * sin + x2 * cos], axis=1)


def _ret_log_gamma():
    return jnp.log1p(-jnp.power(2.0, -5.0 - jnp.arange(RET_HEADS, dtype=F32)))


def _ret_prompt_body(lg_ref, q_ref, k_ref, v_ref, g_ref, cos_ref, sin_ref, o_ref, so_ref, st_ref, *, pad, n_chunks, hp):
    hg = pl.program_id(1)
    c = pl.program_id(2)
    chunk = q_ref.shape[0]

    @pl.when(c == 0)
    def _():
        st_ref[...] = jnp.zeros_like(st_ref)

    lead = jnp.where(c == 0, float(pad), 0.0)
    cos, sin = cos_ref[...], sin_ref[...]
    n_col = lax.broadcasted_iota(jnp.int32, (chunk, 1), 0).astype(F32)
    n_row = lax.broadcasted_iota(jnp.int32, (1, chunk), 1).astype(F32)
    diff = n_col - n_row
    for i in range(hp):
        lg = lg_ref[hg * hp + i]
        q = _ret_rope(q_ref[:, i * RET_DK:(i + 1) * RET_DK].astype(F32), cos, sin)
        k = _ret_rope(k_ref[:, i * RET_DK:(i + 1) * RET_DK].astype(F32), cos, sin) * (RET_DK ** -0.5)
        v = v_ref[:, i * RET_DV:(i + 1) * RET_DV]
        intra = jnp.where(diff >= 0, jnp.exp(lg * jnp.maximum(diff, 0.0)), 0.0)
        q_dec = jnp.exp(lg * (n_col + 1.0 - lead))
        k_dec = jnp.exp(lg * (chunk - 1.0 - n_col))
        s_dec = jnp.exp(jnp.full((1, 1), lg * (chunk - lead), F32))
        st = st_ref[i]
        scores = lax.dot_general(q.astype(BF16), k.astype(BF16), (((1,), (1,)), ((), ())),
                                 preferred_element_type=F32) * intra
        out = jnp.dot(scores.astype(BF16), v, preferred_element_type=F32)
        out = out + jnp.dot((q * q_dec).astype(BF16), st.astype(BF16), preferred_element_type=F32)
        kt = (k * k_dec).T.astype(BF16)
        st_ref[i] = st * s_dec + jnp.dot(kt, v, preferred_element_type=F32)
        g = g_ref[:, i * RET_DV:(i + 1) * RET_DV].astype(F32)
        o_ref[:, i * RET_DV:(i + 1) * RET_DV] = (_silu(g) * _rms(out)).astype(o_ref.dtype)

    @pl.when(c == n_chunks - 1)
    def _():
        so_ref[...] = st_ref[...]


def _ret_prompt(qkvg, cos, sin, pad):
    b, s, _ = qkvg.shape
    chunk = RET_CHUNK
    nc = s // chunk
    hp = _divisor(RET_HEADS, 8, 1)
    ng = RET_HEADS // hp
    body = functools.partial(_ret_prompt_body, pad=pad, n_chunks=nc, hp=hp)
    return pl.pallas_call(
        body,
        out_shape=(jax.ShapeDtypeStruct((b, s, RET_HEADS * RET_DV), BF16),
                   jax.ShapeDtypeStruct((b, RET_HEADS, RET_DK, RET_DV), F32)),
        grid=(b, ng, nc),
        in_specs=[
            pl.BlockSpec(memory_space=pltpu.SMEM),
            pl.BlockSpec((None, chunk, hp * RET_DK), lambda bi, h, c: (bi, c, h)),
            pl.BlockSpec((None, chunk, hp * RET_DK), lambda bi, h, c: (bi, c, ng + h)),
            pl.BlockSpec((None, chunk, hp * RET_DV), lambda bi, h, c: (bi, c, ng + h)),
            pl.BlockSpec((None, chunk, hp * RET_DV), lambda bi, h, c: (bi, c, 2 * ng + h)),
            pl.BlockSpec((chunk, RET_DK // 2), lambda bi, h, c: (c, 0)),
            pl.BlockSpec((chunk, RET_DK // 2), lambda bi, h, c: (c, 0)),
        ],
        out_specs=(
            pl.BlockSpec((None, chunk, hp * RET_DV), lambda bi, h, c: (bi, c, h)),
            pl.BlockSpec((None, hp, RET_DK, RET_DV), lambda bi, h, c: (bi, h, 0, 0)),
        ),
        scratch_shapes=[pltpu.VMEM((hp, RET_DK, RET_DV), F32)],
        compiler_params=_cparams(("parallel", "parallel", "arbitrary"), 16 << 20),
        name="ret_prompt",
    )(_ret_log_gamma(), qkvg, qkvg, qkvg, qkvg, cos, sin)


def _ret_sample_body(lg_ref, q_ref, k_ref, v_ref, g_ref, cos_ref, sin_ref, st_ref, o_ref, so_ref):
    h = pl.program_id(1)
    bt = q_ref.shape[0]
    gamma = jnp.exp(jnp.full((1, 1), lg_ref[h], F32))
    cos, sin = cos_ref[...], sin_ref[...]
    q = _ret_rope(q_ref[...], cos, sin)
    k = _ret_rope(k_ref[...], cos, sin) * (RET_DK ** -0.5)
    v = v_ref[...]
    qk = jnp.sum(q * k, axis=-1, keepdims=True)
    qg = (q * gamma).astype(BF16)
    eye = lax.broadcasted_iota(jnp.int32, (RET_DK, RET_DK), 0) == lax.broadcasted_iota(jnp.int32, (RET_DK, RET_DK), 1)
    rows = []
    for i in range(bt):
        st = st_ref[i]
        cross = jnp.dot(qg, st.astype(BF16), preferred_element_type=F32)
        rows.append(cross[i:i + 1])
        k_col = jnp.sum(jnp.where(eye, k[i:i + 1], 0.0), axis=1, keepdims=True)
        so_ref[i] = st * gamma + k_col * v[i:i + 1]
    out = qk * v + jnp.concatenate(rows, axis=0)
    o_ref[...] = _silu(g_ref[...]) * _rms(out)


def _ret_sample(qkvg, state, cos, sin, out_buf, layer):
    db = qkvg.shape[0]
    bt = SUBLANES
    kq = RET_HEADS
    n_layers = state.shape[0]
    st_block = (None, bt, None, RET_DK, RET_DV)
    in_specs = [
        pl.BlockSpec(memory_space=pltpu.SMEM),
        pl.BlockSpec((bt, RET_DK), lambda bi, h: (bi, h)),
        pl.BlockSpec((bt, RET_DK), lambda bi, h: (bi, kq + h)),
        pl.BlockSpec((bt, RET_DV), lambda bi, h: (bi, kq + h)),
        pl.BlockSpec((bt, RET_DV), lambda bi, h: (bi, 2 * kq + h)),
        pl.BlockSpec((1, RET_DK // 2), lambda bi, h: (0, 0)),
        pl.BlockSpec((1, RET_DK // 2), lambda bi, h: (0, 0)),
        pl.BlockSpec(st_block, lambda bi, h: (layer, bi, h, 0, 0)),
    ]
    args = [_ret_log_gamma(), qkvg, qkvg, qkvg, qkvg, cos, sin, state]
    aliases = {}
    body = _ret_sample_body
    if out_buf is not None:
        in_specs.append(pl.BlockSpec(memory_space=pl.ANY))
        args.append(out_buf)
        aliases = {len(args) - 1: 1}
        body = lambda *refs: _ret_sample_body(*refs[:8], *refs[9:])
    return pl.pallas_call(
        body,
        out_shape=(jax.ShapeDtypeStruct((db, RET_HEADS * RET_DV), F32),
                   jax.ShapeDtypeStruct((n_layers, db, RET_HEADS, RET_DK, RET_DV), F32)),
        grid=(db // bt, RET_HEADS),
        in_specs=in_specs,
        out_specs=(
            pl.BlockSpec((bt, RET_DV), lambda bi, h: (bi, h)),
            pl.BlockSpec(st_block, lambda bi, h: (layer, bi, h, 0, 0)),
        ),
        input_output_aliases=aliases,
        compiler_params=_cparams(("parallel", "parallel"), 4 * bt * RET_DK * RET_DV * 4 + (4 << 20)),
        name="ret_sample",
    )(*args)


def _conv_act(g2, g1, g0, up, cw_ref, cb_ref):
    conv = g2 * cw_ref[0:1, :] + g1 * cw_ref[1:2, :] + g0 * cw_ref[2:3, :] + cb_ref[...]
    return _silu(conv) * up


def _ffn1_seq_body(x_ref, wg_ref, wu_ref, cw_ref, cb_ref, act_ref, cs_ref, carry_ref):
    i = pl.program_id(1)
    j = pl.program_id(2)
    x = x_ref[...]
    gate = jnp.dot(x, wg_ref[...].astype(BF16), preferred_element_type=F32)
    up = jnp.dot(x, wu_ref[...].astype(BF16), preferred_element_type=F32)
    tm = gate.shape[0]
    head = 2 * SUBLANES
    act = _conv_act(pltpu.roll(gate, 2, 0), pltpu.roll(gate, 1, 0), gate, up, cw_ref, cb_ref)
    act_ref[head:, :] = act[head:].astype(act_ref.dtype)
    @pl.when(i == 0)
    def _():
        carry_ref[j] = jnp.zeros(carry_ref.shape[1:], F32)

    win = jnp.concatenate([carry_ref[j], gate[:head]], axis=0)
    w1 = pltpu.roll(win, 1, 0)[SUBLANES:]
    w2 = pltpu.roll(win, 2, 0)[SUBLANES:]
    act_ref[:head, :] = _conv_act(w2, w1, gate[:head], up[:head], cw_ref, cb_ref).astype(act_ref.dtype)
    tail = gate[tm - SUBLANES:]
    carry_ref[j] = tail
    cs_ref[...] = tail


def _ffn1_seq(x, w_gate, w_up, conv_w, conv_b, layer, n_seq):
    m, k = x.shape
    f = w_gate.shape[-1]
    s = m // n_seq
    tm = _divisor(s, 1408, BF16_ROWS)
    tn = _divisor(f, 512, LANES)
    ni, nj = s // tm, f // tn
    est = 2 * (tm * k * 2 + 2 * k * tn * 4 + tm * tn * 2) + 2 * k * tn * 2 + 4 * tm * tn * 4
    act, tails = pl.pallas_call(
        _ffn1_seq_body,
        out_shape=(jax.ShapeDtypeStruct((m, f), BF16), jax.ShapeDtypeStruct((n_seq, ni, SUBLANES, f), F32)),
        grid=(n_seq, ni, nj),
        in_specs=[
            pl.BlockSpec((tm, k), lambda b, i, j: (b * ni + i, 0)),
            pl.BlockSpec((None, k, tn), lambda b, i, j: (layer, 0, j)),
            pl.BlockSpec((None, k, tn), lambda b, i, j: (layer, 0, j)),
            pl.BlockSpec((None, CONV_W, tn), lambda b, i, j: (layer, 0, j)),
            pl.BlockSpec((None, 1, tn), lambda b, i, j: (layer, 0, j)),
        ],
        out_specs=(
            pl.BlockSpec((tm, tn), lambda b, i, j: (b * ni + i, j)),
            pl.BlockSpec((None, None, SUBLANES, tn), lambda b, i, j: (b, i, 0, j)),
        ),
        scratch_shapes=[pltpu.VMEM((nj, SUBLANES, tn), F32)],
        compiler_params=_cparams(("arbitrary", "arbitrary", "arbitrary"), est),
        name="ffn1_seq",
    )(x, w_gate, w_up, conv_w, conv_b.reshape(conv_b.shape[0], 1, f))
    return act, tails[:, ni - 1]


def _ffn1_tok_body(x_ref, wg_ref, wu_ref, cw_ref, cb_ref, s0_ref, s1_ref, act_ref, gate_ref):
    x = x_ref[...]
    gate = jnp.dot(x, wg_ref[...].astype(BF16), preferred_element_type=F32)
    up = jnp.dot(x, wu_ref[...].astype(BF16), preferred_element_type=F32)
    act_ref[...] = _conv_act(s0_ref[...], s1_ref[...], gate, up, cw_ref, cb_ref).astype(act_ref.dtype)
    gate_ref[...] = gate


def _ffn1_tok(x, w_gate, w_up, conv_w, conv_b, layer, conv_state):
    m, k = x.shape
    f = w_gate.shape[-1]
    tn = _divisor(f, 1024, LANES)
    nj = f // tn
    cs = conv_state.reshape(m, (CONV_W - 1) * f)
    est = 2 * (m * k * 2 + 2 * k * tn * 4 + 6 * m * tn * 4) + 2 * k * tn * 2
    return pl.pallas_call(
        _ffn1_tok_body,
        out_shape=(jax.ShapeDtypeStruct((m, f), BF16), jax.ShapeDtypeStruct((m, f), F32)),
        grid=(nj,),
        in_specs=[
            pl.BlockSpec((m, k), lambda j: (0, 0)),
            pl.BlockSpec((None, k, tn), lambda j: (layer, 0, j)),
            pl.BlockSpec((None, k, tn), lambda j: (layer, 0, j)),
            pl.BlockSpec((None, CONV_W, tn), lambda j: (layer, 0, j)),
            pl.BlockSpec((None, 1, tn), lambda j: (layer, 0, j)),
            pl.BlockSpec((m, tn), lambda j: (0, j)),
            pl.BlockSpec((m, tn), lambda j: (0, nj + j)),
        ],
        out_specs=(pl.BlockSpec((m, tn), lambda j: (0, j)), pl.BlockSpec((m, tn), lambda j: (0, j))),
        compiler_params=_cparams(("parallel",), est),
        name="ffn1_tok",
    )(x, w_gate, w_up, conv_w, conv_b.reshape(conv_b.shape[0], 1, f), cs, cs)


def _kv_post_body(y_ref, g_ref, c_ref, s1_ref, s2_ref, cf_ref, cb_ref, r_ref):
    y = y_ref[...]
    c = _rms(y[:, :KV_LORA]) * g_ref[...]
    cf_ref[...] = c
    cb_ref[...] = c.astype(cb_ref.dtype)
    r_ref[...] = _slab_rope(y[:, KV_LORA:], c_ref[...], s1_ref[...], s2_ref[...])


def _kv_post(y, g, tables):
    m = y.shape[0]
    t_rows = tables[0].shape[0]
    tm = _divisor(m if t_rows == 1 else t_rows, 512, BF16_ROWS)
    if t_rows == 1:
        tspec = pl.BlockSpec((1, SLAB), lambda i: (0, 0))
    else:
        nt = t_rows // tm
        tspec = pl.BlockSpec((tm, SLAB), lambda i: (i % nt, 0))
    return pl.pallas_call(
        _kv_post_body,
        out_shape=(jax.ShapeDtypeStruct((m, KV_LORA), F32), jax.ShapeDtypeStruct((m, KV_LORA), BF16),
                   jax.ShapeDtypeStruct((m, SLAB), F32)),
        grid=(m // tm,),
        in_specs=[pl.BlockSpec((tm, KV_LORA + SLAB), lambda i: (i, 0)), pl.BlockSpec((1, KV_LORA), lambda i: (0, 0)),
                  tspec, tspec, tspec],
        out_specs=(pl.BlockSpec((tm, KV_LORA), lambda i: (i, 0)), pl.BlockSpec((tm, KV_LORA), lambda i: (i, 0)),
                   pl.BlockSpec((tm, SLAB), lambda i: (i, 0))),
        compiler_params=_cparams(("parallel",), 16 << 20),
        name="kv_post",
    )(y, g.reshape(1, KV_LORA), *tables)


ATTN_BLOCK = 3 * LANES


def _v_up_t_body(c_ref, w_ref, o_ref):
    o_ref[...] = lax.dot_general(w_ref[...].astype(BF16), c_ref[...], (((1,), (1,)), ((), ())),
                                 preferred_element_type=F32).astype(o_ref.dtype)


def _v_up_t(c, w_uv_t, n_seq, tk):
    m, kl = c.shape
    n = w_uv_t.shape[0]
    nk = m // n_seq // tk
    tn = _divisor(n, 1024, LANES)
    return pl.pallas_call(
        _v_up_t_body,
        out_shape=jax.ShapeDtypeStruct((n_seq, nk, n, tk), BF16),
        grid=(n_seq, nk, n // tn),
        in_specs=[pl.BlockSpec((tk, kl), lambda b, kb, j: (b * nk + kb, 0)), pl.BlockSpec((tn, kl), lambda b, kb, j: (j, 0))],
        out_specs=pl.BlockSpec((None, None, tn, tk), lambda b, kb, j: (b, kb, j, 0)),
        compiler_params=_cparams(("parallel", "parallel", "parallel"), 16 << 20),
        name="v_up_t",
    )(c, w_uv_t)


def _attn_prompt_body(q_ref, k_ref, vt_ref, bias_ref, o_ref, m_ref, l_ref, acc_ref, s_ref, *, pad, blk, hp):
    qi = pl.program_id(2)
    m_ref[...] = jnp.full_like(m_ref, NEG)
    l_ref[...] = jnp.zeros_like(l_ref)
    acc_ref[...] = jnp.zeros_like(acc_ref)
    nt = (((1,), (1,)), ((), ()))

    def scores(ki, slot):
        start = ki * blk if isinstance(ki, int) else pl.multiple_of(ki * blk, blk)
        for i in range(hp):
            kb = k_ref[pl.ds(start, blk), i * SLAB:(i + 1) * SLAB]
            s_ref[slot, i] = lax.dot_general(kb, q_ref[:, i * SLAB:(i + 1) * SLAB], nt, preferred_element_type=F32)

    def consume(ki, slot, kind):
        for i in range(hp):
            s = s_ref[slot, i]
            if kind == "first":
                s = s[pad:]
            elif kind == "diag":
                s = s + bias_ref[jnp.where(qi == 0, 0, 1)]
            m_old = m_ref[i]
            m_new = jnp.maximum(m_old, jnp.max(s, axis=0, keepdims=True))
            alpha = jnp.exp(m_old - m_new)
            p = jnp.exp(s - m_new)
            l_ref[i] = alpha * l_ref[i] + jnp.sum(p, axis=0, keepdims=True)
            pb = p.astype(BF16)
            if kind == "first":
                pb = jnp.concatenate([jnp.zeros((pad, blk), BF16), pb], axis=0)
            vt = vt_ref[ki, i * MLA_V:(i + 1) * MLA_V, :]
            acc_ref[i] = alpha * acc_ref[i] + jnp.dot(vt, pb, preferred_element_type=F32)
            m_ref[i] = m_new

    scores(0, 0)

    @pl.when(qi > 0)
    def _():
        scores(1, 1)
        consume(0, 0, "first")

    def mid(ki, carry):
        slot = lax.rem(ki, 2)
        consume(ki, slot, "full")
        scores(ki + 1, 1 - slot)
        return carry

    lax.fori_loop(1, qi, mid, 0)
    consume(qi, lax.rem(qi, 2), "diag")

    for i in range(hp):
        o_ref[:, i * MLA_V:(i + 1) * MLA_V] = (acc_ref[i] / l_ref[i]).T.astype(o_ref.dtype)


def _attn_bias(blk, pad):
    kpos = jnp.arange(blk)[:, None]
    qpos = jnp.arange(blk)[None, :]
    causal = kpos <= qpos
    first = causal & ((kpos >= pad) | (qpos < pad))
    return jnp.stack([jnp.where(first, 0.0, NEG), jnp.where(causal, 0.0, NEG)]).astype(F32)


def _attn_prompt(q, k, vt, pad):
    b, s, _ = q.shape
    nk, blk = vt.shape[1], vt.shape[3]
    hp = _divisor(MLA_HEADS, 4, 1)
    body = functools.partial(_attn_prompt_body, pad=pad, blk=blk, hp=hp)
    est = (2 * (blk * hp * SLAB * 2 + s * hp * SLAB * 2 + s * hp * MLA_V * 2 + 2 * blk * blk * 4 + blk * hp * MLA_V * 2)
           + 8 * hp * blk * blk * 4)
    return pl.pallas_call(
        body,
        out_shape=jax.ShapeDtypeStruct((b, s, MLA_HEADS * MLA_V), BF16),
        grid=(b, MLA_HEADS // hp, s // blk),
        in_specs=[
            pl.BlockSpec((None, blk, hp * SLAB), lambda bi, h, qi: (bi, qi, h)),
            pl.BlockSpec((None, s, hp * SLAB), lambda bi, h, qi: (bi, 0, h)),
            pl.BlockSpec((None, nk, hp * MLA_V, blk), lambda bi, h, qi: (bi, 0, h, 0)),
            pl.BlockSpec((2, blk, blk), lambda bi, h, qi: (0, 0, 0)),
        ],
        out_specs=pl.BlockSpec((None, blk, hp * MLA_V), lambda bi, h, qi: (bi, qi, h)),
        scratch_shapes=[pltpu.VMEM((hp, 1, blk), F32), pltpu.VMEM((hp, 1, blk), F32), pltpu.VMEM((hp, MLA_V, blk), F32),
                        pltpu.VMEM((2, hp, blk, blk), F32)],
        compiler_params=_cparams(("parallel", "parallel", "arbitrary"), est),
        name="attn_prompt",
    )(q, k, vt, _attn_bias(blk, pad))


def _q_absorb_body(q_ref, w_ref, ql_ref):
    ql_ref[...] = lax.dot_general(q_ref[...][:, :MLA_NOPE].astype(BF16), w_ref[...].astype(BF16), (((1,), (1,)), ((), ())),
                                  preferred_element_type=F32)


def _q_absorb(q, w_uk2):
    db = q.shape[0]
    return pl.pallas_call(
        _q_absorb_body,
        out_shape=jax.ShapeDtypeStruct((db, MLA_HEADS * KV_LORA), F32),
        grid=(MLA_HEADS,),
        in_specs=[pl.BlockSpec((db, SLAB), lambda h: (0, h)), pl.BlockSpec((KV_LORA, MLA_NOPE), lambda h: (0, h))],
        out_specs=pl.BlockSpec((db, KV_LORA), lambda h: (0, h)),
        compiler_params=_cparams(("parallel",), 8 << 20),
        name="q_absorb",
    )(q, w_uk2)


def _page_copies(pt_ref, lat_hbm, rope_hbm, latf_ref, ropef_ref, sem_ref, bi, g, slot, *, nb, pages):
    page = lat_hbm.shape[1]
    out = []
    for b in range(nb):
        for i in range(pages):
            pid = pt_ref[bi, g, b * pages + i]
            rows = pl.ds(i * page, page)
            out.append(pltpu.make_async_copy(lat_hbm.at[pid], latf_ref.at[slot, b, rows, :], sem_ref.at[slot]))
            out.append(pltpu.make_async_copy(rope_hbm.at[pid], ropef_ref.at[slot, b, :, rows], sem_ref.at[slot]))
    return out


def _attn_sample_body(pt_ref, ql_ref, qs_ref, cn_ref, rn_ref, lat_hbm, rope_hbm, o_ref, m_ref, l_ref, acc_ref, latb_ref,
                      ropeb_ref, latf_ref, ropef_ref, sem_ref, *, pages, n_groups, nb, n_steps):
    bi = pl.program_id(0)
    g = pl.program_id(1)
    step = bi * n_groups + g
    slot = lax.rem(step, 2)
    copies = functools.partial(_page_copies, pt_ref, lat_hbm, rope_hbm, latf_ref, ropef_ref, sem_ref, nb=nb, pages=pages)

    @pl.when(step == 0)
    def _():
        for c in copies(bi, g, slot):
            c.start()

    @pl.when(step + 1 < n_steps)
    def _():
        nxt = step + 1
        for c in copies(lax.div(nxt, n_groups), lax.rem(nxt, n_groups), 1 - slot):
            c.start()

    for c in copies(bi, g, slot):
        c.wait()

    @pl.when(g == 0)
    def _():
        m_ref[...] = jnp.full_like(m_ref, NEG)
        l_ref[...] = jnp.zeros_like(l_ref)
        acc_ref[...] = jnp.zeros_like(acc_ref)

    nt = (((1,), (1,)), ((), ()))
    for b in range(nb):
        latb_ref[b] = latf_ref[slot, b].astype(BF16)
        ropeb_ref[b] = ropef_ref[slot, b].astype(BF16)
        ql = ql_ref[b]
        qr = qs_ref[b][:, MLA_NOPE:MLA_NOPE + MLA_ROPE]
        s = lax.dot_general(ql.astype(BF16), latb_ref[b], nt, preferred_element_type=F32)
        s = s + jnp.dot(qr.astype(BF16), ropeb_ref[b], preferred_element_type=F32)
        m_old = m_ref[b]
        m_new = jnp.maximum(m_old, jnp.max(s, axis=-1, keepdims=True))
        alpha = jnp.exp(m_old - m_new)
        p = jnp.exp(s - m_new)
        l_ref[b] = alpha * l_ref[b] + jnp.sum(p, axis=-1, keepdims=True)
        acc_ref[b] = alpha * acc_ref[b] + jnp.dot(p.astype(BF16), latb_ref[b], preferred_element_type=F32)
        m_ref[b] = m_new

    @pl.when(g == n_groups - 1)
    def _():
        for b in range(nb):
            ql = ql_ref[b]
            qr = qs_ref[b][:, MLA_NOPE:MLA_NOPE + MLA_ROPE]
            cn = cn_ref[b]
            rn = rn_ref[b][:, MLA_NOPE:MLA_NOPE + MLA_ROPE]
            s_new = jnp.sum(ql * cn, axis=-1, keepdims=True) + jnp.sum(qr * rn, axis=-1, keepdims=True)
            m_old = m_ref[b]
            m_new = jnp.maximum(m_old, s_new)
            alpha = jnp.exp(m_old - m_new)
            p_new = jnp.exp(s_new - m_new)
            l = alpha * l_ref[b] + p_new
            o_ref[b] = (alpha * acc_ref[b] + p_new * cn) / l


def _attn_sample(page_table, cache_lat, cache_rope_t, q_lat, q_slab, c_new, r_new):
    db, n_pages = page_table.shape
    page = cache_lat.shape[1]
    pages = _divisor(n_pages, 16, 1)
    nb = _divisor(db, 4, 1)
    n_groups = n_pages // pages
    n_steps = (db // nb) * n_groups
    body = functools.partial(_attn_sample_body, pages=pages, n_groups=n_groups, nb=nb, n_steps=n_steps)

    page_table = page_table.reshape(db // nb, nb, n_groups, pages).transpose(0, 2, 1, 3).reshape(db // nb, n_groups, nb * pages)

    def seq_spec(rows, width):
        return pl.BlockSpec((nb, rows, width), lambda bi, g, pt: (bi, 0, 0))

    hbm = pl.BlockSpec(memory_space=pl.ANY)
    keys = pages * page
    est = nb * keys * (KV_LORA + MLA_ROPE) * (2 * 4 + 2) + (2 << 20)
    return pl.pallas_call(
        body,
        out_shape=jax.ShapeDtypeStruct((db, MLA_HEADS, KV_LORA), F32),
        grid_spec=pltpu.PrefetchScalarGridSpec(
            num_scalar_prefetch=1,
            grid=(db // nb, n_groups),
            in_specs=[seq_spec(MLA_HEADS, KV_LORA), seq_spec(MLA_HEADS, SLAB), seq_spec(1, KV_LORA), seq_spec(1, SLAB),
                      hbm, hbm],
            out_specs=seq_spec(MLA_HEADS, KV_LORA),
            scratch_shapes=[
                pltpu.VMEM((nb, MLA_HEADS, 1), F32), pltpu.VMEM((nb, MLA_HEADS, 1), F32),
                pltpu.VMEM((nb, MLA_HEADS, KV_LORA), F32),
                pltpu.VMEM((nb, keys, KV_LORA), BF16), pltpu.VMEM((nb, MLA_ROPE, keys), BF16),
                pltpu.VMEM((2, nb, keys, KV_LORA), F32), pltpu.VMEM((2, nb, MLA_ROPE, keys), F32),
                pltpu.SemaphoreType.DMA((2,)),
            ],
        ),
        compiler_params=_cparams(("arbitrary", "arbitrary"), est),
        name="attn_sample",
    )(page_table, q_lat, q_slab, c_new, r_new, cache_lat, cache_rope_t)


def _head_out_body(o_ref, w_ref, y_ref):
    y_ref[...] = jnp.dot(o_ref[...].astype(BF16), w_ref[...].astype(BF16), preferred_element_type=F32).astype(y_ref.dtype)


def _head_out(o_lat, w_uv2):
    db = o_lat.shape[0]
    return pl.pallas_call(
        _head_out_body,
        out_shape=jax.ShapeDtypeStruct((db, MLA_HEADS * MLA_V), BF16),
        grid=(MLA_HEADS,),
        in_specs=[pl.BlockSpec((db, KV_LORA), lambda h: (0, h)), pl.BlockSpec((KV_LORA, MLA_V), lambda h: (0, h))],
        out_specs=pl.BlockSpec((db, MLA_V), lambda h: (0, h)),
        compiler_params=_cparams(("parallel",), 4 << 20),
        name="head_out",
    )(o_lat, w_uv2)


def _prep_weights(ret_w_qkvg, ret_w_o, ffn_w_gate, ffn_w_up, ffn_w_down, w_dkv, w_kr, w_uk, w_uv, mla_w_dq, mla_w_uq,
                  mla_w_o):
    tail = SLAB - MLA_NOPE - MLA_ROPE
    w_kv = jnp.concatenate([w_dkv, jnp.zeros((D_MODEL, MLA_NOPE), F32), w_kr, jnp.zeros((D_MODEL, tail), F32)], axis=1)
    w_uk_ext = jnp.pad(w_uk, ((0, 0), (0, 0), (0, SLAB - MLA_NOPE))).reshape(KV_LORA, MLA_HEADS * SLAB)
    n_b = mla_w_uq.shape[0]
    w_uq_ext = _swapped_slabs(mla_w_uq.reshape(n_b, Q_LORA, MLA_HEADS, MLA_NOPE + MLA_ROPE)).reshape(n_b, Q_LORA, -1)
    w_uv2 = w_uv.reshape(KV_LORA, MLA_HEADS * MLA_V)
    return dict(
        qkvg=ret_w_qkvg, ret_o=ret_w_o.astype(BF16), gate=ffn_w_gate, up=ffn_w_up, down=ffn_w_down.astype(BF16), kv=w_kv,
        uk_ext=w_uk_ext, uk2=w_uk.reshape(KV_LORA, MLA_HEADS * MLA_NOPE), uv2=w_uv2, uv_t=w_uv2.T, dq=mla_w_dq,
        uq_ext=w_uq_ext, mla_o=mla_w_o.astype(BF16),
    )


def _trunk(h, xn, w, norm_g, ffn_conv_w, ffn_conv_b, kv_in_g, kv_norm_g, mla_q_norm_g, ret_mixer, ffn1, kv_tables, attend):
    conv_states = []
    c_f32 = r_slab = kv_ctx = None
    q_scale = (MLA_NOPE + MLA_ROPE) ** -0.5
    for layer in range(DEPTH):
        g = norm_g[layer]
        if layer < N_A_LAYERS:
            qkvg = _matmul(xn, w["qkvg"], ret_mixer.qkvg_dtype, layer=layer, name="qkvg")
            gated = ret_mixer(layer, qkvg)
            h, xn = _matmul_resid(gated, w["ret_o"], layer, h, g[1], g[2], name="ret_o")
        else:
            j = layer - N_A_LAYERS
            cq = _matmul(xn, w["dq"], F32, layer=j, name="dq")
            cqn = _rms_cast(cq, mla_q_norm_g[j])
            q = _matmul(cqn, w["uq_ext"], attend.q_dtype, layer=j, rope=(kv_tables, q_scale), name="uq")
            o = attend(q, kv_ctx)
            h, xn = _matmul_resid(o, w["mla_o"], j, h, g[1], g[2], name="mla_o")
        act, cs = ffn1(layer, xn, w["gate"], w["up"], ffn_conv_w, ffn_conv_b)
        conv_states.append(cs)
        g_next = norm_g[layer + 1, 0] if layer + 1 < DEPTH else g[3]
        h, xn = _matmul_resid(act, w["down"], layer, h, g[3], g_next, name="down")
        if layer == N_A_LAYERS - 1:
            hn = _rms_cast(h, kv_in_g)
            y = _matmul(hn, w["kv"], F32, name="kv_down")
            c_f32, c_bf16, r_slab = _kv_post(y, kv_norm_g, kv_tables)
            kv_ctx = (c_f32, c_bf16, r_slab)
    return h, conv_states, c_f32, r_slab


class _PromptRetention:
    qkvg_dtype = BF16

    def __init__(self, n_seq, pad, cos, sin):
        self.n_seq, self.pad, self.cos, self.sin = n_seq, pad, cos, sin
        self.states = []

    def __call__(self, layer, qkvg):
        m, n = qkvg.shape
        gated, state = _ret_prompt(qkvg.reshape(self.n_seq, m // self.n_seq, n), self.cos, self.sin, self.pad)
        self.states.append(state)
        return gated.reshape(m, -1)


class _SampleRetention:
    qkvg_dtype = F32

    def __init__(self, state, cos, sin):
        self.state, self.cos, self.sin = state, cos, sin
        self.out = None

    def __call__(self, layer, qkvg):
        gated, self.out = _ret_sample(qkvg, self.state, self.cos, self.sin, self.out, layer)
        return gated


def kernel(x_prompt, x_sample, state_retention, state_conv, cache_kv_latent, cache_k_rope, page_table, meta_tokens, norm_g,
           ret_w_qkvg, ret_w_o, ffn_w_gate, ffn_w_up, ffn_w_down, ffn_conv_w, ffn_conv_b, kv_in_g, w_dkv, kv_norm_g, w_kr,
           w_uk, w_uv, mla_w_dq, mla_q_norm_g, mla_w_uq, mla_w_o):
    w = _prep_weights(ret_w_qkvg, ret_w_o, ffn_w_gate, ffn_w_up, ffn_w_down, w_dkv, w_kr, w_uk, w_uv, mla_w_dq, mla_w_uq,
                      mla_w_o)
    shared = (norm_g, ffn_conv_w, ffn_conv_b, kv_in_g, kv_norm_g, mla_q_norm_g)

    b, seq, d = x_prompt.shape
    pad = RET_CHUNK - N_META
    s_pad = pad + N_META + seq
    h0, xn0 = _embed_norm(x_prompt, meta_tokens, norm_g[0, 0], pad)
    pos_p = jnp.arange(s_pad) - pad
    cos_p, sin_p = _rope_tables(pos_p, RET_DK)
    tables_p = _slab_rope_tables(pos_p)
    ret_p = _PromptRetention(b, pad, cos_p, sin_p)

    def ffn1_p(layer, xn, wg, wu, cw, cb):
        act, tail = _ffn1_seq(xn, wg, wu, cw, cb, layer, b)
        return act, tail[:, SUBLANES - (CONV_W - 1):, :]

    def attend_p(q, kv_ctx):
        if "k" not in attend_p.cache:
            _, c_bf16, r_slab = kv_ctx
            attend_p.cache["k"] = _matmul(c_bf16, w["uk_ext"], BF16, slab=r_slab, name="k_up").reshape(b, s_pad, -1)
            attend_p.cache["vt"] = _v_up_t(c_bf16, w["uv_t"], b, _divisor(s_pad, ATTN_BLOCK, LANES))
        o = _attn_prompt(q.reshape(b, s_pad, -1), attend_p.cache["k"], attend_p.cache["vt"], pad)
        return o.reshape(b * s_pad, -1)

    attend_p.cache = {}
    attend_p.q_dtype = BF16
    h_p, conv_p, lat_p, rslab_p = _trunk(h0, xn0, w, *shared, ret_p, ffn1_p, tables_p, attend_p)
    y_prompt = h_p.reshape(b, s_pad, d)[:, pad + N_META:]
    lat_prompt = lat_p.reshape(b, s_pad, KV_LORA)[:, pad:]
    rope_prompt = rslab_p.reshape(b, s_pad, SLAB)[:, pad:, MLA_NOPE:MLA_NOPE + MLA_ROPE]

    db = x_sample.shape[0]
    pos_s = jnp.full((1,), PAST_LEN)
    cos_s, sin_s = _rope_tables(pos_s, RET_DK)
    tables_s = _slab_rope_tables(pos_s)
    ret_s = _SampleRetention(state_retention, cos_s, sin_s)
    cache_rope_t = jnp.swapaxes(cache_k_rope, 1, 2)

    def ffn1_s(layer, xn, wg, wu, cw, cb):
        act, gate = _ffn1_tok(xn, wg, wu, cw, cb, layer, state_conv[layer])
        return act, jnp.stack([state_conv[layer][:, 1], gate], axis=1)

    def attend_s(q, kv_ctx):
        c_f32, _, r_slab = kv_ctx
        q_lat = _q_absorb(q, w["uk2"])
        o_lat = _attn_sample(page_table, cache_kv_latent, cache_rope_t, q_lat.reshape(db, MLA_HEADS, KV_LORA),
                             q.reshape(db, MLA_HEADS, SLAB), c_f32.reshape(db, 1, KV_LORA), r_slab.reshape(db, 1, SLAB))
        return _head_out(o_lat.reshape(db, MLA_HEADS * KV_LORA), w["uv2"])

    attend_s.q_dtype = F32

    hs0 = x_sample.reshape(db, d)
    h_s, conv_s, lat_s, rslab_s = _trunk(hs0, _rms_cast(hs0, norm_g[0, 0]), w, *shared, ret_s, ffn1_s, tables_s, attend_s)

    return (
        y_prompt,
        h_s.reshape(db, 1, d),
        jnp.stack(ret_p.states),
        ret_s.out,
        jnp.stack(conv_p),
        jnp.stack(conv_s),
        lat_prompt,
        lat_s.reshape(db, 1, KV_LORA),
        rope_prompt,
        rslab_s[:, MLA_NOPE:MLA_NOPE + MLA_ROPE].reshape(db, 1, MLA_ROPE),
    )
```

```python
import functools

import jax
import jax.numpy as jnp
from jax import lax
from jax.experimental import pallas as pl
from jax.experimental.pallas import tpu as pltpu

D_MODEL = 2048
SEQ = 4096
DEPTH = 4
PAST_LEN = 8192
PAGE_SIZE = 128
N_META = 16
N_A_LAYERS = DEPTH // 2
RET_HEADS = 8
RET_DK = D_MODEL // RET_HEADS
RET_DV = 2 * D_MODEL // RET_HEADS
RET_CHUNK = 128
MLA_HEADS = D_MODEL // 128
MLA_NOPE = 128
MLA_ROPE = 64
MLA_V = 128
Q_LORA = 512
KV_LORA = 512
D_FF = 11 * D_MODEL // 4
CONV_W = 3
ROPE_BASE = 10000.0
EPS = 1e-6

V7X_VMEM_BYTES = 64 * 1024 * 1024
VMEM_REQUEST_CAP = V7X_VMEM_BYTES - 8 * 1024 * 1024
LANES = 128
SUBLANES = 8
BF16_ROWS = 16
SLAB = 2 * LANES
NEG = -1e30

F32 = jnp.float32
BF16 = jnp.bfloat16


def _cparams(semantics, vmem_bytes):
    limit = int(min(max(vmem_bytes + (6 << 20), 16 << 20), VMEM_REQUEST_CAP))
    return pltpu.CompilerParams(dimension_semantics=semantics, vmem_limit_bytes=limit)


def _divisor(n, cap, mult):
    d = (min(n, cap) // mult) * mult
    while d >= mult:
        if n % d == 0:
            return d
        d -= mult
    return n


def _silu(x):
    return x * (1.0 / (1.0 + jnp.exp(-x)))


def _rms(x):
    return x * lax.rsqrt(jnp.mean(x * x, axis=-1, keepdims=True) + EPS)


def _rms_cast_body(x_ref, g_ref, o_ref):
    o_ref[...] = (_rms(x_ref[...]) * g_ref[...]).astype(o_ref.dtype)


def _rms_cast(x, g, out_dtype=BF16):
    m, d = x.shape
    tm = _divisor(m, 512, BF16_ROWS)
    return pl.pallas_call(
        _rms_cast_body,
        out_shape=jax.ShapeDtypeStruct((m, d), out_dtype),
        grid=(m // tm,),
        in_specs=[pl.BlockSpec((tm, d), lambda i: (i, 0)), pl.BlockSpec((1, d), lambda i: (0, 0))],
        out_specs=pl.BlockSpec((tm, d), lambda i: (i, 0)),
        compiler_params=_cparams(("parallel",), 2 * tm * d * 6 + tm * d * 8),
        name="rms_cast",
    )(x, g.reshape(1, d))


def _embed_norm_body(x_ref, meta_ref, g_ref, h_ref, xo_ref, *, pad):
    def emit(rows):
        h_ref[...] = rows
        xo_ref[...] = (_rms(rows) * g_ref[...]).astype(xo_ref.dtype)

    @pl.when(pl.program_id(1) == 0)
    def _():
        emit(jnp.concatenate([jnp.zeros((pad, meta_ref.shape[1]), F32), meta_ref[...]], axis=0))

    @pl.when(pl.program_id(1) > 0)
    def _():
        emit(x_ref[...])


def _embed_norm(x, meta, g, pad):
    b, seq, d = x.shape
    blk = pad + meta.shape[0]
    assert seq % blk == 0, (seq, blk)
    nblk = seq // blk + 1
    out = pl.BlockSpec((blk, d), lambda bi, i: (bi * nblk + i, 0))
    return pl.pallas_call(
        functools.partial(_embed_norm_body, pad=pad),
        out_shape=(jax.ShapeDtypeStruct((b * nblk * blk, d), F32), jax.ShapeDtypeStruct((b * nblk * blk, d), BF16)),
        grid=(b, nblk),
        in_specs=[pl.BlockSpec((None, blk, d), lambda bi, i: (bi, jnp.maximum(i - 1, 0), 0)),
                  pl.BlockSpec(meta.shape, lambda bi, i: (0, 0)), pl.BlockSpec((1, d), lambda bi, i: (0, 0))],
        out_specs=(out, out),
        compiler_params=_cparams(("parallel", "arbitrary"), 16 * blk * d * 4),
        name="embed_norm",
    )(x, meta, g.reshape(1, d))


def _resid_norm_body(h_ref, y_ref, gp_ref, gn_ref, ho_ref, xo_ref):
    h = h_ref[...] + _rms(y_ref[...]) * gp_ref[...]
    ho_ref[...] = h
    xo_ref[...] = (_rms(h) * gn_ref[...]).astype(xo_ref.dtype)


def _resid_norm(h, y, g_post, g_next):
    m, d = h.shape
    tm = _divisor(m, 256, BF16_ROWS)
    row = pl.BlockSpec((tm, d), lambda i: (i, 0))
    vec = pl.BlockSpec((1, d), lambda i: (0, 0))
    return pl.pallas_call(
        _resid_norm_body,
        out_shape=(jax.ShapeDtypeStruct((m, d), F32), jax.ShapeDtypeStruct((m, d), BF16)),
        grid=(m // tm,),
        in_specs=[row, row, vec, vec],
        out_specs=(row, row),
        compiler_params=_cparams(("parallel",), 2 * tm * d * 14 + tm * d * 12),
        name="resid_norm",
    )(h, y, g_post.reshape(1, d), g_next.reshape(1, d))


def _mm_body(a_ref, w_ref, o_ref):
    a = a_ref[...].astype(BF16)
    o_ref[...] = jnp.dot(a, w_ref[...].astype(BF16), preferred_element_type=F32).astype(o_ref.dtype)


def _mm_slab_body(a_ref, w_ref, s_ref, o_ref, *, rep):
    a = a_ref[...].astype(BF16)
    acc = jnp.dot(a, w_ref[...].astype(BF16), preferred_element_type=F32)
    s = s_ref[...]
    o_ref[...] = (acc + jnp.concatenate([s] * rep, axis=1)).astype(o_ref.dtype)


def _mm_tiles(m, k, n, ab, wb, ob, mult, budget):
    def est(tm, tn):
        return 2 * (tm * k * ab + k * tn * wb + tm * tn * ob) + tm * tn * 4 + tm * k * 2 + k * tn * 2

    tn_min = min(n, max(mult, SLAB))
    tm = _divisor(m, 1408, BF16_ROWS)
    while True:
        tn = _divisor(n, 1024 if m > 256 else 2048, mult)
        while est(tm, tn) > budget and tn > tn_min:
            tn = _divisor(n, tn - mult, mult)
        if est(tm, tn) <= budget or tm <= BF16_ROWS:
            return tm, tn, est(tm, tn)
        tm = _divisor(m, tm - BF16_ROWS, BF16_ROWS)


def _mm_rope_body(a_ref, w_ref, c_ref, s_ref, o_ref, *, rep, scale):
    a = a_ref[...].astype(BF16)
    acc = jnp.dot(a, w_ref[...].astype(BF16), preferred_element_type=F32)
    c, s = c_ref[...] * scale, s_ref[...] * scale
    for r in range(rep):
        x = acc[:, 2 * r * SLAB:(2 * r + 1) * SLAB]
        xs = acc[:, (2 * r + 1) * SLAB:(2 * r + 2) * SLAB]
        o_ref[:, r * SLAB:(r + 1) * SLAB] = (x * c + xs * s).astype(o_ref.dtype)


def _matmul(a, w, out_dtype, layer=None, slab=None, rope=None, name="matmul"):
    m, k = a.shape
    n = w.shape[-1]
    mult = 2 * SLAB if rope is not None else SLAB if slab is not None else LANES
    period = rope[0][0].shape[0] if rope is not None else 1
    tm, tn, est = _mm_tiles(m if period == 1 else period, k, n, a.dtype.itemsize, w.dtype.itemsize,
                            jnp.dtype(out_dtype).itemsize, mult, 46 << 20)
    n_out, tn_out = (n // 2, tn // 2) if rope is not None else (n, tn)
    if w.ndim == 3:
        w_spec = pl.BlockSpec((None, k, tn), lambda i, j: (layer, 0, j))
    else:
        w_spec = pl.BlockSpec((k, tn), lambda i, j: (0, j))
    in_specs = [pl.BlockSpec((tm, k), lambda i, j: (i, 0)), w_spec]
    args = [a, w]
    body = _mm_body
    if slab is not None:
        in_specs.append(pl.BlockSpec((tm, SLAB), lambda i, j: (i, 0)))
        args.append(slab)
        body = functools.partial(_mm_slab_body, rep=tn // SLAB)
    if rope is not None:
        tables, scale = rope
        period = tables[0].shape[0]
        if period == 1:
            tspec = pl.BlockSpec((1, SLAB), lambda i, j: (0, 0))
        else:
            assert period % tm == 0, (period, tm)
            nt = period // tm
            tspec = pl.BlockSpec((tm, SLAB), lambda i, j: (i % nt, 0))
        c, s1, s2 = tables
        in_specs += [tspec] * 2
        args += [c, s1 + s2]
        body = functools.partial(_mm_rope_body, rep=tn_out // SLAB, scale=scale)
        est += 4 * tm * SLAB * 4
    return pl.pallas_call(
        body,
        out_shape=jax.ShapeDtypeStruct((m, n_out), out_dtype),
        grid=(m // tm, n // tn),
        in_specs=in_specs,
        out_specs=pl.BlockSpec((tm, tn_out), lambda i, j: (i, j)),
        compiler_params=_cparams(("parallel", "parallel"), est),
        name=name,
    )(*args)


def _mm_resid_body(a_ref, w_ref, h_ref, gp_ref, gn_ref, ho_ref, xo_ref, acc_ref, *, nk):
    kk = pl.program_id(1)

    def prod():
        return jnp.dot(a_ref[...].astype(BF16), w_ref[...].astype(BF16), preferred_element_type=F32)

    @pl.when(kk == 0)
    def _():
        acc_ref[...] = prod()

    @pl.when(kk > 0)
    def _():
        acc_ref[...] += prod()

    @pl.when(kk == nk - 1)
    def _():
        h = h_ref[...] + _rms(acc_ref[...]) * gp_ref[...]
        ho_ref[...] = h
        xo_ref[...] = (_rms(h) * gn_ref[...]).astype(xo_ref.dtype)


def _mm_resid_copy_body(a_ref, w_ref, h_ref, gp_ref, gn_ref, ho_ref, xo_ref, wo_ref, acc_ref, *, nk):
    wo_ref[...] = w_ref[...].astype(wo_ref.dtype)
    _mm_resid_body(a_ref, w_ref, h_ref, gp_ref, gn_ref, ho_ref, xo_ref, acc_ref, nk=nk)


def _matmul_resid(a, w, layer, h, g_post, g_next, emit_bf16=False, name="matmul_resid"):
    m, k = a.shape
    n = w.shape[-1]
    ab, wb = a.dtype.itemsize, w.dtype.itemsize
    tk = _divisor(k, (6 << 20) // (n * wb), LANES)
    nk = k // tk

    def est(tm):
        casts = (tk * n * 2 if wb != 2 else 0) + (tm * tk * 2 if ab != 2 else 0) + (2 * tk * n * 2 if emit_bf16 else 0)
        return tm * n * 4 + 2 * (tm * tk * ab + tk * n * wb + 2 * tm * n * 4 + tm * n * 2) + casts

    tm = _divisor(m, 1408, BF16_ROWS)
    while est(tm) > (50 << 20) and tm > BF16_ROWS:
        tm = _divisor(m, tm - BF16_ROWS, BF16_ROWS)
    row = pl.BlockSpec((tm, n), lambda i, kk: (i, 0))
    vec = pl.BlockSpec((1, n), lambda i, kk: (0, 0))
    if w.ndim == 3:
        w_spec = pl.BlockSpec((None, tk, n), lambda i, kk: (layer, kk, 0))
    else:
        w_spec = pl.BlockSpec((tk, n), lambda i, kk: (kk, 0))
    out_shape = [jax.ShapeDtypeStruct((m, n), F32), jax.ShapeDtypeStruct((m, n), BF16)]
    out_specs = [row, row]
    body = _mm_resid_body
    if emit_bf16:
        assert m == tm, (m, tm)
        out_shape.append(jax.ShapeDtypeStruct((k, n), BF16))
        out_specs.append(pl.BlockSpec((tk, n), lambda i, kk: (kk, 0)))
        body = _mm_resid_copy_body
    return pl.pallas_call(
        functools.partial(body, nk=nk),
        out_shape=tuple(out_shape),
        grid=(m // tm, nk),
        in_specs=[pl.BlockSpec((tm, tk), lambda i, kk: (i, kk)), w_spec, row, vec, vec],
        out_specs=tuple(out_specs),
        scratch_shapes=[pltpu.VMEM((tm, n), F32)],
        compiler_params=_cparams(("parallel", "arbitrary"), est(tm)),
        name=name,
    )(a, w, h, g_post.reshape(1, n), g_next.reshape(1, n))


def _rope_tables(pos, d):
    inv = ROPE_BASE ** (-jnp.arange(0, d, 2, dtype=F32) / d)
    ang = pos.astype(F32)[:, None] * inv[None, :]
    return jnp.cos(ang), jnp.sin(ang)


def _slab_rope_tables(pos):
    cos, sin = _rope_tables(pos, MLA_ROPE)
    n, half = cos.shape
    one = jnp.ones((n, MLA_NOPE), F32)
    z_nope = jnp.zeros((n, MLA_NOPE), F32)
    z_half = jnp.zeros((n, half), F32)
    z_tail = jnp.zeros((n, SLAB - MLA_NOPE - MLA_ROPE), F32)
    c = jnp.concatenate([one, cos, cos, z_tail], axis=1)
    s1 = jnp.concatenate([z_nope, -sin, z_half, z_tail], axis=1)
    s2 = jnp.concatenate([z_nope, z_half, sin, z_tail], axis=1)
    return c, s1, s2


def _swapped_slabs(w):
    half = MLA_ROPE // 2
    nope, x1, x2 = w[..., :MLA_NOPE], w[..., MLA_NOPE:MLA_NOPE + half], w[..., MLA_NOPE + half:]
    z_tail = jnp.zeros(w.shape[:-1] + (SLAB - MLA_NOPE - MLA_ROPE,), w.dtype)
    return jnp.concatenate([nope, x1, x2, z_tail, jnp.zeros_like(nope), x2, x1, z_tail], axis=-1)


def _slab_rope(x, c, s1, s2):
    half = MLA_ROPE // 2
    return x * c + pltpu.roll(x, SLAB - half, 1) * s1 + pltpu.roll(x, half, 1) * s2


def _ret_rope(x, cos, sin):
    half = RET_DK // 2
    x1, x2 = x[:, :half], x[:, half:]
    return jnp.concatenate([x1 * cos - x2 * sin, x1 * sin + x2 * cos], axis=1)


def _ret_log_gamma():
    return jnp.log1p(-jnp.power(2.0, -5.0 - jnp.arange(RET_HEADS, dtype=F32)))


def _ret_prompt_body(lg_ref, q_ref, k_ref, v_ref, g_ref, cos_ref, sin_ref, o_ref, so_ref, st_ref, *, pad, n_chunks, hp):
    hg = pl.program_id(1)
    c = pl.program_id(2)
    chunk = q_ref.shape[0]

    @pl.when(c == 0)
    def _():
        st_ref[...] = jnp.zeros_like(st_ref)

    lead = jnp.where(c == 0, float(pad), 0.0)
    cos, sin = cos_ref[...], sin_ref[...]
    n_col = lax.broadcasted_iota(jnp.int32, (chunk, 1), 0).astype(F32)
    n_row = lax.broadcasted_iota(jnp.int32, (1, chunk), 1).astype(F32)
    diff = n_col - n_row
    for i in range(hp):
        lg = lg_ref[hg * hp + i]
        q = _ret_rope(q_ref[:, i * RET_DK:(i + 1) * RET_DK].astype(F32), cos, sin)
        k = _ret_rope(k_ref[:, i * RET_DK:(i + 1) * RET_DK].astype(F32), cos, sin) * (RET_DK ** -0.5)
        v = v_ref[:, i * RET_DV:(i + 1) * RET_DV]
        intra = jnp.where(diff >= 0, jnp.exp(lg * jnp.maximum(diff, 0.0)), 0.0)
        q_dec = jnp.exp(lg * (n_col + 1.0 - lead))
        k_dec = jnp.exp(lg * (chunk - 1.0 - n_col))
        s_dec = jnp.exp(jnp.full((1, 1), lg * (chunk - lead), F32))
        st = st_ref[i]
        scores = lax.dot_general(q.astype(BF16), k.astype(BF16), (((1,), (1,)), ((), ())),
                                 preferred_element_type=F32) * intra
        out = jnp.dot(scores.astype(BF16), v, preferred_element_type=F32)
        out = out + jnp.dot((q * q_dec).astype(BF16), st.astype(BF16), preferred_element_type=F32)
        kt = (k * k_dec).T.astype(BF16)
        st_ref[i] = st * s_dec + jnp.dot(kt, v, preferred_element_type=F32)
        g = g_ref[:, i * RET_DV:(i + 1) * RET_DV].astype(F32)
        o_ref[:, i * RET_DV:(i + 1) * RET_DV] = (_silu(g) * _rms(out)).astype(o_ref.dtype)

    @pl.when(c == n_chunks - 1)
    def _():
        so_ref[...] = st_ref[...]


def _ret_prompt(qkvg, cos, sin, pad):
    b, s, _ = qkvg.shape
    chunk = RET_CHUNK
    nc = s // chunk
    hp = _divisor(RET_HEADS, 8, 1)
    ng = RET_HEADS // hp
    body = functools.partial(_ret_prompt_body, pad=pad, n_chunks=nc, hp=hp)
    return pl.pallas_call(
        body,
        out_shape=(jax.ShapeDtypeStruct((b, s, RET_HEADS * RET_DV), BF16),
                   jax.ShapeDtypeStruct((b, RET_HEADS, RET_DK, RET_DV), F32)),
        grid=(b, ng, nc),
        in_specs=[
            pl.BlockSpec(memory_space=pltpu.SMEM),
            pl.BlockSpec((None, chunk, hp * RET_DK), lambda bi, h, c: (bi, c, h)),
            pl.BlockSpec((None, chunk, hp * RET_DK), lambda bi, h, c: (bi, c, ng + h)),
            pl.BlockSpec((None, chunk, hp * RET_DV), lambda bi, h, c: (bi, c, ng + h)),
            pl.BlockSpec((None, chunk, hp * RET_DV), lambda bi, h, c: (bi, c, 2 * ng + h)),
            pl.BlockSpec((chunk, RET_DK // 2), lambda bi, h, c: (c, 0)),
            pl.BlockSpec((chunk, RET_DK // 2), lambda bi, h, c: (c, 0)),
        ],
        out_specs=(
            pl.BlockSpec((None, chunk, hp * RET_DV), lambda bi, h, c: (bi, c, h)),
            pl.BlockSpec((None, hp, RET_DK, RET_DV), lambda bi, h, c: (bi, h, 0, 0)),
        ),
        scratch_shapes=[pltpu.VMEM((hp, RET_DK, RET_DV), F32)],
        compiler_params=_cparams(("parallel", "parallel", "arbitrary"), 16 << 20),
        name="ret_prompt",
    )(_ret_log_gamma(), qkvg, qkvg, qkvg, qkvg, cos, sin)


def _ret_sample_body(lg_ref, q_ref, k_ref, v_ref, g_ref, cos_ref, sin_ref, st_ref, o_ref, so_ref):
    h = pl.program_id(1)
    bt = q_ref.shape[0]
    gamma = jnp.exp(jnp.full((1, 1), lg_ref[h], F32))
    cos, sin = cos_ref[...], sin_ref[...]
    q = _ret_rope(q_ref[...], cos, sin)
    k = _ret_rope(k_ref[...], cos, sin) * (RET_DK ** -0.5)
    v = v_ref[...]
    qk = jnp.sum(q * k, axis=-1, keepdims=True)
    qg = (q * gamma).astype(BF16)
    eye = lax.broadcasted_iota(jnp.int32, (RET_DK, RET_DK), 0) == lax.broadcasted_iota(jnp.int32, (RET_DK, RET_DK), 1)
    rows = []
    for i in range(bt):
        st = st_ref[i]
        cross = jnp.dot(qg, st.astype(BF16), preferred_element_type=F32)
        rows.append(cross[i:i + 1])
        k_col = jnp.sum(jnp.where(eye, k[i:i + 1], 0.0), axis=1, keepdims=True)
        so_ref[i] = st * gamma + k_col * v[i:i + 1]
    out = qk * v + jnp.concatenate(rows, axis=0)
    o_ref[...] = _silu(g_ref[...]) * _rms(out)


def _ret_sample(qkvg, state, cos, sin, out_buf, layer):
    db = qkvg.shape[0]
    bt = SUBLANES
    kq = RET_HEADS
    n_layers = state.shape[0]
    st_block = (None, bt, None, RET_DK, RET_DV)
    in_specs = [
        pl.BlockSpec(memory_space=pltpu.SMEM),
        pl.BlockSpec((bt, RET_DK), lambda bi, h: (bi, h)),
        pl.BlockSpec((bt, RET_DK), lambda bi, h: (bi, kq + h)),
        pl.BlockSpec((bt, RET_DV), lambda bi, h: (bi, kq + h)),
        pl.BlockSpec((bt, RET_DV), lambda bi, h: (bi, 2 * kq + h)),
        pl.BlockSpec((1, RET_DK // 2), lambda bi, h: (0, 0)),
        pl.BlockSpec((1, RET_DK // 2), lambda bi, h: (0, 0)),
        pl.BlockSpec(st_block, lambda bi, h: (layer, bi, h, 0, 0)),
    ]
    args = [_ret_log_gamma(), qkvg, qkvg, qkvg, qkvg, cos, sin, state]
    aliases = {}
    body = _ret_sample_body
    if out_buf is not None:
        in_specs.append(pl.BlockSpec(memory_space=pl.ANY))
        args.append(out_buf)
        aliases = {len(args) - 1: 1}
        body = lambda *refs: _ret_sample_body(*refs[:8], *refs[9:])
    return pl.pallas_call(
        body,
        out_shape=(jax.ShapeDtypeStruct((db, RET_HEADS * RET_DV), F32),
                   jax.ShapeDtypeStruct((n_layers, db, RET_HEADS, RET_DK, RET_DV), F32)),
        grid=(db // bt, RET_HEADS),
        in_specs=in_specs,
        out_specs=(
            pl.BlockSpec((bt, RET_DV), lambda bi, h: (bi, h)),
            pl.BlockSpec(st_block, lambda bi, h: (layer, bi, h, 0, 0)),
        ),
        input_output_aliases=aliases,
        compiler_params=_cparams(("parallel", "parallel"), 4 * bt * RET_DK * RET_DV * 4 + (4 << 20)),
        name="ret_sample",
    )(*args)


def _conv_act(g2, g1, g0, up, cw_ref, cb_ref):
    conv = g2 * cw_ref[0:1, :] + g1 * cw_ref[1:2, :] + g0 * cw_ref[2:3, :] + cb_ref[...]
    return _silu(conv) * up


def _ffn1_seq_body(x_ref, wg_ref, wu_ref, cw_ref, cb_ref, act_ref, cs_ref, carry_ref):
    i = pl.program_id(1)
    j = pl.program_id(2)
    x = x_ref[...]
    gate = jnp.dot(x, wg_ref[...].astype(BF16), preferred_element_type=F32)
    up = jnp.dot(x, wu_ref[...].astype(BF16), preferred_element_type=F32)
    tm = gate.shape[0]
    head = 2 * SUBLANES
    act = _conv_act(pltpu.roll(gate, 2, 0), pltpu.roll(gate, 1, 0), gate, up, cw_ref, cb_ref)
    act_ref[head:, :] = act[head:].astype(act_ref.dtype)
    @pl.when(i == 0)
    def _():
        carry_ref[j] = jnp.zeros(carry_ref.shape[1:], F32)

    win = jnp.concatenate([carry_ref[j], gate[:head]], axis=0)
    w1 = pltpu.roll(win, 1, 0)[SUBLANES:]
    w2 = pltpu.roll(win, 2, 0)[SUBLANES:]
    act_ref[:head, :] = _conv_act(w2, w1, gate[:head], up[:head], cw_ref, cb_ref).astype(act_ref.dtype)
    tail = gate[tm - SUBLANES:]
    carry_ref[j] = tail
    cs_ref[...] = tail


def _ffn1_seq(x, w_gate, w_up, conv_w, conv_b, layer, n_seq):
    m, k = x.shape
    f = w_gate.shape[-1]
    s = m // n_seq
    tm = _divisor(s, 1408, BF16_ROWS)
    tn = _divisor(f, 512, LANES)
    ni, nj = s // tm, f // tn
    est = 2 * (tm * k * 2 + 2 * k * tn * 4 + tm * tn * 2) + 2 * k * tn * 2 + 4 * tm * tn * 4
    act, tails = pl.pallas_call(
        _ffn1_seq_body,
        out_shape=(jax.ShapeDtypeStruct((m, f), BF16), jax.ShapeDtypeStruct((n_seq, ni, SUBLANES, f), F32)),
        grid=(n_seq, ni, nj),
        in_specs=[
            pl.BlockSpec((tm, k), lambda b, i, j: (b * ni + i, 0)),
            pl.BlockSpec((None, k, tn), lambda b, i, j: (layer, 0, j)),
            pl.BlockSpec((None, k, tn), lambda b, i, j: (layer, 0, j)),
            pl.BlockSpec((None, CONV_W, tn), lambda b, i, j: (layer, 0, j)),
            pl.BlockSpec((None, 1, tn), lambda b, i, j: (layer, 0, j)),
        ],
        out_specs=(
            pl.BlockSpec((tm, tn), lambda b, i, j: (b * ni + i, j)),
            pl.BlockSpec((None, None, SUBLANES, tn), lambda b, i, j: (b, i, 0, j)),
        ),
        scratch_shapes=[pltpu.VMEM((nj, SUBLANES, tn), F32)],
        compiler_params=_cparams(("arbitrary", "arbitrary", "arbitrary"), est),
        name="ffn1_seq",
    )(x, w_gate, w_up, conv_w, conv_b.reshape(conv_b.shape[0], 1, f))
    return act, tails[:, ni - 1]


def _ffn1_tok_body(x_ref, wg_ref, wu_ref, cw_ref, cb_ref, s0_ref, s1_ref, act_ref, gate_ref):
    x = x_ref[...]
    gate = jnp.dot(x, wg_ref[...].astype(BF16), preferred_element_type=F32)
    up = jnp.dot(x, wu_ref[...].astype(BF16), preferred_element_type=F32)
    act_ref[...] = _conv_act(s0_ref[...], s1_ref[...], gate, up, cw_ref, cb_ref).astype(act_ref.dtype)
    gate_ref[...] = gate


def _ffn1_tok(x, w_gate, w_up, conv_w, conv_b, layer, conv_state):
    m, k = x.shape
    f = w_gate.shape[-1]
    tn = _divisor(f, 1024, LANES)
    nj = f // tn
    cs = conv_state.reshape(m, (CONV_W - 1) * f)
    est = 2 * (m * k * 2 + 2 * k * tn * 4 + 6 * m * tn * 4) + 2 * k * tn * 2
    return pl.pallas_call(
        _ffn1_tok_body,
        out_shape=(jax.ShapeDtypeStruct((m, f), BF16), jax.ShapeDtypeStruct((m, f), F32)),
        grid=(nj,),
        in_specs=[
            pl.BlockSpec((m, k), lambda j: (0, 0)),
            pl.BlockSpec((None, k, tn), lambda j: (layer, 0, j)),
            pl.BlockSpec((None, k, tn), lambda j: (layer, 0, j)),
            pl.BlockSpec((None, CONV_W, tn), lambda j: (layer, 0, j)),
            pl.BlockSpec((None, 1, tn), lambda j: (layer, 0, j)),
            pl.BlockSpec((m, tn), lambda j: (0, j)),
            pl.BlockSpec((m, tn), lambda j: (0, nj + j)),
        ],
        out_specs=(pl.BlockSpec((m, tn), lambda j: (0, j)), pl.BlockSpec((m, tn), lambda j: (0, j))),
        compiler_params=_cparams(("parallel",), est),
        name="ffn1_tok",
    )(x, w_gate, w_up, conv_w, conv_b.reshape(conv_b.shape[0], 1, f), cs, cs)


def _kv_post_body(y_ref, g_ref, c_ref, s1_ref, s2_ref, cf_ref, cb_ref, r_ref):
    y = y_ref[...]
    c = _rms(y[:, :KV_LORA]) * g_ref[...]
    cf_ref[...] = c
    cb_ref[...] = c.astype(cb_ref.dtype)
    r_ref[...] = _slab_rope(y[:, KV_LORA:], c_ref[...], s1_ref[...], s2_ref[...])


def _kv_post(y, g, tables):
    m = y.shape[0]
    t_rows = tables[0].shape[0]
    tm = _divisor(m if t_rows == 1 else t_rows, 512, BF16_ROWS)
    if t_rows == 1:
        tspec = pl.BlockSpec((1, SLAB), lambda i: (0, 0))
    else:
        nt = t_rows // tm
        tspec = pl.BlockSpec((tm, SLAB), lambda i: (i % nt, 0))
    return pl.pallas_call(
        _kv_post_body,
        out_shape=(jax.ShapeDtypeStruct((m, KV_LORA), F32), jax.ShapeDtypeStruct((m, KV_LORA), BF16),
                   jax.ShapeDtypeStruct((m, SLAB), F32)),
        grid=(m // tm,),
        in_specs=[pl.BlockSpec((tm, KV_LORA + SLAB), lambda i: (i, 0)), pl.BlockSpec((1, KV_LORA), lambda i: (0, 0)),
                  tspec, tspec, tspec],
        out_specs=(pl.BlockSpec((tm, KV_LORA), lambda i: (i, 0)), pl.BlockSpec((tm, KV_LORA), lambda i: (i, 0)),
                   pl.BlockSpec((tm, SLAB), lambda i: (i, 0))),
        compiler_params=_cparams(("parallel",), 16 << 20),
        name="kv_post",
    )(y, g.reshape(1, KV_LORA), *tables)


ATTN_BLOCK = 3 * LANES


def _v_up_t_body(c_ref, w_ref, o_ref):
    o_ref[...] = lax.dot_general(w_ref[...].astype(BF16), c_ref[...], (((1,), (1,)), ((), ())),
                                 preferred_element_type=F32).astype(o_ref.dtype)


def _v_up_t(c, w_uv_t, n_seq, tk):
    m, kl = c.shape
    n = w_uv_t.shape[0]
    nk = m // n_seq // tk
    tn = _divisor(n, 1024, LANES)
    return pl.pallas_call(
        _v_up_t_body,
        out_shape=jax.ShapeDtypeStruct((n_seq, nk, n, tk), BF16),
        grid=(n_seq, nk, n // tn),
        in_specs=[pl.BlockSpec((tk, kl), lambda b, kb, j: (b * nk + kb, 0)), pl.BlockSpec((tn, kl), lambda b, kb, j: (j, 0))],
        out_specs=pl.BlockSpec((None, None, tn, tk), lambda b, kb, j: (b, kb, j, 0)),
        compiler_params=_cparams(("parallel", "parallel", "parallel"), 16 << 20),
        name="v_up_t",
    )(c, w_uv_t)


def _attn_prompt_body(q_ref, k_ref, vt_ref, bias_ref, o_ref, m_ref, l_ref, acc_ref, s_ref, *, pad, blk, hp):
    qi = pl.program_id(2)
    m_ref[...] = jnp.full_like(m_ref, NEG)
    l_ref[...] = jnp.zeros_like(l_ref)
    acc_ref[...] = jnp.zeros_like(acc_ref)
    nt = (((1,), (1,)), ((), ()))

    def scores(ki, slot):
        start = ki * blk if isinstance(ki, int) else pl.multiple_of(ki * blk, blk)
        for i in range(hp):
            kb = k_ref[pl.ds(start, blk), i * SLAB:(i + 1) * SLAB]
            s_ref[slot, i] = lax.dot_general(kb, q_ref[:, i * SLAB:(i + 1) * SLAB], nt, preferred_element_type=F32)

    def consume(ki, slot, kind):
        for i in range(hp):
            s = s_ref[slot, i]
            if kind == "first":
                s = s[pad:]
            elif kind == "diag":
                s = s + bias_ref[jnp.where(qi == 0, 0, 1)]
            m_old = m_ref[i]
            m_new = jnp.maximum(m_old, jnp.max(s, axis=0, keepdims=True))
            alpha = jnp.exp(m_old - m_new)
            p = jnp.exp(s - m_new)
            l_ref[i] = alpha * l_ref[i] + jnp.sum(p, axis=0, keepdims=True)
            pb = p.astype(BF16)
            if kind == "first":
                pb = jnp.concatenate([jnp.zeros((pad, blk), BF16), pb], axis=0)
            vt = vt_ref[ki, i * MLA_V:(i + 1) * MLA_V, :]
            acc_ref[i] = alpha * acc_ref[i] + jnp.dot(vt, pb, preferred_element_type=F32)
            m_ref[i] = m_new

    scores(0, 0)

    @pl.when(qi > 0)
    def _():
        scores(1, 1)
        consume(0, 0, "first")

    def mid(ki, carry):
        slot = lax.rem(ki, 2)
        consume(ki, slot, "full")
        scores(ki + 1, 1 - slot)
        return carry

    lax.fori_loop(1, qi, mid, 0)
    consume(qi, lax.rem(qi, 2), "diag")

    for i in range(hp):
        o_ref[:, i * MLA_V:(i + 1) * MLA_V] = (acc_ref[i] / l_ref[i]).T.astype(o_ref.dtype)


def _attn_bias(blk, pad):
    kpos = jnp.arange(blk)[:, None]
    qpos = jnp.arange(blk)[None, :]
    causal = kpos <= qpos
    first = causal & ((kpos >= pad) | (qpos < pad))
    return jnp.stack([jnp.where(first, 0.0, NEG), jnp.where(causal, 0.0, NEG)]).astype(F32)


def _attn_prompt(q, k, vt, pad):
    b, s, _ = q.shape
    nk, blk = vt.shape[1], vt.shape[3]
    hp = _divisor(MLA_HEADS, 4, 1)
    body = functools.partial(_attn_prompt_body, pad=pad, blk=blk, hp=hp)
    est = (2 * (blk * hp * SLAB * 2 + s * hp * SLAB * 2 + s * hp * MLA_V * 2 + 2 * blk * blk * 4 + blk * hp * MLA_V * 2)
           + 8 * hp * blk * blk * 4)
    return pl.pallas_call(
        body,
        out_shape=jax.ShapeDtypeStruct((b, s, MLA_HEADS * MLA_V), BF16),
        grid=(b, MLA_HEADS // hp, s // blk),
        in_specs=[
            pl.BlockSpec((None, blk, hp * SLAB), lambda bi, h, qi: (bi, qi, h)),
            pl.BlockSpec((None, s, hp * SLAB), lambda bi, h, qi: (bi, 0, h)),
            pl.BlockSpec((None, nk, hp * MLA_V, blk), lambda bi, h, qi: (bi, 0, h, 0)),
            pl.BlockSpec((2, blk, blk), lambda bi, h, qi: (0, 0, 0)),
        ],
        out_specs=pl.BlockSpec((None, blk, hp * MLA_V), lambda bi, h, qi: (bi, qi, h)),
        scratch_shapes=[pltpu.VMEM((hp, 1, blk), F32), pltpu.VMEM((hp, 1, blk), F32), pltpu.VMEM((hp, MLA_V, blk), F32),
                        pltpu.VMEM((2, hp, blk, blk), F32)],
        compiler_params=_cparams(("parallel", "parallel", "arbitrary"), est),
        name="attn_prompt",
    )(q, k, vt, _attn_bias(blk, pad))


def _q_absorb_body(q_ref, w_ref, ql_ref):
    ql_ref[...] = lax.dot_general(q_ref[...][:, :MLA_NOPE].astype(BF16), w_ref[...].astype(BF16), (((1,), (1,)), ((), ())),
                                  preferred_element_type=F32)


def _q_absorb(q, w_uk2):
    db = q.shape[0]
    return pl.pallas_call(
        _q_absorb_body,
        out_shape=jax.ShapeDtypeStruct((db, MLA_HEADS * KV_LORA), F32),
        grid=(MLA_HEADS,),
        in_specs=[pl.BlockSpec((db, SLAB), lambda h: (0, h)), pl.BlockSpec((KV_LORA, MLA_NOPE), lambda h: (0, h))],
        out_specs=pl.BlockSpec((db, KV_LORA), lambda h: (0, h)),
        compiler_params=_cparams(("parallel",), 8 << 20),
        name="q_absorb",
    )(q, w_uk2)


def _page_copies(pt_ref, lat_hbm, rope_hbm, latf_ref, ropef_ref, sem_ref, bi, g, slot, *, nb, pages):
    page = lat_hbm.shape[1]
    out = []
    for b in range(nb):
        for i in range(pages):
            pid = pt_ref[bi, g, b * pages + i]
            rows = pl.ds(i * page, page)
            out.append(pltpu.make_async_copy(lat_hbm.at[pid], latf_ref.at[slot, b, rows, :], sem_ref.at[slot]))
            out.append(pltpu.make_async_copy(rope_hbm.at[pid], ropef_ref.at[slot, b, :, rows], sem_ref.at[slot]))
    return out


def _attn_sample_body(pt_ref, ql_ref, qs_ref, cn_ref, rn_ref, lat_hbm, rope_hbm, o_ref, m_ref, l_ref, acc_ref, latb_ref,
                      ropeb_ref, latf_ref, ropef_ref, sem_ref, *, pages, n_groups, nb, n_steps):
    bi = pl.program_id(0)
    g = pl.program_id(1)
    step = bi * n_groups + g
    slot = lax.rem(step, 2)
    copies = functools.partial(_page_copies, pt_ref, lat_hbm, rope_hbm, latf_ref, ropef_ref, sem_ref, nb=nb, pages=pages)

    @pl.when(step == 0)
    def _():
        for c in copies(bi, g, slot):
            c.start()

    @pl.when(step + 1 < n_steps)
    def _():
        nxt = step + 1
        for c in copies(lax.div(nxt, n_groups), lax.rem(nxt, n_groups), 1 - slot):
            c.start()

    for c in copies(bi, g, slot):
        c.wait()

    @pl.when(g == 0)
    def _():
        m_ref[...] = jnp.full_like(m_ref, NEG)
        l_ref[...] = jnp.zeros_like(l_ref)
        acc_ref[...] = jnp.zeros_like(acc_ref)

    nt = (((1,), (1,)), ((), ()))
    for b in range(nb):
        latb_ref[b] = latf_ref[slot, b].astype(BF16)
        ropeb_ref[b] = ropef_ref[slot, b].astype(BF16)
        ql = ql_ref[b]
        qr = qs_ref[b][:, MLA_NOPE:MLA_NOPE + MLA_ROPE]
        s = lax.dot_general(ql.astype(BF16), latb_ref[b], nt, preferred_element_type=F32)
        s = s + jnp.dot(qr.astype(BF16), ropeb_ref[b], preferred_element_type=F32)
        m_old = m_ref[b]
        m_new = jnp.maximum(m_old, jnp.max(s, axis=-1, keepdims=True))
        alpha = jnp.exp(m_old - m_new)
        p = jnp.exp(s - m_new)
        l_ref[b] = alpha * l_ref[b] + jnp.sum(p, axis=-1, keepdims=True)
        acc_ref[b] = alpha * acc_ref[b] + jnp.dot(p.astype(BF16), latb_ref[b], preferred_element_type=F32)
        m_ref[b] = m_new

    @pl.when(g == n_groups - 1)
    def _():
        for b in range(nb):
            ql = ql_ref[b]
            qr = qs_ref[b][:, MLA_NOPE:MLA_NOPE + MLA_ROPE]
            cn = cn_ref[b]
            rn = rn_ref[b][:, MLA_NOPE:MLA_NOPE + MLA_ROPE]
            s_new = jnp.sum(ql * cn, axis=-1, keepdims=True) + jnp.sum(qr * rn, axis=-1, keepdims=True)
            m_old = m_ref[b]
            m_new = jnp.maximum(m_old, s_new)
            alpha = jnp.exp(m_old - m_new)
            p_new = jnp.exp(s_new - m_new)
            l = alpha * l_ref[b] + p_new
            o_ref[b] = (alpha * acc_ref[b] + p_new * cn) / l


def _attn_sample(page_table, cache_lat, cache_rope_t, q_lat, q_slab, c_new, r_new):
    db, n_pages = page_table.shape
    page = cache_lat.shape[1]
    pages = _divisor(n_pages, 16, 1)
    nb = _divisor(db, 4, 1)
    n_groups = n_pages // pages
    n_steps = (db // nb) * n_groups
    body = functools.partial(_attn_sample_body, pages=pages, n_groups=n_groups, nb=nb, n_steps=n_steps)

    page_table = page_table.reshape(db // nb, nb, n_groups, pages).transpose(0, 2, 1, 3).reshape(db // nb, n_groups, nb * pages)

    def seq_spec(rows, width):
        return pl.BlockSpec((nb, rows, width), lambda bi, g, pt: (bi, 0, 0))

    hbm = pl.BlockSpec(memory_space=pl.ANY)
    keys = pages * page
    est = nb * keys * (KV_LORA + MLA_ROPE) * (2 * 4 + 2) + (2 << 20)
    return pl.pallas_call(
        body,
        out_shape=jax.ShapeDtypeStruct((db, MLA_HEADS, KV_LORA), F32),
        grid_spec=pltpu.PrefetchScalarGridSpec(
            num_scalar_prefetch=1,
            grid=(db // nb, n_groups),
            in_specs=[seq_spec(MLA_HEADS, KV_LORA), seq_spec(MLA_HEADS, SLAB), seq_spec(1, KV_LORA), seq_spec(1, SLAB),
                      hbm, hbm],
            out_specs=seq_spec(MLA_HEADS, KV_LORA),
            scratch_shapes=[
                pltpu.VMEM((nb, MLA_HEADS, 1), F32), pltpu.VMEM((nb, MLA_HEADS, 1), F32),
                pltpu.VMEM((nb, MLA_HEADS, KV_LORA), F32),
                pltpu.VMEM((nb, keys, KV_LORA), BF16), pltpu.VMEM((nb, MLA_ROPE, keys), BF16),
                pltpu.VMEM((2, nb, keys, KV_LORA), F32), pltpu.VMEM((2, nb, MLA_ROPE, keys), F32),
                pltpu.SemaphoreType.DMA((2,)),
            ],
        ),
        compiler_params=_cparams(("arbitrary", "arbitrary"), est),
        name="attn_sample",
    )(page_table, q_lat, q_slab, c_new, r_new, cache_lat, cache_rope_t)


def _head_out_body(o_ref, w_ref, y_ref):
    y_ref[...] = jnp.dot(o_ref[...].astype(BF16), w_ref[...].astype(BF16), preferred_element_type=F32).astype(y_ref.dtype)


def _head_out(o_lat, w_uv2):
    db = o_lat.shape[0]
    return pl.pallas_call(
        _head_out_body,
        out_shape=jax.ShapeDtypeStruct((db, MLA_HEADS * MLA_V), BF16),
        grid=(MLA_HEADS,),
        in_specs=[pl.BlockSpec((db, KV_LORA), lambda h: (0, h)), pl.BlockSpec((KV_LORA, MLA_V), lambda h: (0, h))],
        out_specs=pl.BlockSpec((db, MLA_V), lambda h: (0, h)),
        compiler_params=_cparams(("parallel",), 4 << 20),
        name="head_out",
    )(o_lat, w_uv2)


def _prep_weights(ret_w_qkvg, ret_w_o, ffn_w_gate, ffn_w_up, ffn_w_down, w_dkv, w_kr, w_uk, w_uv, mla_w_dq, mla_w_uq,
                  mla_w_o):
    tail = SLAB - MLA_NOPE - MLA_ROPE
    w_kv = jnp.concatenate([w_dkv, jnp.zeros((D_MODEL, MLA_NOPE), F32), w_kr, jnp.zeros((D_MODEL, tail), F32)], axis=1)
    w_uk_ext = jnp.pad(w_uk, ((0, 0), (0, 0), (0, SLAB - MLA_NOPE))).reshape(KV_LORA, MLA_HEADS * SLAB)
    n_b = mla_w_uq.shape[0]
    w_uq_ext = _swapped_slabs(mla_w_uq.reshape(n_b, Q_LORA, MLA_HEADS, MLA_NOPE + MLA_ROPE)).reshape(n_b, Q_LORA, -1)
    w_uv2 = w_uv.reshape(KV_LORA, MLA_HEADS * MLA_V)
    return dict(
        qkvg=ret_w_qkvg, ret_o=ret_w_o, gate=ffn_w_gate, up=ffn_w_up, down=ffn_w_down, kv=w_kv, uk_ext=w_uk_ext,
        uk2=w_uk.reshape(KV_LORA, MLA_HEADS * MLA_NOPE), uv2=w_uv2, uv_t=w_uv2.T, dq=mla_w_dq, uq_ext=w_uq_ext,
        mla_o=mla_w_o,
    )


def _trunk(h, xn, w, w16, norm_g, ffn_conv_w, ffn_conv_b, kv_in_g, kv_norm_g, mla_q_norm_g, ret_mixer, ffn1, kv_tables,
           attend):
    conv_states = []
    c_f32 = r_slab = kv_ctx = None
    q_scale = (MLA_NOPE + MLA_ROPE) ** -0.5

    def resid(key, idx, a, h, g_post, g_next):
        if (key, idx) in w16:
            return _matmul_resid(a, w16[key, idx], None, h, g_post, g_next, name=key)
        h, xn, w16[key, idx] = _matmul_resid(a, w[key], idx, h, g_post, g_next, emit_bf16=True, name=key)
        return h, xn

    for layer in range(DEPTH):
        g = norm_g[layer]
        if layer < N_A_LAYERS:
            qkvg = _matmul(xn, w["qkvg"], ret_mixer.qkvg_dtype, layer=layer, name="qkvg")
            gated = ret_mixer(layer, qkvg)
            h, xn = resid("ret_o", layer, gated, h, g[1], g[2])
        else:
            j = layer - N_A_LAYERS
            cq = _matmul(xn, w["dq"], F32, layer=j, name="dq")
            cqn = _rms_cast(cq, mla_q_norm_g[j])
            q = _matmul(cqn, w["uq_ext"], attend.q_dtype, layer=j, rope=(kv_tables, q_scale), name="uq")
            o = attend(q, kv_ctx)
            h, xn = resid("mla_o", j, o, h, g[1], g[2])
        act, cs = ffn1(layer, xn, w["gate"], w["up"], ffn_conv_w, ffn_conv_b)
        conv_states.append(cs)
        g_next = norm_g[layer + 1, 0] if layer + 1 < DEPTH else g[3]
        h, xn = resid("down", layer, act, h, g[3], g_next)
        if layer == N_A_LAYERS - 1:
            hn = _rms_cast(h, kv_in_g)
            y = _matmul(hn, w["kv"], F32, name="kv_down")
            c_f32, c_bf16, r_slab = _kv_post(y, kv_norm_g, kv_tables)
            kv_ctx = (c_f32, c_bf16, r_slab)
    return h, conv_states, c_f32, r_slab


class _PromptRetention:
    qkvg_dtype = BF16

    def __init__(self, n_seq, pad, cos, sin):
        self.n_seq, self.pad, self.cos, self.sin = n_seq, pad, cos, sin
        self.states = []

    def __call__(self, layer, qkvg):
        m, n = qkvg.shape
        gated, state = _ret_prompt(qkvg.reshape(self.n_seq, m // self.n_seq, n), self.cos, self.sin, self.pad)
        self.states.append(state)
        return gated.reshape(m, -1)


class _SampleRetention:
    qkvg_dtype = F32

    def __init__(self, state, cos, sin):
        self.state, self.cos, self.sin = state, cos, sin
        self.out = None

    def __call__(self, layer, qkvg):
        gated, self.out = _ret_sample(qkvg, self.state, self.cos, self.sin, self.out, layer)
        return gated


def kernel(x_prompt, x_sample, state_retention, state_conv, cache_kv_latent, cache_k_rope, page_table, meta_tokens, norm_g,
           ret_w_qkvg, ret_w_o, ffn_w_gate, ffn_w_up, ffn_w_down, ffn_conv_w, ffn_conv_b, kv_in_g, w_dkv, kv_norm_g, w_kr,
           w_uk, w_uv, mla_w_dq, mla_q_norm_g, mla_w_uq, mla_w_o):
    w = _prep_weights(ret_w_qkvg, ret_w_o, ffn_w_gate, ffn_w_up, ffn_w_down, w_dkv, w_kr, w_uk, w_uv, mla_w_dq, mla_w_uq,
                      mla_w_o)
    shared = (norm_g, ffn_conv_w, ffn_conv_b, kv_in_g, kv_norm_g, mla_q_norm_g)

    b, seq, d = x_prompt.shape
    pad = RET_CHUNK - N_META
    s_pad = pad + N_META + seq
    h0, xn0 = _embed_norm(x_prompt, meta_tokens, norm_g[0, 0], pad)
    pos_p = jnp.arange(s_pad) - pad
    cos_p, sin_p = _rope_tables(pos_p, RET_DK)
    tables_p = _slab_rope_tables(pos_p)
    ret_p = _PromptRetention(b, pad, cos_p, sin_p)

    def ffn1_p(layer, xn, wg, wu, cw, cb):
        act, tail = _ffn1_seq(xn, wg, wu, cw, cb, layer, b)
        return act, tail[:, SUBLANES - (CONV_W - 1):, :]

    def attend_p(q, kv_ctx):
        if "k" not in attend_p.cache:
            _, c_bf16, r_slab = kv_ctx
            attend_p.cache["k"] = _matmul(c_bf16, w["uk_ext"], BF16, slab=r_slab, name="k_up").reshape(b, s_pad, -1)
            attend_p.cache["vt"] = _v_up_t(c_bf16, w["uv_t"], b, _divisor(s_pad, ATTN_BLOCK, LANES))
        o = _attn_prompt(q.reshape(b, s_pad, -1), attend_p.cache["k"], attend_p.cache["vt"], pad)
        return o.reshape(b * s_pad, -1)

    attend_p.cache = {}
    attend_p.q_dtype = BF16

    db = x_sample.shape[0]
    pos_s = jnp.full((1,), PAST_LEN)
    cos_s, sin_s = _rope_tables(pos_s, RET_DK)
    tables_s = _slab_rope_tables(pos_s)
    ret_s = _SampleRetention(state_retention, cos_s, sin_s)
    cache_rope_t = jnp.swapaxes(cache_k_rope, 1, 2)

    def ffn1_s(layer, xn, wg, wu, cw, cb):
        act, gate = _ffn1_tok(xn, wg, wu, cw, cb, layer, state_conv[layer])
        return act, jnp.stack([state_conv[layer][:, 1], gate], axis=1)

    def attend_s(q, kv_ctx):
        c_f32, _, r_slab = kv_ctx
        q_lat = _q_absorb(q, w["uk2"])
        o_lat = _attn_sample(page_table, cache_kv_latent, cache_rope_t, q_lat.reshape(db, MLA_HEADS, KV_LORA),
                             q.reshape(db, MLA_HEADS, SLAB), c_f32.reshape(db, 1, KV_LORA), r_slab.reshape(db, 1, SLAB))
        return _head_out(o_lat.reshape(db, MLA_HEADS * KV_LORA), w["uv2"])

    attend_s.q_dtype = F32

    w16 = {}
    hs0 = x_sample.reshape(db, d)
    h_s, conv_s, lat_s, rslab_s = _trunk(hs0, _rms_cast(hs0, norm_g[0, 0]), w, w16, *shared, ret_s, ffn1_s, tables_s,
                                         attend_s)
    h_p, conv_p, lat_p, rslab_p = _trunk(h0, xn0, w, w16, *shared, ret_p, ffn1_p, tables_p, attend_p)
    y_prompt = h_p.reshape(b, s_pad, d)[:, pad + N_META:]
    lat_prompt = lat_p.reshape(b, s_pad, KV_LORA)[:, pad:]
    rope_prompt = rslab_p.reshape(b, s_pad, SLAB)[:, pad:, MLA_NOPE:MLA_NOPE + MLA_ROPE]

    return (
        y_prompt,
        h_s.reshape(db, 1, d),
        jnp.stack(ret_p.states),
        ret_s.out,
        jnp.stack(conv_p),
        jnp.stack(conv_s),
        lat_prompt,
        lat_s.reshape(db, 1, KV_LORA),
        rope_prompt,
        rslab_s[:, MLA_NOPE:MLA_NOPE + MLA_ROPE].reshape(db, 1, MLA_ROPE),
    )
```

```python
import functools

import jax
import jax.numpy as jnp
from jax import lax
from jax.experimental import pallas as pl
from jax.experimental.pallas import tpu as pltpu

D_MODEL = 2048
SEQ = 4096
DEPTH = 4
PAST_LEN = 8192
PAGE_SIZE = 128
N_META = 16
N_A_LAYERS = DEPTH // 2
RET_HEADS = 8
RET_DK = D_MODEL // RET_HEADS
RET_DV = 2 * D_MODEL // RET_HEADS
RET_CHUNK = 128
MLA_HEADS = D_MODEL // 128
MLA_NOPE = 128
MLA_ROPE = 64
MLA_V = 128
Q_LORA = 512
KV_LORA = 512
D_FF = 11 * D_MODEL // 4
CONV_W = 3
ROPE_BASE = 10000.0
EPS = 1e-6

V7X_VMEM_BYTES = 64 * 1024 * 1024
VMEM_REQUEST_CAP = V7X_VMEM_BYTES - 8 * 1024 * 1024
LANES = 128
SUBLANES = 8
BF16_ROWS = 16
SLAB = 2 * LANES
NEG = -1e30

F32 = jnp.float32
BF16 = jnp.bfloat16


def _cparams(semantics, vmem_bytes):
    limit = int(min(max(vmem_bytes + (6 << 20), 16 << 20), VMEM_REQUEST_CAP))
    return pltpu.CompilerParams(dimension_semantics=semantics, vmem_limit_bytes=limit)


def _divisor(n, cap, mult):
    d = (min(n, cap) // mult) * mult
    while d >= mult:
        if n % d == 0:
            return d
        d -= mult
    return n


def _silu(x):
    return x * (1.0 / (1.0 + jnp.exp(-x)))


def _rms(x):
    return x * lax.rsqrt(jnp.mean(x * x, axis=-1, keepdims=True) + EPS)


def _rms_cast_body(x_ref, g_ref, o_ref):
    o_ref[...] = (_rms(x_ref[...]) * g_ref[...]).astype(o_ref.dtype)


def _rms_cast(x, g, out_dtype=BF16):
    m, d = x.shape
    tm = _divisor(m, 512, BF16_ROWS)
    return pl.pallas_call(
        _rms_cast_body,
        out_shape=jax.ShapeDtypeStruct((m, d), out_dtype),
        grid=(m // tm,),
        in_specs=[pl.BlockSpec((tm, d), lambda i: (i, 0)), pl.BlockSpec((1, d), lambda i: (0, 0))],
        out_specs=pl.BlockSpec((tm, d), lambda i: (i, 0)),
        compiler_params=_cparams(("parallel",), 2 * tm * d * 6 + tm * d * 8),
        name="rms_cast",
    )(x, g.reshape(1, d))


def _embed_norm_body(x_ref, meta_ref, g_ref, h_ref, xo_ref, *, pad):
    def emit(rows):
        h_ref[...] = rows
        xo_ref[...] = (_rms(rows) * g_ref[...]).astype(xo_ref.dtype)

    @pl.when(pl.program_id(1) == 0)
    def _():
        emit(jnp.concatenate([jnp.zeros((pad, meta_ref.shape[1]), F32), meta_ref[...]], axis=0))

    @pl.when(pl.program_id(1) > 0)
    def _():
        emit(x_ref[...])


def _embed_norm(x, meta, g, pad):
    b, seq, d = x.shape
    blk = pad + meta.shape[0]
    assert seq % blk == 0, (seq, blk)
    nblk = seq // blk + 1
    out = pl.BlockSpec((blk, d), lambda bi, i: (bi * nblk + i, 0))
    return pl.pallas_call(
        functools.partial(_embed_norm_body, pad=pad),
        out_shape=(jax.ShapeDtypeStruct((b * nblk * blk, d), F32), jax.ShapeDtypeStruct((b * nblk * blk, d), BF16)),
        grid=(b, nblk),
        in_specs=[pl.BlockSpec((None, blk, d), lambda bi, i: (bi, jnp.maximum(i - 1, 0), 0)),
                  pl.BlockSpec(meta.shape, lambda bi, i: (0, 0)), pl.BlockSpec((1, d), lambda bi, i: (0, 0))],
        out_specs=(out, out),
        compiler_params=_cparams(("parallel", "arbitrary"), 16 * blk * d * 4),
        name="embed_norm",
    )(x, meta, g.reshape(1, d))


def _resid_norm_body(h_ref, y_ref, gp_ref, gn_ref, ho_ref, xo_ref):
    h = h_ref[...] + _rms(y_ref[...]) * gp_ref[...]
    ho_ref[...] = h
    xo_ref[...] = (_rms(h) * gn_ref[...]).astype(xo_ref.dtype)


def _resid_norm(h, y, g_post, g_next):
    m, d = h.shape
    tm = _divisor(m, 256, BF16_ROWS)
    row = pl.BlockSpec((tm, d), lambda i: (i, 0))
    vec = pl.BlockSpec((1, d), lambda i: (0, 0))
    return pl.pallas_call(
        _resid_norm_body,
        out_shape=(jax.ShapeDtypeStruct((m, d), F32), jax.ShapeDtypeStruct((m, d), BF16)),
        grid=(m // tm,),
        in_specs=[row, row, vec, vec],
        out_specs=(row, row),
        compiler_params=_cparams(("parallel",), 2 * tm * d * 14 + tm * d * 12),
        name="resid_norm",
    )(h, y, g_post.reshape(1, d), g_next.reshape(1, d))


def _mm_body(a_ref, w_ref, o_ref):
    a = a_ref[...].astype(BF16)
    o_ref[...] = jnp.dot(a, w_ref[...].astype(BF16), preferred_element_type=F32).astype(o_ref.dtype)


def _mm_slab_body(a_ref, w_ref, s_ref, o_ref, *, rep):
    a = a_ref[...].astype(BF16)
    acc = jnp.dot(a, w_ref[...].astype(BF16), preferred_element_type=F32)
    s = s_ref[...]
    o_ref[...] = (acc + jnp.concatenate([s] * rep, axis=1)).astype(o_ref.dtype)


def _mm_tiles(m, k, n, ab, wb, ob, mult, budget):
    def est(tm, tn):
        return 2 * (tm * k * ab + k * tn * wb + tm * tn * ob) + tm * tn * 4 + tm * k * 2 + k * tn * 2

    tn_min = min(n, max(mult, SLAB))
    tm = _divisor(m, 1408, BF16_ROWS)
    while True:
        tn = _divisor(n, 1024 if m > 256 else 2048, mult)
        while est(tm, tn) > budget and tn > tn_min:
            tn = _divisor(n, tn - mult, mult)
        if est(tm, tn) <= budget or tm <= BF16_ROWS:
            return tm, tn, est(tm, tn)
        tm = _divisor(m, tm - BF16_ROWS, BF16_ROWS)


def _mm_rope_body(a_ref, w_ref, c_ref, s_ref, o_ref, *, rep, scale):
    a = a_ref[...].astype(BF16)
    acc = jnp.dot(a, w_ref[...].astype(BF16), preferred_element_type=F32)
    c, s = c_ref[...] * scale, s_ref[...] * scale
    for r in range(rep):
        x = acc[:, 2 * r * SLAB:(2 * r + 1) * SLAB]
        xs = acc[:, (2 * r + 1) * SLAB:(2 * r + 2) * SLAB]
        o_ref[:, r * SLAB:(r + 1) * SLAB] = (x * c + xs * s).astype(o_ref.dtype)


def _matmul(a, w, out_dtype, layer=None, slab=None, rope=None, name="matmul"):
    m, k = a.shape
    n = w.shape[-1]
    mult = 2 * SLAB if rope is not None else SLAB if slab is not None else LANES
    period = rope[0][0].shape[0] if rope is not None else 1
    tm, tn, est = _mm_tiles(m if period == 1 else period, k, n, a.dtype.itemsize, w.dtype.itemsize,
                            jnp.dtype(out_dtype).itemsize, mult, 46 << 20)
    n_out, tn_out = (n // 2, tn // 2) if rope is not None else (n, tn)
    if w.ndim == 3:
        w_spec = pl.BlockSpec((None, k, tn), lambda i, j: (layer, 0, j))
    else:
        w_spec = pl.BlockSpec((k, tn), lambda i, j: (0, j))
    in_specs = [pl.BlockSpec((tm, k), lambda i, j: (i, 0)), w_spec]
    args = [a, w]
    body = _mm_body
    if slab is not None:
        in_specs.append(pl.BlockSpec((tm, SLAB), lambda i, j: (i, 0)))
        args.append(slab)
        body = functools.partial(_mm_slab_body, rep=tn // SLAB)
    if rope is not None:
        tables, scale = rope
        period = tables[0].shape[0]
        if period == 1:
            tspec = pl.BlockSpec((1, SLAB), lambda i, j: (0, 0))
        else:
            assert period % tm == 0, (period, tm)
            nt = period // tm
            tspec = pl.BlockSpec((tm, SLAB), lambda i, j: (i % nt, 0))
        c, s1, s2 = tables
        in_specs += [tspec] * 2
        args += [c, s1 + s2]
        body = functools.partial(_mm_rope_body, rep=tn_out // SLAB, scale=scale)
        est += 4 * tm * SLAB * 4
    return pl.pallas_call(
        body,
        out_shape=jax.ShapeDtypeStruct((m, n_out), out_dtype),
        grid=(m // tm, n // tn),
        in_specs=in_specs,
        out_specs=pl.BlockSpec((tm, tn_out), lambda i, j: (i, j)),
        compiler_params=_cparams(("parallel", "parallel"), est),
        name=name,
    )(*args)


def _mm_resid_body(a_ref, w_ref, h_ref, gp_ref, gn_ref, ho_ref, xo_ref, acc_ref, *, nk):
    kk = pl.program_id(1)

    def prod():
        return jnp.dot(a_ref[...].astype(BF16), w_ref[...].astype(BF16), preferred_element_type=F32)

    @pl.when(kk == 0)
    def _():
        acc_ref[...] = prod()

    @pl.when(kk > 0)
    def _():
        acc_ref[...] += prod()

    @pl.when(kk == nk - 1)
    def _():
        h = h_ref[...] + _rms(acc_ref[...]) * gp_ref[...]
        ho_ref[...] = h
        xo_ref[...] = (_rms(h) * gn_ref[...]).astype(xo_ref.dtype)


def _mm_resid_copy_body(a_ref, w_ref, h_ref, gp_ref, gn_ref, ho_ref, xo_ref, wo_ref, acc_ref, *, nk):
    wo_ref[...] = w_ref[...].astype(wo_ref.dtype)
    _mm_resid_body(a_ref, w_ref, h_ref, gp_ref, gn_ref, ho_ref, xo_ref, acc_ref, nk=nk)


def _matmul_resid(a, w, layer, h, g_post, g_next, emit_bf16=False, name="matmul_resid"):
    m, k = a.shape
    n = w.shape[-1]
    ab, wb = a.dtype.itemsize, w.dtype.itemsize
    tk = _divisor(k, (6 << 20) // (n * wb), LANES)
    nk = k // tk

    def est(tm):
        casts = (tk * n * 2 if wb != 2 else 0) + (tm * tk * 2 if ab != 2 else 0) + (2 * tk * n * 2 if emit_bf16 else 0)
        return tm * n * 4 + 2 * (tm * tk * ab + tk * n * wb + 2 * tm * n * 4 + tm * n * 2) + casts

    tm = _divisor(m, 1408, BF16_ROWS)
    while est(tm) > (50 << 20) and tm > BF16_ROWS:
        tm = _divisor(m, tm - BF16_ROWS, BF16_ROWS)
    row = pl.BlockSpec((tm, n), lambda i, kk: (i, 0))
    vec = pl.BlockSpec((1, n), lambda i, kk: (0, 0))
    if w.ndim == 3:
        w_spec = pl.BlockSpec((None, tk, n), lambda i, kk: (layer, kk, 0))
    else:
        w_spec = pl.BlockSpec((tk, n), lambda i, kk: (kk, 0))
    out_shape = [jax.ShapeDtypeStruct((m, n), F32), jax.ShapeDtypeStruct((m, n), BF16)]
    out_specs = [row, row]
    body = _mm_resid_body
    if emit_bf16:
        assert m == tm, (m, tm)
        out_shape.append(jax.ShapeDtypeStruct((k, n), BF16))
        out_specs.append(pl.BlockSpec((tk, n), lambda i, kk: (kk, 0)))
        body = _mm_resid_copy_body
    return pl.pallas_call(
        functools.partial(body, nk=nk),
        out_shape=tuple(out_shape),
        grid=(m // tm, nk),
        in_specs=[pl.BlockSpec((tm, tk), lambda i, kk: (i, kk)), w_spec, row, vec, vec],
        out_specs=tuple(out_specs),
        scratch_shapes=[pltpu.VMEM((tm, n), F32)],
        compiler_params=_cparams(("parallel", "arbitrary"), est(tm)),
        name=name,
    )(a, w, h, g_post.reshape(1, n), g_next.reshape(1, n))


def _rope_tables(pos, d):
    inv = ROPE_BASE ** (-jnp.arange(0, d, 2, dtype=F32) / d)
    ang = pos.astype(F32)[:, None] * inv[None, :]
    return jnp.cos(ang), jnp.sin(ang)


def _slab_rope_tables(pos):
    cos, sin = _rope_tables(pos, MLA_ROPE)
    n, half = cos.shape
    one = jnp.ones((n, MLA_NOPE), F32)
    z_nope = jnp.zeros((n, MLA_NOPE), F32)
    z_half = jnp.zeros((n, half), F32)
    z_tail = jnp.zeros((n, SLAB - MLA_NOPE - MLA_ROPE), F32)
    c = jnp.concatenate([one, cos, cos, z_tail], axis=1)
    s1 = jnp.concatenate([z_nope, -sin, z_half, z_tail], axis=1)
    s2 = jnp.concatenate([z_nope, z_half, sin, z_tail], axis=1)
    return c, s1, s2


def _swapped_slabs(w):
    half = MLA_ROPE // 2
    nope, x1, x2 = w[..., :MLA_NOPE], w[..., MLA_NOPE:MLA_NOPE + half], w[..., MLA_NOPE + half:]
    z_tail = jnp.zeros(w.shape[:-1] + (SLAB - MLA_NOPE - MLA_ROPE,), w.dtype)
    return jnp.concatenate([nope, x1, x2, z_tail, jnp.zeros_like(nope), x2, x1, z_tail], axis=-1)


def _slab_rope(x, c, s1, s2):
    half = MLA_ROPE // 2
    return x * c + pltpu.roll(x, SLAB - half, 1) * s1 + pltpu.roll(x, half, 1) * s2


def _ret_rope(x, cos, sin):
    half = RET_DK // 2
    x1, x2 = x[:, :half], x[:, half:]
    return jnp.concatenate([x1 * cos - x2 * sin, x1 * sin + x2 * cos], axis=1)


def _ret_log_gamma():
    return jnp.log1p(-jnp.power(2.0, -5.0 - jnp.arange(RET_HEADS, dtype=F32)))


def _ret_prompt_body(lg_ref, q_ref, k_ref, v_ref, g_ref, cos_ref, sin_ref, o_ref, so_ref, st_ref, *, pad, n_chunks, hp):
    hg = pl.program_id(1)
    c = pl.program_id(2)
    chunk = q_ref.shape[0]

    @pl.when(c == 0)
    def _():
        st_ref[...] = jnp.zeros_like(st_ref)

    lead = jnp.where(c == 0, float(pad), 0.0)
    cos, sin = cos_ref[...], sin_ref[...]
    n_col = lax.broadcasted_iota(jnp.int32, (chunk, 1), 0).astype(F32)
    n_row = lax.broadcasted_iota(jnp.int32, (1, chunk), 1).astype(F32)
    diff = n_col - n_row
    for i in range(hp):
        lg = lg_ref[hg * hp + i]
        q = _ret_rope(q_ref[:, i * RET_DK:(i + 1) * RET_DK].astype(F32), cos, sin)
        k = _ret_rope(k_ref[:, i * RET_DK:(i + 1) * RET_DK].astype(F32), cos, sin) * (RET_DK ** -0.5)
        v = v_ref[:, i * RET_DV:(i + 1) * RET_DV]
        intra = jnp.where(diff >= 0, jnp.exp(lg * jnp.maximum(diff, 0.0)), 0.0)
        q_dec = jnp.exp(lg * (n_col + 1.0 - lead))
        k_dec = jnp.exp(lg * (chunk - 1.0 - n_col))
        s_dec = jnp.exp(jnp.full((1, 1), lg * (chunk - lead), F32))
        st = st_ref[i]
        scores = lax.dot_general(q.astype(BF16), k.astype(BF16), (((1,), (1,)), ((), ())),
                                 preferred_element_type=F32) * intra
        out = jnp.dot(scores.astype(BF16), v, preferred_element_type=F32)
        out = out + jnp.dot((q * q_dec).astype(BF16), st.astype(BF16), preferred_element_type=F32)
        kt = (k * k_dec).T.astype(BF16)
        st_ref[i] = st * s_dec + jnp.dot(kt, v, preferred_element_type=F32)
        g = g_ref[:, i * RET_DV:(i + 1) * RET_DV].astype(F32)
        o_ref[:, i * RET_DV:(i + 1) * RET_DV] = (_silu(g) * _rms(out)).astype(o_ref.dtype)

    @pl.when(c == n_chunks - 1)
    def _():
        so_ref[...] = st_ref[...]


def _ret_prompt(qkvg, cos, sin, pad):
    b, s, _ = qkvg.shape
    chunk = RET_CHUNK
    nc = s // chunk
    hp = _divisor(RET_HEADS, 8, 1)
    ng = RET_HEADS // hp
    body = functools.partial(_ret_prompt_body, pad=pad, n_chunks=nc, hp=hp)
    return pl.pallas_call(
        body,
        out_shape=(jax.ShapeDtypeStruct((b, s, RET_HEADS * RET_DV), BF16),
                   jax.ShapeDtypeStruct((b, RET_HEADS, RET_DK, RET_DV), F32)),
        grid=(b, ng, nc),
        in_specs=[
            pl.BlockSpec(memory_space=pltpu.SMEM),
            pl.BlockSpec((None, chunk, hp * RET_DK), lambda bi, h, c: (bi, c, h)),
            pl.BlockSpec((None, chunk, hp * RET_DK), lambda bi, h, c: (bi, c, ng + h)),
            pl.BlockSpec((None, chunk, hp * RET_DV), lambda bi, h, c: (bi, c, ng + h)),
            pl.BlockSpec((None, chunk, hp * RET_DV), lambda bi, h, c: (bi, c, 2 * ng + h)),
            pl.BlockSpec((chunk, RET_DK // 2), lambda bi, h, c: (c, 0)),
            pl.BlockSpec((chunk, RET_DK // 2), lambda bi, h, c: (c, 0)),
        ],
        out_specs=(
            pl.BlockSpec((None, chunk, hp * RET_DV), lambda bi, h, c: (bi, c, h)),
            pl.BlockSpec((None, hp, RET_DK, RET_DV), lambda bi, h, c: (bi, h, 0, 0)),
        ),
        scratch_shapes=[pltpu.VMEM((hp, RET_DK, RET_DV), F32)],
        compiler_params=_cparams(("parallel", "parallel", "arbitrary"), 16 << 20),
        name="ret_prompt",
    )(_ret_log_gamma(), qkvg, qkvg, qkvg, qkvg, cos, sin)


def _ret_sample_body(lg_ref, q_ref, k_ref, v_ref, g_ref, cos_ref, sin_ref, st_ref, o_ref, so_ref):
    h = pl.program_id(1)
    bt = q_ref.shape[0]
    gamma = jnp.exp(jnp.full((1, 1), lg_ref[h], F32))
    cos, sin = cos_ref[...], sin_ref[...]
    q = _ret_rope(q_ref[...], cos, sin)
    k = _ret_rope(k_ref[...], cos, sin) * (RET_DK ** -0.5)
    v = v_ref[...]
    qk = jnp.sum(q * k, axis=-1, keepdims=True)
    qg = (q * gamma).astype(BF16)
    eye = lax.broadcasted_iota(jnp.int32, (RET_DK, RET_DK), 0) == lax.broadcasted_iota(jnp.int32, (RET_DK, RET_DK), 1)
    rows = []
    for i in range(bt):
        st = st_ref[i]
        cross = jnp.dot(qg, st.astype(BF16), preferred_element_type=F32)
        rows.append(cross[i:i + 1])
        k_col = jnp.sum(jnp.where(eye, k[i:i + 1], 0.0), axis=1, keepdims=True)
        so_ref[i] = st * gamma + k_col * v[i:i + 1]
    out = qk * v + jnp.concatenate(rows, axis=0)
    o_ref[...] = _silu(g_ref[...]) * _rms(out)


def _ret_sample(qkvg, state, cos, sin, out_buf, layer):
    db = qkvg.shape[0]
    bt = SUBLANES
    kq = RET_HEADS
    n_layers = state.shape[0]
    st_block = (None, bt, None, RET_DK, RET_DV)
    in_specs = [
        pl.BlockSpec(memory_space=pltpu.SMEM),
        pl.BlockSpec((bt, RET_DK), lambda bi, h: (bi, h)),
        pl.BlockSpec((bt, RET_DK), lambda bi, h: (bi, kq + h)),
        pl.BlockSpec((bt, RET_DV), lambda bi, h: (bi, kq + h)),
        pl.BlockSpec((bt, RET_DV), lambda bi, h: (bi, 2 * kq + h)),
        pl.BlockSpec((1, RET_DK // 2), lambda bi, h: (0, 0)),
        pl.BlockSpec((1, RET_DK // 2), lambda bi, h: (0, 0)),
        pl.BlockSpec(st_block, lambda bi, h: (layer, bi, h, 0, 0)),
    ]
    args = [_ret_log_gamma(), qkvg, qkvg, qkvg, qkvg, cos, sin, state]
    aliases = {}
    body = _ret_sample_body
    if out_buf is not None:
        in_specs.append(pl.BlockSpec(memory_space=pl.ANY))
        args.append(out_buf)
        aliases = {len(args) - 1: 1}
        body = lambda *refs: _ret_sample_body(*refs[:8], *refs[9:])
    return pl.pallas_call(
        body,
        out_shape=(jax.ShapeDtypeStruct((db, RET_HEADS * RET_DV), F32),
                   jax.ShapeDtypeStruct((n_layers, db, RET_HEADS, RET_DK, RET_DV), F32)),
        grid=(db // bt, RET_HEADS),
        in_specs=in_specs,
        out_specs=(
            pl.BlockSpec((bt, RET_DV), lambda bi, h: (bi, h)),
            pl.BlockSpec(st_block, lambda bi, h: (layer, bi, h, 0, 0)),
        ),
        input_output_aliases=aliases,
        compiler_params=_cparams(("parallel", "parallel"), 4 * bt * RET_DK * RET_DV * 4 + (4 << 20)),
        name="ret_sample",
    )(*args)


def _conv_act(g2, g1, g0, up, cw_ref, cb_ref):
    conv = g2 * cw_ref[0:1, :] + g1 * cw_ref[1:2, :] + g0 * cw_ref[2:3, :] + cb_ref[...]
    return _silu(conv) * up


def _ffn1_seq_body(x_ref, wg_ref, wu_ref, cw_ref, cb_ref, act_ref, cs_ref, carry_ref):
    i = pl.program_id(1)
    j = pl.program_id(2)
    x = x_ref[...]
    gate = jnp.dot(x, wg_ref[...].astype(BF16), preferred_element_type=F32)
    up = jnp.dot(x, wu_ref[...].astype(BF16), preferred_element_type=F32)
    tm = gate.shape[0]
    head = 2 * SUBLANES
    act = _conv_act(pltpu.roll(gate, 2, 0), pltpu.roll(gate, 1, 0), gate, up, cw_ref, cb_ref)
    act_ref[head:, :] = act[head:].astype(act_ref.dtype)
    @pl.when(i == 0)
    def _():
        carry_ref[j] = jnp.zeros(carry_ref.shape[1:], F32)

    win = jnp.concatenate([carry_ref[j], gate[:head]], axis=0)
    w1 = pltpu.roll(win, 1, 0)[SUBLANES:]
    w2 = pltpu.roll(win, 2, 0)[SUBLANES:]
    act_ref[:head, :] = _conv_act(w2, w1, gate[:head], up[:head], cw_ref, cb_ref).astype(act_ref.dtype)
    tail = gate[tm - SUBLANES:]
    carry_ref[j] = tail
    cs_ref[...] = tail


def _ffn1_seq(x, w_gate, w_up, conv_w, conv_b, layer, n_seq):
    m, k = x.shape
    f = w_gate.shape[-1]
    s = m // n_seq
    tm = _divisor(s, 1408, BF16_ROWS)
    tn = _divisor(f, 512, LANES)
    ni, nj = s // tm, f // tn
    est = 2 * (tm * k * 2 + 2 * k * tn * 4 + tm * tn * 2) + 2 * k * tn * 2 + 4 * tm * tn * 4
    act, tails = pl.pallas_call(
        _ffn1_seq_body,
        out_shape=(jax.ShapeDtypeStruct((m, f), BF16), jax.ShapeDtypeStruct((n_seq, ni, SUBLANES, f), F32)),
        grid=(n_seq, ni, nj),
        in_specs=[
            pl.BlockSpec((tm, k), lambda b, i, j: (b * ni + i, 0)),
            pl.BlockSpec((None, k, tn), lambda b, i, j: (layer, 0, j)),
            pl.BlockSpec((None, k, tn), lambda b, i, j: (layer, 0, j)),
            pl.BlockSpec((None, CONV_W, tn), lambda b, i, j: (layer, 0, j)),
            pl.BlockSpec((None, 1, tn), lambda b, i, j: (layer, 0, j)),
        ],
        out_specs=(
            pl.BlockSpec((tm, tn), lambda b, i, j: (b * ni + i, j)),
            pl.BlockSpec((None, None, SUBLANES, tn), lambda b, i, j: (b, i, 0, j)),
        ),
        scratch_shapes=[pltpu.VMEM((nj, SUBLANES, tn), F32)],
        compiler_params=_cparams(("arbitrary", "arbitrary", "arbitrary"), est),
        name="ffn1_seq",
    )(x, w_gate, w_up, conv_w, conv_b.reshape(conv_b.shape[0], 1, f))
    return act, tails[:, ni - 1]


def _ffn1_tok_body(x_ref, wg_ref, wu_ref, cw_ref, cb_ref, s0_ref, s1_ref, act_ref, gate_ref):
    x = x_ref[...]
    gate = jnp.dot(x, wg_ref[...].astype(BF16), preferred_element_type=F32)
    up = jnp.dot(x, wu_ref[...].astype(BF16), preferred_element_type=F32)
    act_ref[...] = _conv_act(s0_ref[...], s1_ref[...], gate, up, cw_ref, cb_ref).astype(act_ref.dtype)
    gate_ref[...] = gate


def _ffn1_tok(x, w_gate, w_up, conv_w, conv_b, layer, conv_state):
    m, k = x.shape
    f = w_gate.shape[-1]
    tn = _divisor(f, 1024, LANES)
    nj = f // tn
    cs = conv_state.reshape(m, (CONV_W - 1) * f)
    est = 2 * (m * k * 2 + 2 * k * tn * 4 + 6 * m * tn * 4) + 2 * k * tn * 2
    return pl.pallas_call(
        _ffn1_tok_body,
        out_shape=(jax.ShapeDtypeStruct((m, f), BF16), jax.ShapeDtypeStruct((m, f), F32)),
        grid=(nj,),
        in_specs=[
            pl.BlockSpec((m, k), lambda j: (0, 0)),
            pl.BlockSpec((None, k, tn), lambda j: (layer, 0, j)),
            pl.BlockSpec((None, k, tn), lambda j: (layer, 0, j)),
            pl.BlockSpec((None, CONV_W, tn), lambda j: (layer, 0, j)),
            pl.BlockSpec((None, 1, tn), lambda j: (layer, 0, j)),
            pl.BlockSpec((m, tn), lambda j: (0, j)),
            pl.BlockSpec((m, tn), lambda j: (0, nj + j)),
        ],
        out_specs=(pl.BlockSpec((m, tn), lambda j: (0, j)), pl.BlockSpec((m, tn), lambda j: (0, j))),
        compiler_params=_cparams(("parallel",), est),
        name="ffn1_tok",
    )(x, w_gate, w_up, conv_w, conv_b.reshape(conv_b.shape[0], 1, f), cs, cs)


def _kv_post_body(y_ref, g_ref, c_ref, s1_ref, s2_ref, cf_ref, cb_ref, r_ref):
    y = y_ref[...]
    c = _rms(y[:, :KV_LORA]) * g_ref[...]
    cf_ref[...] = c
    cb_ref[...] = c.astype(cb_ref.dtype)
    r_ref[...] = _slab_rope(y[:, KV_LORA:], c_ref[...], s1_ref[...], s2_ref[...])


def _kv_post(y, g, tables):
    m = y.shape[0]
    t_rows = tables[0].shape[0]
    tm = _divisor(m if t_rows == 1 else t_rows, 512, BF16_ROWS)
    if t_rows == 1:
        tspec = pl.BlockSpec((1, SLAB), lambda i: (0, 0))
    else:
        nt = t_rows // tm
        tspec = pl.BlockSpec((tm, SLAB), lambda i: (i % nt, 0))
    return pl.pallas_call(
        _kv_post_body,
        out_shape=(jax.ShapeDtypeStruct((m, KV_LORA), F32), jax.ShapeDtypeStruct((m, KV_LORA), BF16),
                   jax.ShapeDtypeStruct((m, SLAB), F32)),
        grid=(m // tm,),
        in_specs=[pl.BlockSpec((tm, KV_LORA + SLAB), lambda i: (i, 0)), pl.BlockSpec((1, KV_LORA), lambda i: (0, 0)),
                  tspec, tspec, tspec],
        out_specs=(pl.BlockSpec((tm, KV_LORA), lambda i: (i, 0)), pl.BlockSpec((tm, KV_LORA), lambda i: (i, 0)),
                   pl.BlockSpec((tm, SLAB), lambda i: (i, 0))),
        compiler_params=_cparams(("parallel",), 16 << 20),
        name="kv_post",
    )(y, g.reshape(1, KV_LORA), *tables)


ATTN_BLOCK = 3 * LANES


def _v_up_t_body(c_ref, w_ref, o_ref):
    o_ref[...] = lax.dot_general(w_ref[...].astype(BF16), c_ref[...], (((1,), (1,)), ((), ())),
                                 preferred_element_type=F32).astype(o_ref.dtype)


def _v_up_t(c, w_uv_t, n_seq, tk):
    m, kl = c.shape
    n = w_uv_t.shape[0]
    nk = m // n_seq // tk
    tn = _divisor(n, 1024, LANES)
    return pl.pallas_call(
        _v_up_t_body,
        out_shape=jax.ShapeDtypeStruct((n_seq, nk, n, tk), BF16),
        grid=(n_seq, nk, n // tn),
        in_specs=[pl.BlockSpec((tk, kl), lambda b, kb, j: (b * nk + kb, 0)), pl.BlockSpec((tn, kl), lambda b, kb, j: (j, 0))],
        out_specs=pl.BlockSpec((None, None, tn, tk), lambda b, kb, j: (b, kb, j, 0)),
        compiler_params=_cparams(("parallel", "parallel", "parallel"), 16 << 20),
        name="v_up_t",
    )(c, w_uv_t)


def _attn_prompt_body(q_ref, k_ref, vt_ref, bias_ref, o_ref, m_ref, l_ref, acc_ref, s_ref, *, pad, blk, hp):
    qi = pl.program_id(2)
    m_ref[...] = jnp.full_like(m_ref, NEG)
    l_ref[...] = jnp.zeros_like(l_ref)
    acc_ref[...] = jnp.zeros_like(acc_ref)
    nt = (((1,), (1,)), ((), ()))

    def scores(ki, slot):
        start = ki * blk if isinstance(ki, int) else pl.multiple_of(ki * blk, blk)
        for i in range(hp):
            kb = k_ref[pl.ds(start, blk), i * SLAB:(i + 1) * SLAB]
            s_ref[slot, i] = lax.dot_general(kb, q_ref[:, i * SLAB:(i + 1) * SLAB], nt, preferred_element_type=F32)

    def consume(ki, slot, kind):
        for i in range(hp):
            s = s_ref[slot, i]
            if kind == "first":
                s = s[pad:]
            elif kind == "diag":
                s = s + bias_ref[jnp.where(qi == 0, 0, 1)]
            m_old = m_ref[i]
            m_new = jnp.maximum(m_old, jnp.max(s, axis=0, keepdims=True))
            alpha = jnp.exp(m_old - m_new)
            p = jnp.exp(s - m_new)
            l_ref[i] = alpha * l_ref[i] + jnp.sum(p, axis=0, keepdims=True)
            pb = p.astype(BF16)
            if kind == "first":
                pb = jnp.concatenate([jnp.zeros((pad, blk), BF16), pb], axis=0)
            vt = vt_ref[ki, i * MLA_V:(i + 1) * MLA_V, :]
            acc_ref[i] = alpha * acc_ref[i] + jnp.dot(vt, pb, preferred_element_type=F32)
            m_ref[i] = m_new

    scores(0, 0)

    @pl.when(qi > 0)
    def _():
        scores(1, 1)
        consume(0, 0, "first")

    def mid_pair(kp, carry):
        ki = 2 * kp + 1
        consume(ki, 1, "full")
        scores(ki + 1, 0)
        consume(ki + 1, 0, "full")
        scores(ki + 2, 1)
        return carry

    n_mid = qi - 1
    lax.fori_loop(0, n_mid // 2, mid_pair, 0)

    @pl.when(lax.rem(n_mid, 2) == 1)
    def _():
        consume(qi - 1, 1, "full")
        scores(qi, 0)

    consume(qi, lax.rem(qi, 2), "diag")

    for i in range(hp):
        o_ref[:, i * MLA_V:(i + 1) * MLA_V] = (acc_ref[i] / l_ref[i]).T.astype(o_ref.dtype)


def _attn_bias(blk, pad):
    kpos = jnp.arange(blk)[:, None]
    qpos = jnp.arange(blk)[None, :]
    causal = kpos <= qpos
    first = causal & ((kpos >= pad) | (qpos < pad))
    return jnp.stack([jnp.where(first, 0.0, NEG), jnp.where(causal, 0.0, NEG)]).astype(F32)


def _attn_prompt(q, k, vt, pad):
    b, s, _ = q.shape
    nk, blk = vt.shape[1], vt.shape[3]
    hp = _divisor(MLA_HEADS, 4, 1)
    body = functools.partial(_attn_prompt_body, pad=pad, blk=blk, hp=hp)
    est = (2 * (blk * hp * SLAB * 2 + s * hp * SLAB * 2 + s * hp * MLA_V * 2 + 2 * blk * blk * 4 + blk * hp * MLA_V * 2)
           + 8 * hp * blk * blk * 4)
    return pl.pallas_call(
        body,
        out_shape=jax.ShapeDtypeStruct((b, s, MLA_HEADS * MLA_V), BF16),
        grid=(b, MLA_HEADS // hp, s // blk),
        in_specs=[
            pl.BlockSpec((None, blk, hp * SLAB), lambda bi, h, qi: (bi, qi, h)),
            pl.BlockSpec((None, s, hp * SLAB), lambda bi, h, qi: (bi, 0, h)),
            pl.BlockSpec((None, nk, hp * MLA_V, blk), lambda bi, h, qi: (bi, 0, h, 0)),
            pl.BlockSpec((2, blk, blk), lambda bi, h, qi: (0, 0, 0)),
        ],
        out_specs=pl.BlockSpec((None, blk, hp * MLA_V), lambda bi, h, qi: (bi, qi, h)),
        scratch_shapes=[pltpu.VMEM((hp, 1, blk), F32), pltpu.VMEM((hp, 1, blk), F32), pltpu.VMEM((hp, MLA_V, blk), F32),
                        pltpu.VMEM((2, hp, blk, blk), F32)],
        compiler_params=_cparams(("parallel", "parallel", "arbitrary"), est),
        name="attn_prompt",
    )(q, k, vt, _attn_bias(blk, pad))


def _q_absorb_body(q_ref, w_ref, ql_ref):
    ql_ref[...] = lax.dot_general(q_ref[...][:, :MLA_NOPE].astype(BF16), w_ref[...].astype(BF16), (((1,), (1,)), ((), ())),
                                  preferred_element_type=F32)


def _q_absorb(q, w_uk2):
    db = q.shape[0]
    return pl.pallas_call(
        _q_absorb_body,
        out_shape=jax.ShapeDtypeStruct((db, MLA_HEADS * KV_LORA), F32),
        grid=(MLA_HEADS,),
        in_specs=[pl.BlockSpec((db, SLAB), lambda h: (0, h)), pl.BlockSpec((KV_LORA, MLA_NOPE), lambda h: (0, h))],
        out_specs=pl.BlockSpec((db, KV_LORA), lambda h: (0, h)),
        compiler_params=_cparams(("parallel",), 8 << 20),
        name="q_absorb",
    )(q, w_uk2)


def _page_copies(pt_ref, lat_hbm, rope_hbm, latf_ref, ropef_ref, sem_ref, bi, g, slot, *, nb, pages):
    page = lat_hbm.shape[1]
    out = []
    for b in range(nb):
        for i in range(pages):
            pid = pt_ref[bi, g, b * pages + i]
            rows = pl.ds(i * page, page)
            out.append(pltpu.make_async_copy(lat_hbm.at[pid], latf_ref.at[slot, b, rows, :], sem_ref.at[slot]))
            out.append(pltpu.make_async_copy(rope_hbm.at[pid], ropef_ref.at[slot, b, :, rows], sem_ref.at[slot]))
    return out


def _attn_sample_body(pt_ref, ql_ref, qs_ref, cn_ref, rn_ref, lat_hbm, rope_hbm, o_ref, m_ref, l_ref, acc_ref, latb_ref,
                      ropeb_ref, latf_ref, ropef_ref, sem_ref, *, pages, n_groups, nb, n_steps):
    bi = pl.program_id(0)
    g = pl.program_id(1)
    step = bi * n_groups + g
    slot = lax.rem(step, 2)
    copies = functools.partial(_page_copies, pt_ref, lat_hbm, rope_hbm, latf_ref, ropef_ref, sem_ref, nb=nb, pages=pages)

    @pl.when(step == 0)
    def _():
        for c in copies(bi, g, slot):
            c.start()

    @pl.when(step + 1 < n_steps)
    def _():
        nxt = step + 1
        for c in copies(lax.div(nxt, n_groups), lax.rem(nxt, n_groups), 1 - slot):
            c.start()

    for c in copies(bi, g, slot):
        c.wait()

    @pl.when(g == 0)
    def _():
        m_ref[...] = jnp.full_like(m_ref, NEG)
        l_ref[...] = jnp.zeros_like(l_ref)
        acc_ref[...] = jnp.zeros_like(acc_ref)

    nt = (((1,), (1,)), ((), ()))
    for b in range(nb):
        latb_ref[b] = latf_ref[slot, b].astype(BF16)
        ropeb_ref[b] = ropef_ref[slot, b].astype(BF16)
        ql = ql_ref[b]
        qr = qs_ref[b][:, MLA_NOPE:MLA_NOPE + MLA_ROPE]
        s = lax.dot_general(ql.astype(BF16), latb_ref[b], nt, preferred_element_type=F32)
        s = s + jnp.dot(qr.astype(BF16), ropeb_ref[b], preferred_element_type=F32)
        m_old = m_ref[b]
        m_new = jnp.maximum(m_old, jnp.max(s, axis=-1, keepdims=True))
        alpha = jnp.exp(m_old - m_new)
        p = jnp.exp(s - m_new)
        l_ref[b] = alpha * l_ref[b] + jnp.sum(p, axis=-1, keepdims=True)
        acc_ref[b] = alpha * acc_ref[b] + jnp.dot(p.astype(BF16), latb_ref[b], preferred_element_type=F32)
        m_ref[b] = m_new

    @pl.when(g == n_groups - 1)
    def _():
        for b in range(nb):
            ql = ql_ref[b]
            qr = qs_ref[b][:, MLA_NOPE:MLA_NOPE + MLA_ROPE]
            cn = cn_ref[b]
            rn = rn_ref[b][:, MLA_NOPE:MLA_NOPE + MLA_ROPE]
            s_new = jnp.sum(ql * cn, axis=-1, keepdims=True) + jnp.sum(qr * rn, axis=-1, keepdims=True)
            m_old = m_ref[b]
            m_new = jnp.maximum(m_old, s_new)
            alpha = jnp.exp(m_old - m_new)
            p_new = jnp.exp(s_new - m_new)
            l = alpha * l_ref[b] + p_new
            o_ref[b] = (alpha * acc_ref[b] + p_new * cn) / l


def _attn_sample(page_table, cache_lat, cache_rope_t, q_lat, q_slab, c_new, r_new):
    db, n_pages = page_table.shape
    page = cache_lat.shape[1]
    pages = _divisor(n_pages, 16, 1)
    nb = _divisor(db, 4, 1)
    n_groups = n_pages // pages
    n_steps = (db // nb) * n_groups
    body = functools.partial(_attn_sample_body, pages=pages, n_groups=n_groups, nb=nb, n_steps=n_steps)

    page_table = page_table.reshape(db // nb, nb, n_groups, pages).transpose(0, 2, 1, 3).reshape(db // nb, n_groups, nb * pages)

    def seq_spec(rows, width):
        return pl.BlockSpec((nb, rows, width), lambda bi, g, pt: (bi, 0, 0))

    hbm = pl.BlockSpec(memory_space=pl.ANY)
    keys = pages * page
    est = nb * keys * (KV_LORA + MLA_ROPE) * (2 * 4 + 2) + (2 << 20)
    return pl.pallas_call(
        body,
        out_shape=jax.ShapeDtypeStruct((db, MLA_HEADS, KV_LORA), F32),
        grid_spec=pltpu.PrefetchScalarGridSpec(
            num_scalar_prefetch=1,
            grid=(db // nb, n_groups),
            in_specs=[seq_spec(MLA_HEADS, KV_LORA), seq_spec(MLA_HEADS, SLAB), seq_spec(1, KV_LORA), seq_spec(1, SLAB),
                      hbm, hbm],
            out_specs=seq_spec(MLA_HEADS, KV_LORA),
            scratch_shapes=[
                pltpu.VMEM((nb, MLA_HEADS, 1), F32), pltpu.VMEM((nb, MLA_HEADS, 1), F32),
                pltpu.VMEM((nb, MLA_HEADS, KV_LORA), F32),
                pltpu.VMEM((nb, keys, KV_LORA), BF16), pltpu.VMEM((nb, MLA_ROPE, keys), BF16),
                pltpu.VMEM((2, nb, keys, KV_LORA), F32), pltpu.VMEM((2, nb, MLA_ROPE, keys), F32),
                pltpu.SemaphoreType.DMA((2,)),
            ],
        ),
        compiler_params=_cparams(("arbitrary", "arbitrary"), est),
        name="attn_sample",
    )(page_table, q_lat, q_slab, c_new, r_new, cache_lat, cache_rope_t)


def _head_out_body(o_ref, w_ref, y_ref):
    y_ref[...] = jnp.dot(o_ref[...].astype(BF16), w_ref[...].astype(BF16), preferred_element_type=F32).astype(y_ref.dtype)


def _head_out(o_lat, w_uv2):
    db = o_lat.shape[0]
    return pl.pallas_call(
        _head_out_body,
        out_shape=jax.ShapeDtypeStruct((db, MLA_HEADS * MLA_V), BF16),
        grid=(MLA_HEADS,),
        in_specs=[pl.BlockSpec((db, KV_LORA), lambda h: (0, h)), pl.BlockSpec((KV_LORA, MLA_V), lambda h: (0, h))],
        out_specs=pl.BlockSpec((db, MLA_V), lambda h: (0, h)),
        compiler_params=_cparams(("parallel",), 4 << 20),
        name="head_out",
    )(o_lat, w_uv2)


def _prep_weights(ret_w_qkvg, ret_w_o, ffn_w_gate, ffn_w_up, ffn_w_down, w_dkv, w_kr, w_uk, w_uv, mla_w_dq, mla_w_uq,
                  mla_w_o):
    tail = SLAB - MLA_NOPE - MLA_ROPE
    w_kv = jnp.concatenate([w_dkv, jnp.zeros((D_MODEL, MLA_NOPE), F32), w_kr, jnp.zeros((D_MODEL, tail), F32)], axis=1)
    w_uk_ext = jnp.pad(w_uk, ((0, 0), (0, 0), (0, SLAB - MLA_NOPE))).reshape(KV_LORA, MLA_HEADS * SLAB)
    n_b = mla_w_uq.shape[0]
    w_uq_ext = _swapped_slabs(mla_w_uq.reshape(n_b, Q_LORA, MLA_HEADS, MLA_NOPE + MLA_ROPE)).reshape(n_b, Q_LORA, -1)
    w_uv2 = w_uv.reshape(KV_LORA, MLA_HEADS * MLA_V)
    return dict(
        qkvg=ret_w_qkvg, ret_o=ret_w_o, gate=ffn_w_gate, up=ffn_w_up, down=ffn_w_down, kv=w_kv, uk_ext=w_uk_ext,
        uk2=w_uk.reshape(KV_LORA, MLA_HEADS * MLA_NOPE), uv2=w_uv2, uv_t=w_uv2.T, dq=mla_w_dq, uq_ext=w_uq_ext,
        mla_o=mla_w_o,
    )


def _trunk(h, xn, w, w16, norm_g, ffn_conv_w, ffn_conv_b, kv_in_g, kv_norm_g, mla_q_norm_g, ret_mixer, ffn1, kv_tables,
           attend):
    conv_states = []
    c_f32 = r_slab = kv_ctx = None
    q_scale = (MLA_NOPE + MLA_ROPE) ** -0.5

    def resid(key, idx, a, h, g_post, g_next):
        if (key, idx) in w16:
            return _matmul_resid(a, w16[key, idx], None, h, g_post, g_next, name=key)
        h, xn, w16[key, idx] = _matmul_resid(a, w[key], idx, h, g_post, g_next, emit_bf16=True, name=key)
        return h, xn

    for layer in range(DEPTH):
        g = norm_g[layer]
        if layer < N_A_LAYERS:
            qkvg = _matmul(xn, w["qkvg"], ret_mixer.qkvg_dtype, layer=layer, name="qkvg")
            gated = ret_mixer(layer, qkvg)
            h, xn = resid("ret_o", layer, gated, h, g[1], g[2])
        else:
            j = layer - N_A_LAYERS
            cq = _matmul(xn, w["dq"], F32, layer=j, name="dq")
            cqn = _rms_cast(cq, mla_q_norm_g[j])
            q = _matmul(cqn, w["uq_ext"], attend.q_dtype, layer=j, rope=(kv_tables, q_scale), name="uq")
            o = attend(q, kv_ctx)
            h, xn = resid("mla_o", j, o, h, g[1], g[2])
        act, cs = ffn1(layer, xn, w["gate"], w["up"], ffn_conv_w, ffn_conv_b)
        conv_states.append(cs)
        g_next = norm_g[layer + 1, 0] if layer + 1 < DEPTH else g[3]
        h, xn = resid("down", layer, act, h, g[3], g_next)
        if layer == N_A_LAYERS - 1:
            hn = _rms_cast(h, kv_in_g)
            y = _matmul(hn, w["kv"], F32, name="kv_down")
            c_f32, c_bf16, r_slab = _kv_post(y, kv_norm_g, kv_tables)
            kv_ctx = (c_f32, c_bf16, r_slab)
    return h, conv_states, c_f32, r_slab


class _PromptRetention:
    qkvg_dtype = BF16

    def __init__(self, n_seq, pad, cos, sin):
        self.n_seq, self.pad, self.cos, self.sin = n_seq, pad, cos, sin
        self.states = []

    def __call__(self, layer, qkvg):
        m, n = qkvg.shape
        gated, state = _ret_prompt(qkvg.reshape(self.n_seq, m // self.n_seq, n), self.cos, self.sin, self.pad)
        self.states.append(state)
        return gated.reshape(m, -1)


class _SampleRetention:
    qkvg_dtype = F32

    def __init__(self, state, cos, sin):
        self.state, self.cos, self.sin = state, cos, sin
        self.out = None

    def __call__(self, layer, qkvg):
        gated, self.out = _ret_sample(qkvg, self.state, self.cos, self.sin, self.out, layer)
        return gated


def kernel(x_prompt, x_sample, state_retention, state_conv, cache_kv_latent, cache_k_rope, page_table, meta_tokens, norm_g,
           ret_w_qkvg, ret_w_o, ffn_w_gate, ffn_w_up, ffn_w_down, ffn_conv_w, ffn_conv_b, kv_in_g, w_dkv, kv_norm_g, w_kr,
           w_uk, w_uv, mla_w_dq, mla_q_norm_g, mla_w_uq, mla_w_o):
    w = _prep_weights(ret_w_qkvg, ret_w_o, ffn_w_gate, ffn_w_up, ffn_w_down, w_dkv, w_kr, w_uk, w_uv, mla_w_dq, mla_w_uq,
                      mla_w_o)
    shared = (norm_g, ffn_conv_w, ffn_conv_b, kv_in_g, kv_norm_g, mla_q_norm_g)

    b, seq, d = x_prompt.shape
    pad = RET_CHUNK - N_META
    s_pad = pad + N_META + seq
    h0, xn0 = _embed_norm(x_prompt, meta_tokens, norm_g[0, 0], pad)
    pos_p = jnp.arange(s_pad) - pad
    cos_p, sin_p = _rope_tables(pos_p, RET_DK)
    tables_p = _slab_rope_tables(pos_p)
    ret_p = _PromptRetention(b, pad, cos_p, sin_p)

    def ffn1_p(layer, xn, wg, wu, cw, cb):
        act, tail = _ffn1_seq(xn, wg, wu, cw, cb, layer, b)
        return act, tail[:, SUBLANES - (CONV_W - 1):, :]

    def attend_p(q, kv_ctx):
        if "k" not in attend_p.cache:
            _, c_bf16, r_slab = kv_ctx
            attend_p.cache["k"] = _matmul(c_bf16, w["uk_ext"], BF16, slab=r_slab, name="k_up").reshape(b, s_pad, -1)
            attend_p.cache["vt"] = _v_up_t(c_bf16, w["uv_t"], b, _divisor(s_pad, ATTN_BLOCK, LANES))
        o = _attn_prompt(q.reshape(b, s_pad, -1), attend_p.cache["k"], attend_p.cache["vt"], pad)
        return o.reshape(b * s_pad, -1)

    attend_p.cache = {}
    attend_p.q_dtype = BF16

    db = x_sample.shape[0]
    pos_s = jnp.full((1,), PAST_LEN)
    cos_s, sin_s = _rope_tables(pos_s, RET_DK)
    tables_s = _slab_rope_tables(pos_s)
    ret_s = _SampleRetention(state_retention, cos_s, sin_s)
    cache_rope_t = jnp.swapaxes(cache_k_rope, 1, 2)

    def ffn1_s(layer, xn, wg, wu, cw, cb):
        act, gate = _ffn1_tok(xn, wg, wu, cw, cb, layer, state_conv[layer])
        return act, jnp.stack([state_conv[layer][:, 1], gate], axis=1)

    def attend_s(q, kv_ctx):
        c_f32, _, r_slab = kv_ctx
        q_lat = _q_absorb(q, w["uk2"])
        o_lat = _attn_sample(page_table, cache_kv_latent, cache_rope_t, q_lat.reshape(db, MLA_HEADS, KV_LORA),
                             q.reshape(db, MLA_HEADS, SLAB), c_f32.reshape(db, 1, KV_LORA), r_slab.reshape(db, 1, SLAB))
        return _head_out(o_lat.reshape(db, MLA_HEADS * KV_LORA), w["uv2"])

    attend_s.q_dtype = F32

    w16 = {}
    hs0 = x_sample.reshape(db, d)
    h_s, conv_s, lat_s, rslab_s = _trunk(hs0, _rms_cast(hs0, norm_g[0, 0]), w, w16, *shared, ret_s, ffn1_s, tables_s,
                                         attend_s)
    h_p, conv_p, lat_p, rslab_p = _trunk(h0, xn0, w, w16, *shared, ret_p, ffn1_p, tables_p, attend_p)
    y_prompt = h_p.reshape(b, s_pad, d)[:, pad + N_META:]
    lat_prompt = lat_p.reshape(b, s_pad, KV_LORA)[:, pad:]
    rope_prompt = rslab_p.reshape(b, s_pad, SLAB)[:, pad:, MLA_NOPE:MLA_NOPE + MLA_ROPE]

    return (
        y_prompt,
        h_s.reshape(db, 1, d),
        jnp.stack(ret_p.states),
        ret_s.out,
        jnp.stack(conv_p),
        jnp.stack(conv_s),
        lat_prompt,
        lat_s.reshape(db, 1, KV_LORA),
        rope_prompt,
        rslab_s[:, MLA_NOPE:MLA_NOPE + MLA_ROPE].reshape(db, 1, MLA_ROPE),
    )
```
